```python
import jax, jax.numpy as jnp
from jax import lax
import numpy as np


D_MODEL = 4096
BATCH = 8
SEQ = 4096
DEPTH = 4

N_MIXERS = 4
MLA_HEADS = 32
QK_NOPE_DIM = 128
QK_ROPE_DIM = 64
V_HEAD_DIM = 128
Q_LORA_RANK = 1024
KV_LORA_RANK = 512
ROPE_THETA = 10000.0
Q_BLOCK = 128
SC_WIDTH = D_MODEL
SC_CONV_WIDTH = 3
GM_WIDTH = D_MODEL
GM_GROUPS = 8
GM_CHUNK = 128
CF_WIDTH = D_MODEL
CF_CONV_WIDTH = 31
NORM_EPS = 1e-6
LN_EPS = 1e-5
N_MLA_LAYERS = len(range(0, DEPTH, N_MIXERS))
N_SC_LAYERS = len(range(1, DEPTH, N_MIXERS))
N_GM_LAYERS = len(range(2, DEPTH, N_MIXERS))
N_CF_LAYERS = len(range(3, DEPTH, N_MIXERS))

kernel_name = "hybrid_interleaved_mla_conv_gmlp_conformer"


def rms_norm(x, g):
    xf = x.astype(jnp.float32)
    y = xf * lax.rsqrt(jnp.mean(xf * xf, axis=-1, keepdims=True) + NORM_EPS)
    return (y * g.astype(jnp.float32)).astype(x.dtype)


def layer_norm(x, g, b):
    xf = x.astype(jnp.float32)
    mu = jnp.mean(xf, axis=-1, keepdims=True)
    var = jnp.mean(jnp.square(xf - mu), axis=-1, keepdims=True)
    y = (xf - mu) * lax.rsqrt(var + LN_EPS) * g.astype(jnp.float32) + b.astype(jnp.float32)
    return y.astype(x.dtype)


def causal_depthwise_conv(x, w):
    width = w.shape[0]
    return lax.conv_general_dilated(
        x, w[:, None, :].astype(x.dtype), window_strides=(1,), padding=[(width - 1, 0)],
        dimension_numbers=("NWC", "WIO", "NWC"), feature_group_count=x.shape[-1])


def apply_rope(t, positions):
    half = QK_ROPE_DIM // 2
    inv_freq = ROPE_THETA ** (-jnp.arange(half, dtype=jnp.float32) / half)
    ang = positions.astype(jnp.float32)[..., None] * inv_freq
    cos, sin = jnp.cos(ang), jnp.sin(ang)
    if t.ndim == 4:
        cos, sin = cos[:, :, None, :], sin[:, :, None, :]
    tf = t.astype(jnp.float32)
    t1, t2 = tf[..., :half], tf[..., half:]
    return jnp.concatenate([t1 * cos - t2 * sin, t2 * cos + t1 * sin], axis=-1).astype(t.dtype)


def causal_block_attention(q, k, v):
    B, S, H, Dqk = q.shape
    n_blocks = S // Q_BLOCK
    scale = Dqk ** -0.5
    q_blocks = q.reshape(B, n_blocks, Q_BLOCK, H, Dqk).transpose(1, 0, 2, 3, 4)
    key_pos = jnp.arange(S)

    def one_block(args):
        q_blk, blk = args
        s = jnp.einsum("bqhd,bkhd->bhqk", q_blk, k, preferred_element_type=jnp.float32) * scale
        q_pos = blk * Q_BLOCK + jnp.arange(Q_BLOCK)
        s = jnp.where(key_pos[None, :] <= q_pos[:, None], s, -1e30)
        p = jax.nn.softmax(s, axis=-1)
        return jnp.einsum("bhqk,bkhd->bqhd", p.astype(v.dtype), v)

    o = lax.map(one_block, (q_blocks, jnp.arange(n_blocks)))
    return o.transpose(1, 0, 2, 3, 4).reshape(B, S, H * v.shape[-1])


def mla_mixer(h, positions, w_in, q_norm, w_uq, kv_norm, w_ukv, w_out):
    B, S, _ = h.shape
    c1 = Q_LORA_RANK
    c2 = c1 + KV_LORA_RANK
    c3 = c2 + QK_ROPE_DIM
    cq, ckv, k_rope, z = jnp.split(h @ w_in, [c1, c2, c3], axis=-1)
    q = (rms_norm(cq, q_norm) @ w_uq).reshape(B, S, MLA_HEADS, QK_NOPE_DIM + QK_ROPE_DIM)
    kv = (rms_norm(ckv, kv_norm) @ w_ukv).reshape(B, S, MLA_HEADS, QK_NOPE_DIM + V_HEAD_DIM)
    q_nope, q_rope = q[..., :QK_NOPE_DIM], q[..., QK_NOPE_DIM:]
    k_nope, v = kv[..., :QK_NOPE_DIM], kv[..., QK_NOPE_DIM:]
    q = jnp.concatenate([q_nope, apply_rope(q_rope, positions)], axis=-1)
    k_rope = apply_rope(k_rope, positions)
    k = jnp.concatenate(
        [k_nope, jnp.broadcast_to(k_rope[:, :, None, :], (B, S, MLA_HEADS, QK_ROPE_DIM))], axis=-1)
    o = causal_block_attention(q, k, v)
    return (o * jax.nn.silu(z)) @ w_out


def short_conv_mixer(h, w_in, w_conv, w_out):
    b_gate, c_gate, u, z = jnp.split(h @ w_in, 4, axis=-1)
    y = b_gate * causal_depthwise_conv(c_gate * u, w_conv)
    return (y * jax.nn.silu(z)) @ w_out


def gmlp_mixer(h, w_in, ln_g, ln_b, w_s, b_s, w_out):
    u, v, z = jnp.split(h @ w_in, 3, axis=-1)
    u = jax.nn.gelu(u, approximate=False)
    v = layer_norm(jax.nn.gelu(v, approximate=False), ln_g, ln_b)
    B, S, E = v.shape
    vc = v.reshape(B, S // GM_CHUNK, GM_CHUNK, GM_GROUPS, E // GM_GROUPS)
    causal = jnp.tril(jnp.ones((GM_CHUNK, GM_CHUNK), dtype=w_s.dtype))
    w_mix = (w_s * causal).astype(v.dtype)
    s = jnp.einsum("gts,bcsgd->bctgd", w_mix, vc) + b_s.T.astype(v.dtype)[None, None, :, :, None]
    y = u * s.reshape(B, S, E)
    return (y * jax.nn.silu(z)) @ w_out


def conformer_conv_mixer(h, w_in, w_dw, b_dw, ln_g, ln_b, w_out):
    a, g, z = jnp.split(h @ w_in, 3, axis=-1)
    y = a * jax.nn.sigmoid(g)
    y = causal_depthwise_conv(y, w_dw) + b_dw
    y = jax.nn.silu(layer_norm(y, ln_g, ln_b))
    return (y * jax.nn.silu(z)) @ w_out


def _fwd_setup_inputs(seed: int = 0) -> dict:
    key = jax.random.key(seed)
    ks = jax.random.split(key, 32)

    def dense(k, shape, fan_in):
        return jax.random.normal(k, shape, jnp.float32) * (fan_in ** -0.5)

    def gain(k, shape):
        return 1.0 + 0.05 * jax.random.normal(k, shape, jnp.float32)

    def small(k, shape):
        return 0.02 * jax.random.normal(k, shape, jnp.float32)

    x = jax.random.normal(ks[0], (BATCH, SEQ, D_MODEL), jnp.float32)
    positions = (jax.random.randint(ks[1], (BATCH, 1), 0, 1024)
                 + jnp.arange(SEQ)[None, :]).astype(jnp.int32)
    norm_pre = gain(ks[2], (DEPTH, D_MODEL))
    norm_post = gain(ks[3], (DEPTH, D_MODEL))

    n = N_MLA_LAYERS
    mla_in = Q_LORA_RANK + KV_LORA_RANK + QK_ROPE_DIM + MLA_HEADS * V_HEAD_DIM
    w_in_mla = dense(ks[4], (n, D_MODEL, mla_in), D_MODEL)
    mla_q_norm = gain(ks[5], (n, Q_LORA_RANK))
    w_uq = dense(ks[6], (n, Q_LORA_RANK, MLA_HEADS * (QK_NOPE_DIM + QK_ROPE_DIM)), Q_LORA_RANK)
    mla_kv_norm = gain(ks[7], (n, KV_LORA_RANK))
    w_ukv = dense(ks[8], (n, KV_LORA_RANK, MLA_HEADS * (QK_NOPE_DIM + V_HEAD_DIM)), KV_LORA_RANK)
    w_out_mla = dense(ks[9], (n, MLA_HEADS * V_HEAD_DIM, D_MODEL), MLA_HEADS * V_HEAD_DIM)

    n = N_SC_LAYERS
    w_in_sc = dense(ks[10], (n, D_MODEL, 4 * SC_WIDTH), D_MODEL)
    sc_conv = dense(ks[11], (n, SC_CONV_WIDTH, SC_WIDTH), SC_CONV_WIDTH)
    w_out_sc = dense(ks[12], (n, SC_WIDTH, D_MODEL), SC_WIDTH)

    n = N_GM_LAYERS
    w_in_gm = dense(ks[13], (n, D_MODEL, 3 * GM_WIDTH), D_MODEL)
    gm_ln_g = gain(ks[14], (n, GM_WIDTH))
    gm_ln_b = small(ks[15], (n, GM_WIDTH))
    gm_w_s = dense(ks[16], (n, GM_GROUPS, GM_CHUNK, GM_CHUNK), GM_CHUNK)
    gm_b_s = gain(ks[17], (n, GM_GROUPS, GM_CHUNK))
    w_out_gm = dense(ks[18], (n, GM_WIDTH, D_MODEL), GM_WIDTH)

    n = N_CF_LAYERS
    w_in_cf = dense(ks[19], (n, D_MODEL, 3 * CF_WIDTH), D_MODEL)
    cf_dw = dense(ks[20], (n, CF_CONV_WIDTH, CF_WIDTH), CF_CONV_WIDTH)
    cf_dw_b = small(ks[21], (n, CF_WIDTH))
    cf_ln_g = gain(ks[22], (n, CF_WIDTH))
    cf_ln_b = small(ks[23], (n, CF_WIDTH))
    w_out_cf = dense(ks[24], (n, CF_WIDTH, D_MODEL), CF_WIDTH)

    return {
        "x": x, "positions": positions, "norm_pre": norm_pre, "norm_post": norm_post,
        "w_in_mla": w_in_mla, "mla_q_norm": mla_q_norm, "w_uq": w_uq,
        "mla_kv_norm": mla_kv_norm, "w_ukv": w_ukv, "w_out_mla": w_out_mla,
        "w_in_sc": w_in_sc, "sc_conv": sc_conv, "w_out_sc": w_out_sc,
        "w_in_gm": w_in_gm, "gm_ln_g": gm_ln_g, "gm_ln_b": gm_ln_b,
        "gm_w_s": gm_w_s, "gm_b_s": gm_b_s, "w_out_gm": w_out_gm,
        "w_in_cf": w_in_cf, "cf_dw": cf_dw, "cf_dw_b": cf_dw_b,
        "cf_ln_g": cf_ln_g, "cf_ln_b": cf_ln_b, "w_out_cf": w_out_cf,
    }


def _fwd_reference(x, positions, norm_pre, norm_post,
              w_in_mla, mla_q_norm, w_uq, mla_kv_norm, w_ukv, w_out_mla,
              w_in_sc, sc_conv, w_out_sc,
              w_in_gm, gm_ln_g, gm_ln_b, gm_w_s, gm_b_s, w_out_gm,
              w_in_cf, cf_dw, cf_dw_b, cf_ln_g, cf_ln_b, w_out_cf):
    for i in range(DEPTH):
        m, j = i % N_MIXERS, i // N_MIXERS
        h = rms_norm(x, norm_pre[i])
        if m == 0:
            y = mla_mixer(h, positions, w_in_mla[j], mla_q_norm[j], w_uq[j],
                          mla_kv_norm[j], w_ukv[j], w_out_mla[j])
        elif m == 1:
            y = short_conv_mixer(h, w_in_sc[j], sc_conv[j], w_out_sc[j])
        elif m == 2:
            y = gmlp_mixer(h, w_in_gm[j], gm_ln_g[j], gm_ln_b[j], gm_w_s[j], gm_b_s[j], w_out_gm[j])
        else:
            y = conformer_conv_mixer(h, w_in_cf[j], cf_dw[j], cf_dw_b[j], cf_ln_g[j], cf_ln_b[j],
                                     w_out_cf[j])
        x = x + rms_norm(y, norm_post[i])
    return x


import jax as _jax
import jax.numpy as _jnp

TWIN_FORMAT = 'train_step'
FWD_PARAMS = ['x', 'positions', 'norm_pre', 'norm_post', 'w_in_mla', 'mla_q_norm', 'w_uq', 'mla_kv_norm', 'w_ukv', 'w_out_mla', 'w_in_sc', 'sc_conv', 'w_out_sc', 'w_in_gm', 'gm_ln_g', 'gm_ln_b', 'gm_w_s', 'gm_b_s', 'w_out_gm', 'w_in_cf', 'cf_dw', 'cf_dw_b', 'cf_ln_g', 'cf_ln_b', 'w_out_cf']
TWIN_WEIGHTS = ['norm_pre', 'norm_post', 'w_in_mla', 'mla_q_norm', 'w_uq', 'mla_kv_norm', 'w_ukv', 'w_out_mla', 'w_in_sc', 'sc_conv', 'w_out_sc', 'w_in_gm', 'gm_ln_g', 'gm_ln_b', 'gm_w_s', 'gm_b_s', 'w_out_gm', 'w_in_cf', 'cf_dw', 'cf_dw_b', 'cf_ln_g', 'cf_ln_b', 'w_out_cf']
TWIN_DIFF_INPUT = 'x'
TWIN_INPUTS = ['x', 'positions', 'norm_pre', 'norm_post', 'w_in_mla', 'mla_q_norm', 'w_uq', 'mla_kv_norm', 'w_ukv', 'w_out_mla', 'w_in_sc', 'sc_conv', 'w_out_sc', 'w_in_gm', 'gm_ln_g', 'gm_ln_b', 'gm_w_s', 'gm_b_s', 'w_out_gm', 'w_in_cf', 'cf_dw', 'cf_dw_b', 'cf_ln_g', 'cf_ln_b', 'w_out_cf', 'loss_target', 'm_norm_pre', 'm_norm_post', 'm_w_in_mla', 'm_mla_q_norm', 'm_w_uq', 'm_mla_kv_norm', 'm_w_ukv', 'm_w_out_mla', 'm_w_in_sc', 'm_sc_conv', 'm_w_out_sc', 'm_w_in_gm', 'm_gm_ln_g', 'm_gm_ln_b', 'm_gm_w_s', 'm_gm_b_s', 'm_w_out_gm', 'm_w_in_cf', 'm_cf_dw', 'm_cf_dw_b', 'm_cf_ln_g', 'm_cf_ln_b', 'm_w_out_cf', 'v_norm_pre', 'v_norm_post', 'v_w_in_mla', 'v_mla_q_norm', 'v_w_uq', 'v_mla_kv_norm', 'v_w_ukv', 'v_w_out_mla', 'v_w_in_sc', 'v_sc_conv', 'v_w_out_sc', 'v_w_in_gm', 'v_gm_ln_g', 'v_gm_ln_b', 'v_gm_w_s', 'v_gm_b_s', 'v_w_out_gm', 'v_w_in_cf', 'v_cf_dw', 'v_cf_dw_b', 'v_cf_ln_g', 'v_cf_ln_b', 'v_w_out_cf']
TWIN_OUTPUTS = ['loss', 'grad_x', 'grad_norm_pre', 'grad_norm_post', 'grad_w_in_mla', 'grad_mla_q_norm', 'grad_w_uq', 'grad_mla_kv_norm', 'grad_w_ukv', 'grad_w_out_mla', 'grad_w_in_sc', 'grad_sc_conv', 'grad_w_out_sc', 'grad_w_in_gm', 'grad_gm_ln_g', 'grad_gm_ln_b', 'grad_gm_w_s', 'grad_gm_b_s', 'grad_w_out_gm', 'grad_w_in_cf', 'grad_cf_dw', 'grad_cf_dw_b', 'grad_cf_ln_g', 'grad_cf_ln_b', 'grad_w_out_cf', 'delta_norm_pre', 'delta_norm_post', 'delta_w_in_mla', 'delta_mla_q_norm', 'delta_w_uq', 'delta_mla_kv_norm', 'delta_w_ukv', 'delta_w_out_mla', 'delta_w_in_sc', 'delta_sc_conv', 'delta_w_out_sc', 'delta_w_in_gm', 'delta_gm_ln_g', 'delta_gm_ln_b', 'delta_gm_w_s', 'delta_gm_b_s', 'delta_w_out_gm', 'delta_w_in_cf', 'delta_cf_dw', 'delta_cf_dw_b', 'delta_cf_ln_g', 'delta_cf_ln_b', 'delta_w_out_cf', 'new_m_norm_pre', 'new_m_norm_post', 'new_m_w_in_mla', 'new_m_mla_q_norm', 'new_m_w_uq', 'new_m_mla_kv_norm', 'new_m_w_ukv', 'new_m_w_out_mla', 'new_m_w_in_sc', 'new_m_sc_conv', 'new_m_w_out_sc', 'new_m_w_in_gm', 'new_m_gm_ln_g', 'new_m_gm_ln_b', 'new_m_gm_w_s', 'new_m_gm_b_s', 'new_m_w_out_gm', 'new_m_w_in_cf', 'new_m_cf_dw', 'new_m_cf_dw_b', 'new_m_cf_ln_g', 'new_m_cf_ln_b', 'new_m_w_out_cf', 'new_v_norm_pre', 'new_v_norm_post', 'new_v_w_in_mla', 'new_v_mla_q_norm', 'new_v_w_uq', 'new_v_mla_kv_norm', 'new_v_w_ukv', 'new_v_w_out_mla', 'new_v_w_in_sc', 'new_v_sc_conv', 'new_v_w_out_sc', 'new_v_w_in_gm', 'new_v_gm_ln_g', 'new_v_gm_ln_b', 'new_v_gm_w_s', 'new_v_gm_b_s', 'new_v_w_out_gm', 'new_v_w_in_cf', 'new_v_cf_dw', 'new_v_cf_dw_b', 'new_v_cf_ln_g', 'new_v_cf_ln_b', 'new_v_w_out_cf']
TWIN_LEAF_KINDS = {'loss': 'loss', 'grad_x': 'grad_x', 'grad_norm_pre': 'grad_w', 'grad_norm_post': 'grad_w', 'grad_w_in_mla': 'grad_w', 'grad_mla_q_norm': 'grad_w', 'grad_w_uq': 'grad_w', 'grad_mla_kv_norm': 'grad_w', 'grad_w_ukv': 'grad_w', 'grad_w_out_mla': 'grad_w', 'grad_w_in_sc': 'grad_w', 'grad_sc_conv': 'grad_w', 'grad_w_out_sc': 'grad_w', 'grad_w_in_gm': 'grad_w', 'grad_gm_ln_g': 'grad_w', 'grad_gm_ln_b': 'grad_w', 'grad_gm_w_s': 'grad_w', 'grad_gm_b_s': 'grad_w', 'grad_w_out_gm': 'grad_w', 'grad_w_in_cf': 'grad_w', 'grad_cf_dw': 'grad_w', 'grad_cf_dw_b': 'grad_w', 'grad_cf_ln_g': 'grad_w', 'grad_cf_ln_b': 'grad_w', 'grad_w_out_cf': 'grad_w', 'delta_norm_pre': 'delta_w', 'delta_norm_post': 'delta_w', 'delta_w_in_mla': 'delta_w', 'delta_mla_q_norm': 'delta_w', 'delta_w_uq': 'delta_w', 'delta_mla_kv_norm': 'delta_w', 'delta_w_ukv': 'delta_w', 'delta_w_out_mla': 'delta_w', 'delta_w_in_sc': 'delta_w', 'delta_sc_conv': 'delta_w', 'delta_w_out_sc': 'delta_w', 'delta_w_in_gm': 'delta_w', 'delta_gm_ln_g': 'delta_w', 'delta_gm_ln_b': 'delta_w', 'delta_gm_w_s': 'delta_w', 'delta_gm_b_s': 'delta_w', 'delta_w_out_gm': 'delta_w', 'delta_w_in_cf': 'delta_w', 'delta_cf_dw': 'delta_w', 'delta_cf_dw_b': 'delta_w', 'delta_cf_ln_g': 'delta_w', 'delta_cf_ln_b': 'delta_w', 'delta_w_out_cf': 'delta_w', 'new_m_norm_pre': 'new_m', 'new_m_norm_post': 'new_m', 'new_m_w_in_mla': 'new_m', 'new_m_mla_q_norm': 'new_m', 'new_m_w_uq': 'new_m', 'new_m_mla_kv_norm': 'new_m', 'new_m_w_ukv': 'new_m', 'new_m_w_out_mla': 'new_m', 'new_m_w_in_sc': 'new_m', 'new_m_sc_conv': 'new_m', 'new_m_w_out_sc': 'new_m', 'new_m_w_in_gm': 'new_m', 'new_m_gm_ln_g': 'new_m', 'new_m_gm_ln_b': 'new_m', 'new_m_gm_w_s': 'new_m', 'new_m_gm_b_s': 'new_m', 'new_m_w_out_gm': 'new_m', 'new_m_w_in_cf': 'new_m', 'new_m_cf_dw': 'new_m', 'new_m_cf_dw_b': 'new_m', 'new_m_cf_ln_g': 'new_m', 'new_m_cf_ln_b': 'new_m', 'new_m_w_out_cf': 'new_m', 'new_v_norm_pre': 'new_v', 'new_v_norm_post': 'new_v', 'new_v_w_in_mla': 'new_v', 'new_v_mla_q_norm': 'new_v', 'new_v_w_uq': 'new_v', 'new_v_mla_kv_norm': 'new_v', 'new_v_w_ukv': 'new_v', 'new_v_w_out_mla': 'new_v', 'new_v_w_in_sc': 'new_v', 'new_v_sc_conv': 'new_v', 'new_v_w_out_sc': 'new_v', 'new_v_w_in_gm': 'new_v', 'new_v_gm_ln_g': 'new_v', 'new_v_gm_ln_b': 'new_v', 'new_v_gm_w_s': 'new_v', 'new_v_gm_b_s': 'new_v', 'new_v_w_out_gm': 'new_v', 'new_v_w_in_cf': 'new_v', 'new_v_cf_dw': 'new_v', 'new_v_cf_dw_b': 'new_v', 'new_v_cf_ln_g': 'new_v', 'new_v_cf_ln_b': 'new_v', 'new_v_w_out_cf': 'new_v'}


def _forward(args):
    return _fwd_reference(*[args[k] for k in FWD_PARAMS])


def _output_shape():
    out = _jax.eval_shape(lambda: _forward(_fwd_setup_inputs(0)))
    return out.shape, out.dtype

N_MICROBATCH = 1
ADAM_LR = 0.001
ADAM_B1 = 0.9
ADAM_B2 = 0.999
ADAM_EPS = 1e-08
ADAM_WD = 0.01
ADAM_STEP = 10
PER_EXAMPLE_BATCH_AXIS = {'x': 0, 'positions': 0, 'loss_target': 0}
SHARED_INPUTS = []
_WEIGHT_DTYPES = {'norm_pre': _jnp.float32, 'norm_post': _jnp.float32, 'w_in_mla': _jnp.float32, 'mla_q_norm': _jnp.float32, 'w_uq': _jnp.float32, 'mla_kv_norm': _jnp.float32, 'w_ukv': _jnp.float32, 'w_out_mla': _jnp.float32, 'w_in_sc': _jnp.float32, 'sc_conv': _jnp.float32, 'w_out_sc': _jnp.float32, 'w_in_gm': _jnp.float32, 'gm_ln_g': _jnp.float32, 'gm_ln_b': _jnp.float32, 'gm_w_s': _jnp.float32, 'gm_b_s': _jnp.float32, 'w_out_gm': _jnp.float32, 'w_in_cf': _jnp.float32, 'cf_dw': _jnp.float32, 'cf_dw_b': _jnp.float32, 'cf_ln_g': _jnp.float32, 'cf_ln_b': _jnp.float32, 'w_out_cf': _jnp.float32}
MOMENT_SCALE = {'norm_pre': 3.598904e-01, 'norm_post': 7.962041e+00, 'w_in_mla': 4.844737e-01, 'mla_q_norm': 6.257422e-01, 'w_uq': 2.514989e-01, 'mla_kv_norm': 1.258401e+00, 'w_ukv': 2.893543e-01, 'w_out_mla': 3.040862e-01, 'w_in_sc': 1.718669e-01, 'sc_conv': 1.708158e-01, 'w_out_sc': 1.721962e-01, 'w_in_gm': 1.187544e-01, 'gm_ln_g': 6.828203e-02, 'gm_ln_b': 6.995378e-02, 'gm_w_s': 1.382698e-01, 'gm_b_s': 2.126993e-01, 'w_out_gm': 1.600213e-01, 'w_in_cf': 9.123503e-02, 'cf_dw': 1.062071e-01, 'cf_dw_b': 4.616913e-01, 'cf_ln_g': 2.013044e-01, 'cf_ln_b': 2.835096e-01, 'w_out_cf': 1.387866e-01}


def _to_microbatches(a, axis):
    t = _jnp.moveaxis(a, axis, 0)
    t = t.reshape((N_MICROBATCH, t.shape[0] // N_MICROBATCH) + t.shape[1:])
    return _jnp.moveaxis(t, 1, axis + 1)


def setup_inputs(seed: int = 0) -> dict:
    inp = _fwd_setup_inputs(seed)
    key = _jax.random.fold_in(_jax.random.key(seed), 7919)
    shape, _ = _output_shape()
    out = dict(inp)
    out["loss_target"] = _jax.random.normal(_jax.random.fold_in(key, 0), shape, _jnp.float32)
    for i, name in enumerate(TWIN_WEIGHTS):
        w = inp[name].astype(_jnp.float32)
        if MOMENT_SCALE is None:
            s = _jnp.sqrt(_jnp.mean(_jnp.square(w)) + 1e-30)
        else:
            s = MOMENT_SCALE[name]
        km, kv = _jax.random.split(_jax.random.fold_in(key, i + 1))
        out[name] = w
        out["m_" + name] = s * _jax.random.normal(km, w.shape, _jnp.float32)
        out["v_" + name] = (s * s) * _jax.random.uniform(kv, w.shape, _jnp.float32, 0.5, 1.5)
    if N_MICROBATCH > 1:
        for name, axis in PER_EXAMPLE_BATCH_AXIS.items():
            out[name] = _to_microbatches(out[name], axis)
    return {'x': out['x'], 'positions': out['positions'], 'norm_pre': out['norm_pre'], 'norm_post': out['norm_post'], 'w_in_mla': out['w_in_mla'], 'mla_q_norm': out['mla_q_norm'], 'w_uq': out['w_uq'], 'mla_kv_norm': out['mla_kv_norm'], 'w_ukv': out['w_ukv'], 'w_out_mla': out['w_out_mla'], 'w_in_sc': out['w_in_sc'], 'sc_conv': out['sc_conv'], 'w_out_sc': out['w_out_sc'], 'w_in_gm': out['w_in_gm'], 'gm_ln_g': out['gm_ln_g'], 'gm_ln_b': out['gm_ln_b'], 'gm_w_s': out['gm_w_s'], 'gm_b_s': out['gm_b_s'], 'w_out_gm': out['w_out_gm'], 'w_in_cf': out['w_in_cf'], 'cf_dw': out['cf_dw'], 'cf_dw_b': out['cf_dw_b'], 'cf_ln_g': out['cf_ln_g'], 'cf_ln_b': out['cf_ln_b'], 'w_out_cf': out['w_out_cf'], 'loss_target': out['loss_target'], 'm_norm_pre': out['m_norm_pre'], 'm_norm_post': out['m_norm_post'], 'm_w_in_mla': out['m_w_in_mla'], 'm_mla_q_norm': out['m_mla_q_norm'], 'm_w_uq': out['m_w_uq'], 'm_mla_kv_norm': out['m_mla_kv_norm'], 'm_w_ukv': out['m_w_ukv'], 'm_w_out_mla': out['m_w_out_mla'], 'm_w_in_sc': out['m_w_in_sc'], 'm_sc_conv': out['m_sc_conv'], 'm_w_out_sc': out['m_w_out_sc'], 'm_w_in_gm': out['m_w_in_gm'], 'm_gm_ln_g': out['m_gm_ln_g'], 'm_gm_ln_b': out['m_gm_ln_b'], 'm_gm_w_s': out['m_gm_w_s'], 'm_gm_b_s': out['m_gm_b_s'], 'm_w_out_gm': out['m_w_out_gm'], 'm_w_in_cf': out['m_w_in_cf'], 'm_cf_dw': out['m_cf_dw'], 'm_cf_dw_b': out['m_cf_dw_b'], 'm_cf_ln_g': out['m_cf_ln_g'], 'm_cf_ln_b': out['m_cf_ln_b'], 'm_w_out_cf': out['m_w_out_cf'], 'v_norm_pre': out['v_norm_pre'], 'v_norm_post': out['v_norm_post'], 'v_w_in_mla': out['v_w_in_mla'], 'v_mla_q_norm': out['v_mla_q_norm'], 'v_w_uq': out['v_w_uq'], 'v_mla_kv_norm': out['v_mla_kv_norm'], 'v_w_ukv': out['v_w_ukv'], 'v_w_out_mla': out['v_w_out_mla'], 'v_w_in_sc': out['v_w_in_sc'], 'v_sc_conv': out['v_sc_conv'], 'v_w_out_sc': out['v_w_out_sc'], 'v_w_in_gm': out['v_w_in_gm'], 'v_gm_ln_g': out['v_gm_ln_g'], 'v_gm_ln_b': out['v_gm_ln_b'], 'v_gm_w_s': out['v_gm_w_s'], 'v_gm_b_s': out['v_gm_b_s'], 'v_w_out_gm': out['v_w_out_gm'], 'v_w_in_cf': out['v_w_in_cf'], 'v_cf_dw': out['v_cf_dw'], 'v_cf_dw_b': out['v_cf_dw_b'], 'v_cf_ln_g': out['v_cf_ln_g'], 'v_cf_ln_b': out['v_cf_ln_b'], 'v_w_out_cf': out['v_w_out_cf']}


def _loss(weights, diff, rest, loss_target):
    with _jax.named_scope("forward"):
        args = {**rest, TWIN_DIFF_INPUT: diff, **{k: w.astype(_WEIGHT_DTYPES[k]) for k, w in weights.items()}}
        y = _forward(args)
    with _jax.named_scope("loss_head"):
        err = _jnp.square(y.astype(_jnp.float32) - loss_target)
        return 0.5 * _jnp.sum(_jnp.mean(err, axis=-1)) if err.ndim else 0.5 * err


def _adamw(w, g, m, v):
    m = ADAM_B1 * m + (1.0 - ADAM_B1) * g
    v = ADAM_B2 * v + (1.0 - ADAM_B2) * _jnp.square(g)
    m_hat = m / (1.0 - ADAM_B1 ** ADAM_STEP)
    v_hat = v / (1.0 - ADAM_B2 ** ADAM_STEP)
    delta = -ADAM_LR * (m_hat / (_jnp.sqrt(v_hat) + ADAM_EPS) + ADAM_WD * w)
    return delta, m, v


def reference(x, positions, norm_pre, norm_post, w_in_mla, mla_q_norm, w_uq, mla_kv_norm, w_ukv, w_out_mla, w_in_sc, sc_conv, w_out_sc, w_in_gm, gm_ln_g, gm_ln_b, gm_w_s, gm_b_s, w_out_gm, w_in_cf, cf_dw, cf_dw_b, cf_ln_g, cf_ln_b, w_out_cf, loss_target, m_norm_pre, m_norm_post, m_w_in_mla, m_mla_q_norm, m_w_uq, m_mla_kv_norm, m_w_ukv, m_w_out_mla, m_w_in_sc, m_sc_conv, m_w_out_sc, m_w_in_gm, m_gm_ln_g, m_gm_ln_b, m_gm_w_s, m_gm_b_s, m_w_out_gm, m_w_in_cf, m_cf_dw, m_cf_dw_b, m_cf_ln_g, m_cf_ln_b, m_w_out_cf, v_norm_pre, v_norm_post, v_w_in_mla, v_mla_q_norm, v_w_uq, v_mla_kv_norm, v_w_ukv, v_w_out_mla, v_w_in_sc, v_sc_conv, v_w_out_sc, v_w_in_gm, v_gm_ln_g, v_gm_ln_b, v_gm_w_s, v_gm_b_s, v_w_out_gm, v_w_in_cf, v_cf_dw, v_cf_dw_b, v_cf_ln_g, v_cf_ln_b, v_w_out_cf):
    given = dict(x=x, positions=positions, norm_pre=norm_pre, norm_post=norm_post, w_in_mla=w_in_mla, mla_q_norm=mla_q_norm, w_uq=w_uq, mla_kv_norm=mla_kv_norm, w_ukv=w_ukv, w_out_mla=w_out_mla, w_in_sc=w_in_sc, sc_conv=sc_conv, w_out_sc=w_out_sc, w_in_gm=w_in_gm, gm_ln_g=gm_ln_g, gm_ln_b=gm_ln_b, gm_w_s=gm_w_s, gm_b_s=gm_b_s, w_out_gm=w_out_gm, w_in_cf=w_in_cf, cf_dw=cf_dw, cf_dw_b=cf_dw_b, cf_ln_g=cf_ln_g, cf_ln_b=cf_ln_b, w_out_cf=w_out_cf, loss_target=loss_target, m_norm_pre=m_norm_pre, m_norm_post=m_norm_post, m_w_in_mla=m_w_in_mla, m_mla_q_norm=m_mla_q_norm, m_w_uq=m_w_uq, m_mla_kv_norm=m_mla_kv_norm, m_w_ukv=m_w_ukv, m_w_out_mla=m_w_out_mla, m_w_in_sc=m_w_in_sc, m_sc_conv=m_sc_conv, m_w_out_sc=m_w_out_sc, m_w_in_gm=m_w_in_gm, m_gm_ln_g=m_gm_ln_g, m_gm_ln_b=m_gm_ln_b, m_gm_w_s=m_gm_w_s, m_gm_b_s=m_gm_b_s, m_w_out_gm=m_w_out_gm, m_w_in_cf=m_w_in_cf, m_cf_dw=m_cf_dw, m_cf_dw_b=m_cf_dw_b, m_cf_ln_g=m_cf_ln_g, m_cf_ln_b=m_cf_ln_b, m_w_out_cf=m_w_out_cf, v_norm_pre=v_norm_pre, v_norm_post=v_norm_post, v_w_in_mla=v_w_in_mla, v_mla_q_norm=v_mla_q_norm, v_w_uq=v_w_uq, v_mla_kv_norm=v_mla_kv_norm, v_w_ukv=v_w_ukv, v_w_out_mla=v_w_out_mla, v_w_in_sc=v_w_in_sc, v_sc_conv=v_sc_conv, v_w_out_sc=v_w_out_sc, v_w_in_gm=v_w_in_gm, v_gm_ln_g=v_gm_ln_g, v_gm_ln_b=v_gm_ln_b, v_gm_w_s=v_gm_w_s, v_gm_b_s=v_gm_b_s, v_w_out_gm=v_w_out_gm, v_w_in_cf=v_w_in_cf, v_cf_dw=v_cf_dw, v_cf_dw_b=v_cf_dw_b, v_cf_ln_g=v_cf_ln_g, v_cf_ln_b=v_cf_ln_b, v_w_out_cf=v_w_out_cf)
    weights = {n: given[n] for n in TWIN_WEIGHTS}
    shared = {n: given[n] for n in SHARED_INPUTS}
    per_example = {n: given[n] for n in ['x', 'positions']}
    grad_fn = _jax.value_and_grad(_loss, argnums=(0, 1))

    def one_microbatch(ex, loss_target):
        ex = dict(ex)
        diff = ex.pop(TWIN_DIFF_INPUT)
        return grad_fn(weights, diff, {**shared, **ex}, loss_target)

    if N_MICROBATCH == 1:
        loss, (grad_w, grad_x) = one_microbatch(per_example, given["loss_target"])
    else:
        def body(carry, xs):
            loss_sum, grad_sum = carry
            l_k, (gw_k, gx_k) = one_microbatch(xs[0], xs[1])
            with _jax.named_scope("update"):
                return (loss_sum + l_k, _jax.tree.map(_jnp.add, grad_sum, gw_k)), gx_k

        init = (_jnp.zeros((), _jnp.float32), _jax.tree.map(_jnp.zeros_like, weights))
        (loss, grad_w), grad_x = _jax.lax.scan(body, init, (per_example, given["loss_target"]))
    with _jax.named_scope("update"):
        delta_w, new_m, new_v = {}, {}, {}
        for n in TWIN_WEIGHTS:
            delta_w[n], new_m[n], new_v[n] = _adamw(weights[n], grad_w[n], given["m_" + n], given["v_" + n])
    return (loss, grad_x, *[grad_w[n] for n in TWIN_WEIGHTS], *[delta_w[n] for n in TWIN_WEIGHTS],
            *[new_m[n] for n in TWIN_WEIGHTS], *[new_v[n] for n in TWIN_WEIGHTS])
```

```python
import functools
import math

import numpy as np
import jax
import jax.numpy as jnp
from jax import lax
from jax.experimental import pallas as pl
from jax.experimental.pallas import tpu as pltpu

F32 = jnp.float32
MXU = jnp.bfloat16
WIRE = jnp.bfloat16
MESH = pl.DeviceIdType.MESH
N_DEV = 8

NORM_EPS = 1e-6
LN_EPS = 1e-5
ROPE_THETA = 10000.0
NOPE, ROPE, VHEAD = 128, 64, 128
QPAD = 256
ATT_SCALE = float((NOPE + ROPE) ** -0.5)
INV_SQRT2 = 0.7071067811865476
INV_SQRT_2PI = 0.3989422804014327
ADAM_LR, ADAM_B1, ADAM_B2, ADAM_EPS, ADAM_WD, ADAM_STEP = 0.001, 0.9, 0.999, 1e-08, 0.01, 10

LANE = 128
SUBLANE = 8
VMEM_LIMIT = 56 * 1024 * 1024
MM_TILE = 1024
ATT_TILE = 512
ROW_TILE = 128
CONV_CHUNK = 256
CONV_CB = 128

NT = (((1,), (1,)), ((), ()))
TN = (((0,), (0,)), ((), ()))
NN = (((1,), (0,)), ((), ()))


def _tile(dim, pref, align):
    t = min(pref, dim)
    t -= t % align
    while t >= align:
        if dim % t == 0:
            return t
        t -= align
    return dim


def _cp(*sem):
    return pltpu.CompilerParams(dimension_semantics=sem, vmem_limit_bytes=VMEM_LIMIT)


def _sig(v):
    return jax.nn.sigmoid(v)


def _silu(v):
    return v * _sig(v)


def _silu_grad(v):
    s = _sig(v)
    return s * (1.0 + v * (1.0 - s))


def _gelu(v):
    return 0.5 * v * (1.0 + lax.erf(v * INV_SQRT2))


def _gelu_grad(v):
    return 0.5 * (1.0 + lax.erf(v * INV_SQRT2)) + v * jnp.exp(-0.5 * v * v) * INV_SQRT_2PI


def _mesh_pos():
    return lax.axis_index("x"), lax.axis_index("y"), lax.axis_index("c")


def _other_chips(x, y):
    return [(1 - x, y), (x, 1 - y), (1 - x, 1 - y)]


def _allgather(arrs, name):
    n = len(arrs)

    def body(*refs):
        ins, outs = refs[:n], refs[n:2 * n]
        send_sems, recv_sems, loc_sems = refs[2 * n:]
        x, y, c = _mesh_pos()
        me, sib = (x, y, c), (x, y, 1 - c)
        chips = _other_chips(x, y)

        def copy(a, k, block, to, src=None):
            dst = outs[a].at[4 * block[0] + 2 * block[1] + block[2]]
            return pltpu.make_async_remote_copy(
                src_ref=dst if src is None else src, dst_ref=dst,
                send_sem=send_sems.at[a, k], recv_sem=recv_sems.at[a, k],
                device_id=to, device_id_type=MESH)

        locs = [pltpu.make_async_copy(ins[a], outs[a].at[4 * x + 2 * y + c], loc_sems.at[a]) for a in range(n)]
        for cp in locs:
            cp.start()
        first = []
        for a in range(n):
            first.append(copy(a, 0, me, sib, src=ins[a]))
            for j, chip in enumerate(chips):
                first.append(copy(a, 1 + j, me, (*chip, c), src=ins[a]))
        for cp in first:
            cp.start()
        passed = []
        for j, chip in enumerate(chips):
            for a in range(n):
                copy(a, 1 + j, (*chip, c), me).wait_recv()
                cp = copy(a, 4 + j, (*chip, c), sib)
                cp.start()
                passed.append(cp)
        for a in range(n):
            copy(a, 0, sib, me).wait_recv()
            for j, chip in enumerate(chips):
                copy(a, 4 + j, (*chip, 1 - c), me).wait_recv()
        for cp in first + passed:
            cp.wait_send()
        for cp in locs:
            cp.wait()

    hbm = pl.BlockSpec(memory_space=pltpu.HBM)
    return pl.pallas_call(
        body, name=name,
        out_shape=[jax.ShapeDtypeStruct((N_DEV,) + a.shape, a.dtype) for a in arrs],
        in_specs=[hbm] * n, out_specs=[hbm] * n,
        scratch_shapes=[pltpu.SemaphoreType.DMA((n, 7)), pltpu.SemaphoreType.DMA((n, 7)),
                        pltpu.SemaphoreType.DMA((n,))],
    )(*arrs)


def _rs_sibling(parts, name):
    n = len(parts)

    def body(*refs):
        ins, outs = refs[:n], refs[n:2 * n]
        send_sems, recv_sems = refs[2 * n:]
        x, y, c = _mesh_pos()
        cps = [pltpu.make_async_remote_copy(
            src_ref=ins[a].at[:, pl.ds(1 - c, 1)], dst_ref=outs[a],
            send_sem=send_sems.at[a], recv_sem=recv_sems.at[a],
            device_id=(x, y, 1 - c), device_id_type=MESH) for a in range(n)]
        for cp in cps:
            cp.start()
        for cp in cps:
            cp.wait()

    hbm = pl.BlockSpec(memory_space=pltpu.HBM)
    return pl.pallas_call(
        body, name=name,
        out_shape=[jax.ShapeDtypeStruct((4, 1) + p.shape[2:], p.dtype) for p in parts],
        in_specs=[hbm] * n, out_specs=[hbm] * n,
        scratch_shapes=[pltpu.SemaphoreType.DMA((n,)), pltpu.SemaphoreType.DMA((n,))],
    )(*parts)


def _rs_chips(qs, name):
    n = len(qs)

    def body(*refs):
        ins, outs = refs[:n], refs[n:2 * n]
        send_sems, recv_sems, loc_sems = refs[2 * n:]
        x, y, c = _mesh_pos()
        cps = []
        for a in range(n):
            cps.append(pltpu.make_async_copy(ins[a].at[2 * x + y], outs[a].at[3], loc_sems.at[a]))
            for k, chip in enumerate(_other_chips(x, y)):
                cps.append(pltpu.make_async_remote_copy(
                    src_ref=ins[a].at[2 * chip[0] + chip[1]], dst_ref=outs[a].at[k],
                    send_sem=send_sems.at[a, k], recv_sem=recv_sems.at[a, k],
                    device_id=(*chip, c), device_id_type=MESH))
        for cp in cps:
            cp.start()
        for cp in cps:
            cp.wait()

    hbm = pl.BlockSpec(memory_space=pltpu.HBM)
    return pl.pallas_call(
        body, name=name,
        out_shape=[jax.ShapeDtypeStruct(q.shape, q.dtype) for q in qs],
        in_specs=[hbm] * n, out_specs=[hbm] * n,
        scratch_shapes=[pltpu.SemaphoreType.DMA((n, 3)), pltpu.SemaphoreType.DMA((n, 3)),
                        pltpu.SemaphoreType.DMA((n,))],
    )(*qs)


def _pair_add(part4, recv, core, name):
    _, _, R, C = part4.shape
    tr = _tile(R, max(SUBLANE, (1 << 19) // C), 16)

    def body(core_ref, p_ref, r_ref, o_ref):
        o_ref[...] = (p_ref[...].astype(F32) + r_ref[...].astype(F32)).astype(o_ref.dtype)

    return pl.pallas_call(
        body, name=name,
        out_shape=jax.ShapeDtypeStruct((4, R, C), WIRE),
        grid_spec=pltpu.PrefetchScalarGridSpec(
            num_scalar_prefetch=1, grid=(4, R // tr),
            in_specs=[pl.BlockSpec((None, None, tr, C), lambda s, i, cr: (s, cr[0], i, 0)),
                      pl.BlockSpec((None, None, tr, C), lambda s, i, cr: (s, 0, i, 0))],
            out_specs=pl.BlockSpec((None, tr, C), lambda s, i, cr: (s, i, 0))),
        compiler_params=_cp("parallel", "parallel"),
    )(core, part4, recv)


def _reduce_scatter(parts, core, name):
    p4 = [p.reshape((4, 2) + p.shape[1:]) for p in parts]
    recv = _rs_sibling(p4, name + "_d2d")
    qs = [_pair_add(p, r, core, f"{name}_add{i}") for i, (p, r) in enumerate(zip(p4, recv))]
    return _rs_chips(qs, name + "_ici")


def _mm(a, b, *, ta=False, tb=False, out_dtype, name, blocks=None):
    M, K = (a.shape[1], a.shape[0]) if ta else a.shape
    N = b.shape[0] if tb else b.shape[1]
    assert K == (b.shape[1] if tb else b.shape[0]), (a.shape, b.shape, ta, tb)
    ns = N // blocks if blocks else N
    tm, tk = _tile(M, MM_TILE, LANE), _tile(K, MM_TILE, LANE)
    tn = _tile(ns, MM_TILE, LANE)
    nk = K // tk
    per = ns // tn

    def body(a_ref, b_ref, o_ref, acc_ref):
        k = pl.program_id(2)

        @pl.when(k == 0)
        def _():
            acc_ref[...] = jnp.zeros_like(acc_ref)

        dims = (((0,) if ta else (1,), (1,) if tb else (0,)), ((), ()))
        acc_ref[...] += lax.dot_general(a_ref[...], b_ref[...], dims, preferred_element_type=F32)

        @pl.when(k == nk - 1)
        def _():
            o_ref[...] = acc_ref[...].astype(o_ref.dtype)

    a_spec = pl.BlockSpec((tk, tm), lambda i, j, k: (k, i)) if ta else pl.BlockSpec((tm, tk), lambda i, j, k: (i, k))
    b_spec = pl.BlockSpec((tn, tk), lambda i, j, k: (j, k)) if tb else pl.BlockSpec((tk, tn), lambda i, j, k: (k, j))
    if blocks:
        out_shape = jax.ShapeDtypeStruct((blocks, M, ns), out_dtype)
        o_spec = pl.BlockSpec((None, tm, tn), lambda i, j, k: (j // per, i, j % per))
    else:
        out_shape = jax.ShapeDtypeStruct((M, N), out_dtype)
        o_spec = pl.BlockSpec((tm, tn), lambda i, j, k: (i, j))
    return pl.pallas_call(
        body, name=name, out_shape=out_shape, grid=(M // tm, N // tn, nk),
        in_specs=[a_spec, b_spec], out_specs=o_spec,
        scratch_shapes=[pltpu.VMEM((tm, tn), F32)],
        compiler_params=_cp("parallel", "parallel", "arbitrary"),
    )(a, b)


def _rms_fwd(xin, g, name, resid=None):
    T, D = xin.shape
    tr = _tile(T, ROW_TILE, SUBLANE)

    def body(*refs):
        x_ref, g_ref = refs[0], refs[1]
        o_ref = refs[-1]
        xv = x_ref[...].astype(F32)
        r = lax.rsqrt(jnp.mean(xv * xv, axis=-1, keepdims=True) + NORM_EPS)
        yv = xv * r * g_ref[...]
        if resid is not None:
            yv = refs[2][...] + yv
        o_ref[...] = yv.astype(o_ref.dtype)

    row = pl.BlockSpec((tr, D), lambda i: (i, 0))
    ins = [xin, g] + ([resid] if resid is not None else [])
    return pl.pallas_call(
        body, name=name, out_shape=jax.ShapeDtypeStruct((T, D), F32 if resid is not None else MXU),
        grid=(T // tr,), in_specs=[row, pl.BlockSpec((1, D), lambda i: (0, 0))] + ([row] if resid is not None else []),
        out_specs=row, compiler_params=_cp("parallel"),
    )(*ins)


def _rms_bwd(xin, dout, g, name, resid=None, out_dtype=F32):
    T, D = xin.shape
    tr = _tile(T, ROW_TILE, SUBLANE)

    def body(*refs):
        x_ref, d_ref, g_ref = refs[:3]
        dx_ref, dg_ref = refs[-2:]
        xv = x_ref[...].astype(F32)
        dv = d_ref[...].astype(F32)
        r = lax.rsqrt(jnp.mean(xv * xv, axis=-1, keepdims=True) + NORM_EPS)
        xh = xv * r
        dh = dv * g_ref[...]
        dxv = r * (dh - xh * jnp.mean(dh * xh, axis=-1, keepdims=True))
        if resid is not None:
            dxv = refs[3][...] + dxv
        dx_ref[...] = dxv.astype(dx_ref.dtype)

        @pl.when(pl.program_id(0) == 0)
        def _():
            dg_ref[...] = jnp.zeros_like(dg_ref)

        dg_ref[...] += jnp.sum(dv * xh, axis=0, keepdims=True)

    row = pl.BlockSpec((tr, D), lambda i: (i, 0))
    vec = pl.BlockSpec((1, D), lambda i: (0, 0))
    ins = [xin, dout, g] + ([resid] if resid is not None else [])
    return pl.pallas_call(
        body, name=name,
        out_shape=[jax.ShapeDtypeStruct((T, D), out_dtype), jax.ShapeDtypeStruct((1, D), F32)],
        grid=(T // tr,), in_specs=[row, row, vec] + ([row] if resid is not None else []),
        out_specs=[row, vec], compiler_params=_cp("arbitrary"),
    )(*ins)


def _loss_head(y, target, name):
    T, D = y.shape
    tr = _tile(T, ROW_TILE, SUBLANE)

    def body(y_ref, t_ref, d_ref, l_ref):
        e = y_ref[...] - t_ref[...]
        d_ref[...] = e * (1.0 / D)

        @pl.when(pl.program_id(0) == 0)
        def _():
            l_ref[...] = jnp.zeros_like(l_ref)

        rows = jnp.sum(e * e, axis=-1, keepdims=True) * (1.0 / D)
        l_ref[...] += jnp.broadcast_to(0.5 * jnp.sum(rows, axis=0, keepdims=True), l_ref.shape)

    row = pl.BlockSpec((tr, D), lambda i: (i, 0))
    return pl.pallas_call(
        body, name=name,
        out_shape=[jax.ShapeDtypeStruct((T, D), F32), jax.ShapeDtypeStruct((1, LANE), F32)],
        grid=(T // tr,), in_specs=[row, row], out_specs=[row, pl.BlockSpec((1, LANE), lambda i: (0, 0))],
        compiler_params=_cp("arbitrary"),
    )(y, target)


def _mla_norms_fwd(P, gq, gkv, qr, kvr, name):
    T = P.shape[0]
    tr = _tile(T, ROW_TILE, SUBLANE)

    def body(cq_ref, ckv_ref, gq_ref, gkv_ref, oq_ref, okv_ref):
        for x_ref, g_ref, o_ref in ((cq_ref, gq_ref, oq_ref), (ckv_ref, gkv_ref, okv_ref)):
            xv = x_ref[...].astype(F32)
            r = lax.rsqrt(jnp.mean(xv * xv, axis=-1, keepdims=True) + NORM_EPS)
            o_ref[...] = (xv * r * g_ref[...]).astype(o_ref.dtype)

    return pl.pallas_call(
        body, name=name,
        out_shape=[jax.ShapeDtypeStruct((T, qr), MXU), jax.ShapeDtypeStruct((T, kvr), MXU)],
        grid=(T // tr,),
        in_specs=[pl.BlockSpec((tr, qr), lambda i: (i, 0)), pl.BlockSpec((tr, kvr), lambda i: (i, qr // kvr)),
                  pl.BlockSpec((1, qr), lambda i: (0, 0)), pl.BlockSpec((1, kvr), lambda i: (0, 0))],
        out_specs=[pl.BlockSpec((tr, qr), lambda i: (i, 0)), pl.BlockSpec((tr, kvr), lambda i: (i, 0))],
        compiler_params=_cp("parallel"),
    )(P, P, gq, gkv)


def _mla_norms_bwd(P, dcqn, dckvn_k, dckvn_v, dkr, gq, gkv, qr, kvr, pa, name):
    T = P.shape[0]
    tr = _tile(T, ROW_TILE, SUBLANE)
    c2 = qr + kvr

    def body(cq_ref, ckv_ref, dq_ref, dk_ref, dv_ref, dkr_ref, gq_ref, gkv_ref, dp_ref, dgq_ref, dgkv_ref):
        @pl.when(pl.program_id(0) == 0)
        def _():
            dgq_ref[...] = jnp.zeros_like(dgq_ref)
            dgkv_ref[...] = jnp.zeros_like(dgkv_ref)

        def one(x_ref, dv, g_ref, dg_ref):
            xv = x_ref[...].astype(F32)
            r = lax.rsqrt(jnp.mean(xv * xv, axis=-1, keepdims=True) + NORM_EPS)
            xh = xv * r
            dh = dv * g_ref[...]
            dg_ref[...] += jnp.sum(dv * xh, axis=0, keepdims=True)
            return r * (dh - xh * jnp.mean(dh * xh, axis=-1, keepdims=True))

        dp_ref[:, 0:qr] = one(cq_ref, dq_ref[...], gq_ref, dgq_ref).astype(dp_ref.dtype)
        dp_ref[:, qr:c2] = one(ckv_ref, dk_ref[...] + dv_ref[...], gkv_ref, dgkv_ref).astype(dp_ref.dtype)
        dp_ref[:, c2:c2 + LANE] = dkr_ref[...].astype(dp_ref.dtype)
        if pa > c2 + LANE:
            dp_ref[:, c2 + LANE:pa] = jnp.zeros((tr, pa - c2 - LANE), dp_ref.dtype)

    return pl.pallas_call(
        body, name=name,
        out_shape=[jax.ShapeDtypeStruct((T, pa), MXU), jax.ShapeDtypeStruct((1, qr), F32),
                   jax.ShapeDtypeStruct((1, kvr), F32)],
        grid=(T // tr,),
        in_specs=[pl.BlockSpec((tr, qr), lambda i: (i, 0)), pl.BlockSpec((tr, kvr), lambda i: (i, qr // kvr)),
                  pl.BlockSpec((tr, qr), lambda i: (i, 0)), pl.BlockSpec((tr, kvr), lambda i: (i, 0)),
                  pl.BlockSpec((tr, kvr), lambda i: (i, 0)), pl.BlockSpec((tr, LANE), lambda i: (i, 0)),
                  pl.BlockSpec((1, qr), lambda i: (0, 0)), pl.BlockSpec((1, kvr), lambda i: (0, 0))],
        out_specs=[pl.BlockSpec((tr, pa), lambda i: (i, 0)), pl.BlockSpec((1, qr), lambda i: (0, 0)),
                   pl.BlockSpec((1, kvr), lambda i: (0, 0))],
        compiler_params=_cp("arbitrary"),
    )(P, P, dcqn, dckvn_k, dckvn_v, dkr, gq, gkv)


def _rope_tables(pos_ref, invf_ref):
    ang = pos_ref[...].astype(F32) * invf_ref[...]
    lane = lax.broadcasted_iota(jnp.int32, ang.shape, 1)
    cos, sin = jnp.cos(ang), jnp.sin(ang)
    half = ROPE // 2
    c = jnp.where(lane < ROPE, cos, 0.0)
    s1 = jnp.where(lane < half, -sin, 0.0)
    s2 = jnp.where((lane >= half) & (lane < ROPE), sin, 0.0)
    return c, s1, s2


def _rope_fwd(qpad, kpad, P, pos, invf, kr_blk, H, name):
    T = qpad.shape[0]
    tr = _tile(T, ROW_TILE, SUBLANE)

    def body(q_ref, k_ref, kr_ref, pos_ref, invf_ref, qo_ref, ko_ref):
        c, s1, s2 = _rope_tables(pos_ref, invf_ref)

        def rot(v):
            return v * c + pltpu.roll(v, LANE - ROPE // 2, 1) * s1 + pltpu.roll(v, ROPE // 2, 1) * s2

        kr = rot(kr_ref[...].astype(F32)).astype(ko_ref.dtype)
        for h in range(H):
            lo = h * QPAD
            qo_ref[:, lo:lo + NOPE] = q_ref[:, lo:lo + NOPE]
            qo_ref[:, lo + NOPE:lo + QPAD] = rot(q_ref[:, lo + NOPE:lo + QPAD].astype(F32)).astype(qo_ref.dtype)
            ko_ref[:, lo:lo + NOPE] = k_ref[:, lo:lo + NOPE]
            ko_ref[:, lo + NOPE:lo + QPAD] = kr

    wide = pl.BlockSpec((tr, H * QPAD), lambda i: (i, 0))
    return pl.pallas_call(
        body, name=name,
        out_shape=[jax.ShapeDtypeStruct(qpad.shape, MXU), jax.ShapeDtypeStruct(kpad.shape, MXU)],
        grid=(T // tr,),
        in_specs=[wide, wide, pl.BlockSpec((tr, LANE), lambda i: (i, kr_blk)),
                  pl.BlockSpec((tr, 1), lambda i: (i, 0)), pl.BlockSpec((1, LANE), lambda i: (0, 0))],
        out_specs=[wide, wide], compiler_params=_cp("parallel"),
    )(qpad, kpad, P, pos, invf)


def _rope_bwd(dqcat, dkcat, pos, invf, H, name):
    T = dqcat.shape[0]
    tr = _tile(T, ROW_TILE, SUBLANE)

    def body(dq_ref, dk_ref, pos_ref, invf_ref, dqo_ref, dkr_ref):
        c, s1, s2 = _rope_tables(pos_ref, invf_ref)

        def rot_t(v):
            return v * c + pltpu.roll(v * s1, ROPE // 2, 1) + pltpu.roll(v * s2, LANE - ROPE // 2, 1)

        acc = jnp.zeros((tr, LANE), F32)
        for h in range(H):
            lo = h * QPAD
            dqo_ref[:, lo:lo + NOPE] = dq_ref[:, lo:lo + NOPE]
            dqo_ref[:, lo + NOPE:lo + QPAD] = rot_t(dq_ref[:, lo + NOPE:lo + QPAD].astype(F32)).astype(dqo_ref.dtype)
            acc = acc + dk_ref[:, lo + NOPE:lo + QPAD].astype(F32)
        dkr_ref[...] = rot_t(acc)

    wide = pl.BlockSpec((tr, H * QPAD), lambda i: (i, 0))
    return pl.pallas_call(
        body, name=name,
        out_shape=[jax.ShapeDtypeStruct(dqcat.shape, MXU), jax.ShapeDtypeStruct((T, LANE), F32)],
        grid=(T // tr,),
        in_specs=[wide, wide, pl.BlockSpec((tr, 1), lambda i: (i, 0)), pl.BlockSpec((1, LANE), lambda i: (0, 0))],
        out_specs=[wide, pl.BlockSpec((tr, LANE), lambda i: (i, 0))], compiler_params=_cp("parallel"),
    )(dqcat, dkcat, pos, invf)


def _causal_mask(s):
    row = lax.broadcasted_iota(jnp.int32, s.shape, 0)
    col = lax.broadcasted_iota(jnp.int32, s.shape, 1)
    return jnp.where(col <= row, s, -1e30)


def _flash_fwd(qcat, kcat, v, P, z_blk, H, name):
    T = qcat.shape[0]
    tq = _tile(T, ATT_TILE, LANE)

    def body(q_ref, k_ref, v_ref, z_ref, o_ref, a_ref, lse_ref, m_s, l_s, acc_s):
        i = pl.program_id(1)
        q = q_ref[...]
        m_s[...] = jnp.full_like(m_s, -1e30)
        l_s[...] = jnp.zeros_like(l_s)
        acc_s[...] = jnp.zeros_like(acc_s)

        def step(j, masked):
            r0 = pl.multiple_of(j * tq, tq)
            kb = k_ref[pl.ds(r0, tq), :]
            vb = v_ref[pl.ds(r0, tq), :]
            s = lax.dot_general(q, kb, NT, preferred_element_type=F32) * ATT_SCALE
            if masked:
                s = _causal_mask(s)
            m_prev = m_s[...]
            m_new = jnp.maximum(m_prev, jnp.max(s, axis=1, keepdims=True))
            p = jnp.exp(s - m_new)
            alpha = jnp.exp(m_prev - m_new)
            l_s[...] = alpha * l_s[...] + jnp.sum(p, axis=1, keepdims=True)
            acc_s[...] = alpha * acc_s[...] + lax.dot_general(p.astype(vb.dtype), vb, NN, preferred_element_type=F32)
            m_s[...] = m_new

        def loop(j, carry):
            step(j, False)
            return carry

        lax.fori_loop(0, i, loop, 0)
        step(i, True)
        l = l_s[...]
        o = acc_s[...] / l
        o_ref[...] = o.astype(o_ref.dtype)
        a_ref[...] = (o * _silu(z_ref[...].astype(F32))).astype(a_ref.dtype)
        lse_ref[...] = jnp.broadcast_to(m_s[...] + jnp.log(l), lse_ref.shape)

    return pl.pallas_call(
        body, name=name,
        out_shape=[jax.ShapeDtypeStruct((T, H * VHEAD), MXU), jax.ShapeDtypeStruct((T, H * VHEAD), MXU),
                   jax.ShapeDtypeStruct((H, T, LANE), F32)],
        grid=(H, T // tq),
        in_specs=[pl.BlockSpec((tq, QPAD), lambda h, i: (i, h)), pl.BlockSpec((T, QPAD), lambda h, i: (0, h)),
                  pl.BlockSpec((T, VHEAD), lambda h, i: (0, h)), pl.BlockSpec((tq, VHEAD), lambda h, i: (i, z_blk + h))],
        out_specs=[pl.BlockSpec((tq, VHEAD), lambda h, i: (i, h)), pl.BlockSpec((tq, VHEAD), lambda h, i: (i, h)),
                   pl.BlockSpec((None, tq, LANE), lambda h, i: (h, i, 0))],
        scratch_shapes=[pltpu.VMEM((tq, 1), F32), pltpu.VMEM((tq, 1), F32), pltpu.VMEM((tq, VHEAD), F32)],
        compiler_params=_cp("parallel", "arbitrary"),
    )(qcat, kcat, v, P)


def _flash_bwd_kv(qcat, kcat, v, do, lse, delta, H, name):
    T = qcat.shape[0]
    tq = _tile(T, ATT_TILE, LANE)
    nq = T // tq

    def body(q_ref, do_ref, lse_ref, dl_ref, k_ref, v_ref, dk_ref, dv_ref, dk_s, dv_s):
        j = pl.program_id(1)
        kb, vb = k_ref[...], v_ref[...]
        dk_s[...] = jnp.zeros_like(dk_s)
        dv_s[...] = jnp.zeros_like(dv_s)

        def step(i, masked):
            r0 = pl.multiple_of(i * tq, tq)
            qb = q_ref[pl.ds(r0, tq), :]
            dob = do_ref[pl.ds(r0, tq), :]
            s = lax.dot_general(qb, kb, NT, preferred_element_type=F32) * ATT_SCALE
            if masked:
                s = _causal_mask(s)
            p = jnp.exp(s - lse_ref[pl.ds(r0, tq), 0:1])
            dv_s[...] += lax.dot_general(p.astype(dob.dtype), dob, TN, preferred_element_type=F32)
            dp = lax.dot_general(dob, vb, NT, preferred_element_type=F32)
            ds = p * (dp - dl_ref[pl.ds(r0, tq), 0:1]) * ATT_SCALE
            dk_s[...] += lax.dot_general(ds.astype(qb.dtype), qb, TN, preferred_element_type=F32)

        def loop(i, carry):
            step(i, False)
            return carry

        step(j, True)
        lax.fori_loop(j + 1, nq, loop, 0)
        dk_ref[...] = dk_s[...].astype(dk_ref.dtype)
        dv_ref[...] = dv_s[...].astype(dv_ref.dtype)

    return pl.pallas_call(
        body, name=name,
        out_shape=[jax.ShapeDtypeStruct(kcat.shape, MXU), jax.ShapeDtypeStruct(v.shape, MXU)],
        grid=(H, nq),
        in_specs=[pl.BlockSpec((T, QPAD), lambda h, j: (0, h)), pl.BlockSpec((T, VHEAD), lambda h, j: (0, h)),
                  pl.BlockSpec((None, T, LANE), lambda h, j: (h, 0, 0)), pl.BlockSpec((None, T, LANE), lambda h, j: (h, 0, 0)),
                  pl.BlockSpec((tq, QPAD), lambda h, j: (j, h)), pl.BlockSpec((tq, VHEAD), lambda h, j: (j, h))],
        out_specs=[pl.BlockSpec((tq, QPAD), lambda h, j: (j, h)), pl.BlockSpec((tq, VHEAD), lambda h, j: (j, h))],
        scratch_shapes=[pltpu.VMEM((tq, QPAD), F32), pltpu.VMEM((tq, VHEAD), F32)],
        compiler_params=_cp("parallel", "arbitrary"),
    )(qcat, do, lse, delta, kcat, v)


def _flash_bwd_q(qcat, kcat, v, do, lse, delta, H, name):
    T = qcat.shape[0]
    tq = _tile(T, ATT_TILE, LANE)

    def body(q_ref, do_ref, lse_ref, dl_ref, k_ref, v_ref, dq_ref, dq_s):
        i = pl.program_id(1)
        qb, dob = q_ref[...], do_ref[...]
        lse, dl = lse_ref[:, 0:1], dl_ref[:, 0:1]
        dq_s[...] = jnp.zeros_like(dq_s)

        def step(j, masked):
            r0 = pl.multiple_of(j * tq, tq)
            kb = k_ref[pl.ds(r0, tq), :]
            vb = v_ref[pl.ds(r0, tq), :]
            s = lax.dot_general(qb, kb, NT, preferred_element_type=F32) * ATT_SCALE
            if masked:
                s = _causal_mask(s)
            p = jnp.exp(s - lse)
            dp = lax.dot_general(dob, vb, NT, preferred_element_type=F32)
            ds = p * (dp - dl) * ATT_SCALE
            dq_s[...] += lax.dot_general(ds.astype(kb.dtype), kb, NN, preferred_element_type=F32)

        def loop(j, carry):
            step(j, False)
            return carry

        lax.fori_loop(0, i, loop, 0)
        step(i, True)
        dq_ref[...] = dq_s[...].astype(dq_ref.dtype)

    return pl.pallas_call(
        body, name=name,
        out_shape=jax.ShapeDtypeStruct(qcat.shape, MXU),
        grid=(H, T // tq),
        in_specs=[pl.BlockSpec((tq, QPAD), lambda h, i: (i, h)), pl.BlockSpec((tq, VHEAD), lambda h, i: (i, h)),
                  pl.BlockSpec((None, tq, LANE), lambda h, i: (h, i, 0)), pl.BlockSpec((None, tq, LANE), lambda h, i: (h, i, 0)),
                  pl.BlockSpec((T, QPAD), lambda h, i: (0, h)), pl.BlockSpec((T, VHEAD), lambda h, i: (0, h))],
        out_specs=pl.BlockSpec((tq, QPAD), lambda h, i: (i, h)),
        scratch_shapes=[pltpu.VMEM((tq, QPAD), F32)],
        compiler_params=_cp("parallel", "arbitrary"),
    )(qcat, do, lse, delta, kcat, v)


def _mla_gate_bwd(da, o, P, pa, H, name):
    T, HV = da.shape
    tr = _tile(T, ROW_TILE, SUBLANE)
    cw = _tile(math.gcd(pa, HV), 512, LANE)
    hb = cw // VHEAD

    def body(da_ref, o_ref, z_ref, do_ref, dz_ref, dl_ref):
        dav, ov, zv = da_ref[...].astype(F32), o_ref[...].astype(F32), z_ref[...].astype(F32)
        dov = dav * _silu(zv)
        do_ref[...] = dov.astype(do_ref.dtype)
        dz_ref[...] = (dav * ov * _silu_grad(zv)).astype(dz_ref.dtype)
        prod = dov * ov
        for h in range(hb):
            dl_ref[h] = jnp.broadcast_to(jnp.sum(prod[:, h * VHEAD:(h + 1) * VHEAD], axis=1, keepdims=True), (tr, LANE))

    blk = pl.BlockSpec((tr, cw), lambda i, j: (i, j))
    shifted = pl.BlockSpec((tr, cw), lambda i, j: (i, pa // cw + j))
    return pl.pallas_call(
        body, name=name,
        out_shape=[jax.ShapeDtypeStruct((T, HV), MXU), jax.ShapeDtypeStruct((T, HV), MXU),
                   jax.ShapeDtypeStruct((H, T, LANE), F32)],
        grid=(T // tr, HV // cw),
        in_specs=[blk, blk, shifted],
        out_specs=[blk, blk, pl.BlockSpec((hb, tr, LANE), lambda i, j: (j, i, 0))],
        compiler_params=_cp("parallel", "parallel"),
    )(da, o, P)


def _bank_rows(width):
    return min(SUBLANE, width), -(-(width - 1) // SUBLANE) * SUBLANE


def _bank_fill(bank_ref, val, width, T, causal):
    nr, hp = _bank_rows(width)
    rows = lax.broadcasted_iota(jnp.int32, val.shape, 0)
    zero = jnp.zeros((hp,) + val.shape[1:], F32)
    for r in range(nr):
        if causal:
            bank_ref[r, 0:hp, :] = zero
            bank_ref[r, hp:hp + T, :] = val if r == 0 else jnp.where(rows >= r, pltpu.roll(val, r, 0), 0.0)
        else:
            bank_ref[r, T:T + hp, :] = zero
            bank_ref[r, 0:T, :] = val if r == 0 else jnp.where(rows < T - r, pltpu.roll(val, T - r, 0), 0.0)


def _bank_tap(bank_ref, s, t0, tc, width, causal):
    _, hp = _bank_rows(width)
    q, r = divmod(s, SUBLANE)
    off = hp - SUBLANE * q if causal else SUBLANE * q
    return bank_ref[r, pl.ds(pl.multiple_of(t0 + off, SUBLANE), tc), :]


def _conv_chunk(bank_ref, w_ref, t0, tc, width):
    acc = None
    for s in range(width):
        k = width - 1 - s
        term = w_ref[k:k + 1, :] * _bank_tap(bank_ref, s, t0, tc, width, True)
        acc = term if acc is None else acc + term
    return acc


def _conv_bwd_chunk(bank_ref, w_ref, xin, dw_ref, t0, tc, width):
    acc = None
    for s in range(width):
        k = width - 1 - s
        tap = _bank_tap(bank_ref, s, t0, tc, width, False)
        term = w_ref[k:k + 1, :] * tap
        acc = term if acc is None else acc + term
        dw_ref[k] += jnp.sum((tap * xin).reshape(tc // SUBLANE, SUBLANE, xin.shape[1]), axis=0)
    return acc


def _full_t(T, cb, col):
    return pl.BlockSpec((T, cb), lambda j, col=col: (0, col + j))


def _sc_fwd(P, w, D, name):
    T = P.shape[0]
    cb = _tile(D, CONV_CB, LANE)
    nb = D // cb
    width = w.shape[0]
    tc = _tile(T, CONV_CHUNK, SUBLANE)
    nr, hp = _bank_rows(width)

    def body(b_ref, c_ref, u_ref, z_ref, w_ref, a_ref, bank):
        _bank_fill(bank, c_ref[...].astype(F32) * u_ref[...].astype(F32), width, T, True)

        def chunk(ci, carry):
            t0 = pl.multiple_of(ci * tc, tc)
            rows = pl.ds(t0, tc)
            v = _conv_chunk(bank, w_ref, t0, tc, width)
            a_ref[rows, :] = (b_ref[rows, :].astype(F32) * v * _silu(z_ref[rows, :].astype(F32))).astype(a_ref.dtype)
            return carry

        lax.fori_loop(0, T // tc, chunk, 0)

    return pl.pallas_call(
        body, name=name, out_shape=jax.ShapeDtypeStruct((T, D), MXU), grid=(nb,),
        in_specs=[_full_t(T, cb, 0), _full_t(T, cb, nb), _full_t(T, cb, 2 * nb), _full_t(T, cb, 3 * nb),
                  pl.BlockSpec((width, cb), lambda j: (0, j))],
        out_specs=_full_t(T, cb, 0),
        scratch_shapes=[pltpu.VMEM((nr, T + hp, cb), F32)],
        compiler_params=_cp("parallel"),
    )(P, P, P, P, w)


def _sc_bwd(P, da, w, D, name):
    T = P.shape[0]
    cb = _tile(D, CONV_CB, LANE)
    nb = D // cb
    width = w.shape[0]
    tc = _tile(T, CONV_CHUNK, SUBLANE)
    nr, hp = _bank_rows(width)

    def body(b_ref, c_ref, u_ref, z_ref, da_ref, w_ref, db_ref, dc_ref, du_ref, dz_ref, dw_ref, bank, dv_s, dw_s):
        _bank_fill(bank, c_ref[...].astype(F32) * u_ref[...].astype(F32), width, T, True)

        def first(ci, carry):
            t0 = pl.multiple_of(ci * tc, tc)
            rows = pl.ds(t0, tc)
            v = _conv_chunk(bank, w_ref, t0, tc, width)
            bv, zv, dav = b_ref[rows, :].astype(F32), z_ref[rows, :].astype(F32), da_ref[rows, :].astype(F32)
            dyb = dav * _silu(zv)
            dz_ref[rows, :] = (dav * bv * v * _silu_grad(zv)).astype(dz_ref.dtype)
            db_ref[rows, :] = (dyb * v).astype(db_ref.dtype)
            dv_s[rows, :] = dyb * bv
            return carry

        lax.fori_loop(0, T // tc, first, 0)
        _bank_fill(bank, dv_s[...], width, T, False)
        dw_s[...] = jnp.zeros_like(dw_s)

        def second(ci, carry):
            t0 = pl.multiple_of(ci * tc, tc)
            rows = pl.ds(t0, tc)
            cv, uv = c_ref[rows, :].astype(F32), u_ref[rows, :].astype(F32)
            dcu = _conv_bwd_chunk(bank, w_ref, cv * uv, dw_s, t0, tc, width)
            dc_ref[rows, :] = (dcu * uv).astype(dc_ref.dtype)
            du_ref[rows, :] = (dcu * cv).astype(du_ref.dtype)
            return carry

        lax.fori_loop(0, T // tc, second, 0)
        dw_ref[...] = jnp.sum(dw_s[...], axis=1)

    outs = pl.pallas_call(
        body, name=name,
        out_shape=[jax.ShapeDtypeStruct((T, D), MXU)] * 4 + [jax.ShapeDtypeStruct((width, D), F32)],
        grid=(nb,),
        in_specs=[_full_t(T, cb, 0), _full_t(T, cb, nb), _full_t(T, cb, 2 * nb), _full_t(T, cb, 3 * nb),
                  _full_t(T, cb, 0), pl.BlockSpec((width, cb), lambda j: (0, j))],
        out_specs=[_full_t(T, cb, 0)] * 4 + [pl.BlockSpec((width, cb), lambda j: (0, j))],
        scratch_shapes=[pltpu.VMEM((nr, T + hp, cb), F32), pltpu.VMEM((T, cb), F32),
                        pltpu.VMEM((width, SUBLANE, cb), F32)],
        compiler_params=_cp("parallel"),
    )(P, P, P, P, da, w)
    return outs[:4], outs[4]


def _cf_conv_fwd(P, w, bias, D, name):
    T = P.shape[0]
    cb = _tile(D, CONV_CB, LANE)
    nb = D // cb
    width = w.shape[0]
    tc = _tile(T, CONV_CHUNK, SUBLANE)
    nr, hp = _bank_rows(width)

    def body(a_ref, g_ref, w_ref, b_ref, y_ref, bank):
        _bank_fill(bank, a_ref[...].astype(F32) * _sig(g_ref[...].astype(F32)), width, T, True)

        def chunk(ci, carry):
            t0 = pl.multiple_of(ci * tc, tc)
            y_ref[pl.ds(t0, tc), :] = (_conv_chunk(bank, w_ref, t0, tc, width) + b_ref[...]).astype(y_ref.dtype)
            return carry

        lax.fori_loop(0, T // tc, chunk, 0)

    return pl.pallas_call(
        body, name=name, out_shape=jax.ShapeDtypeStruct((T, D), F32), grid=(nb,),
        in_specs=[_full_t(T, cb, 0), _full_t(T, cb, nb), pl.BlockSpec((width, cb), lambda j: (0, j)),
                  pl.BlockSpec((1, cb), lambda j: (0, j))],
        out_specs=_full_t(T, cb, 0),
        scratch_shapes=[pltpu.VMEM((nr, T + hp, cb), F32)],
        compiler_params=_cp("parallel"),
    )(P, P, w, bias)


def _cf_conv_bwd(P, dyc, w, D, name):
    T = P.shape[0]
    cb = _tile(D, CONV_CB, LANE)
    nb = D // cb
    width = w.shape[0]
    tc = _tile(T, CONV_CHUNK, SUBLANE)
    nr, hp = _bank_rows(width)

    def body(a_ref, g_ref, dy_ref, w_ref, da_ref, dg_ref, dw_ref, db_ref, bank, dw_s):
        dyv = dy_ref[...]
        db_ref[...] = jnp.sum(dyv, axis=0, keepdims=True)
        _bank_fill(bank, dyv, width, T, False)
        dw_s[...] = jnp.zeros_like(dw_s)

        def chunk(ci, carry):
            t0 = pl.multiple_of(ci * tc, tc)
            rows = pl.ds(t0, tc)
            av, sg = a_ref[rows, :].astype(F32), _sig(g_ref[rows, :].astype(F32))
            dyg = _conv_bwd_chunk(bank, w_ref, av * sg, dw_s, t0, tc, width)
            da_ref[rows, :] = (dyg * sg).astype(da_ref.dtype)
            dg_ref[rows, :] = (dyg * av * sg * (1.0 - sg)).astype(dg_ref.dtype)
            return carry

        lax.fori_loop(0, T // tc, chunk, 0)
        dw_ref[...] = jnp.sum(dw_s[...], axis=1)

    return pl.pallas_call(
        body, name=name,
        out_shape=[jax.ShapeDtypeStruct((T, D), MXU), jax.ShapeDtypeStruct((T, D), MXU),
                   jax.ShapeDtypeStruct((width, D), F32), jax.ShapeDtypeStruct((1, D), F32)],
        grid=(nb,),
        in_specs=[_full_t(T, cb, 0), _full_t(T, cb, nb), _full_t(T, cb, 0),
                  pl.BlockSpec((width, cb), lambda j: (0, j))],
        out_specs=[_full_t(T, cb, 0), _full_t(T, cb, 0), pl.BlockSpec((width, cb), lambda j: (0, j)),
                   pl.BlockSpec((1, cb), lambda j: (0, j))],
        scratch_shapes=[pltpu.VMEM((nr, T + hp, cb), F32), pltpu.VMEM((width, SUBLANE, cb), F32)],
        compiler_params=_cp("parallel"),
    )(P, P, dyc, w)


def _layer_norm_stats(v):
    mu = jnp.mean(v, axis=-1, keepdims=True)
    cen = v - mu
    rstd = lax.rsqrt(jnp.mean(cen * cen, axis=-1, keepdims=True) + LN_EPS)
    return cen * rstd, rstd


def _layer_norm_bwd(dxh, xh, rstd):
    return rstd * (dxh - jnp.mean(dxh, axis=-1, keepdims=True) - xh * jnp.mean(dxh * xh, axis=-1, keepdims=True))


def _cf_act_fwd(yc, P, lg, lb, D, name):
    T = yc.shape[0]
    tr = _tile(T, ROW_TILE, SUBLANE)

    def body(y_ref, z_ref, g_ref, b_ref, o_ref):
        xh, _ = _layer_norm_stats(y_ref[...])
        yl = xh * g_ref[...] + b_ref[...]
        o_ref[...] = (_silu(yl) * _silu(z_ref[...].astype(F32))).astype(o_ref.dtype)

    row = pl.BlockSpec((tr, D), lambda i: (i, 0))
    vec = pl.BlockSpec((1, D), lambda i: (0, 0))
    return pl.pallas_call(
        body, name=name, out_shape=jax.ShapeDtypeStruct((T, D), MXU), grid=(T // tr,),
        in_specs=[row, pl.BlockSpec((tr, D), lambda i: (i, 2)), vec, vec], out_specs=row,
        compiler_params=_cp("parallel"),
    )(yc, P, lg, lb)


def _cf_act_bwd(yc, P, da, lg, lb, D, name):
    T = yc.shape[0]
    tr = _tile(T, ROW_TILE, SUBLANE)

    def body(y_ref, z_ref, da_ref, g_ref, b_ref, dy_ref, dz_ref, dg_ref, db_ref):
        @pl.when(pl.program_id(0) == 0)
        def _():
            dg_ref[...] = jnp.zeros_like(dg_ref)
            db_ref[...] = jnp.zeros_like(db_ref)

        xh, rstd = _layer_norm_stats(y_ref[...])
        yl = xh * g_ref[...] + b_ref[...]
        zv, dav = z_ref[...].astype(F32), da_ref[...].astype(F32)
        dz_ref[...] = (dav * _silu(yl) * _silu_grad(zv)).astype(dz_ref.dtype)
        dyl = dav * _silu(zv) * _silu_grad(yl)
        dg_ref[...] += jnp.sum(dyl * xh, axis=0, keepdims=True)
        db_ref[...] += jnp.sum(dyl, axis=0, keepdims=True)
        dy_ref[...] = _layer_norm_bwd(dyl * g_ref[...], xh, rstd)

    row = pl.BlockSpec((tr, D), lambda i: (i, 0))
    zcol = pl.BlockSpec((tr, D), lambda i: (i, 2))
    vec = pl.BlockSpec((1, D), lambda i: (0, 0))
    return pl.pallas_call(
        body, name=name,
        out_shape=[jax.ShapeDtypeStruct((T, D), F32), jax.ShapeDtypeStruct((T, D), MXU),
                   jax.ShapeDtypeStruct((1, D), F32), jax.ShapeDtypeStruct((1, D), F32)],
        grid=(T // tr,), in_specs=[row, zcol, row, vec, vec], out_specs=[row, row, vec, vec],
        compiler_params=_cp("arbitrary"),
    )(yc, P, da, lg, lb)


def _tril(w):
    row = lax.broadcasted_iota(jnp.int32, w.shape, 0)
    col = lax.broadcasted_iota(jnp.int32, w.shape, 1)
    return jnp.where(col <= row, w, 0.0)


def _gm_fwd(P, lg, lb, ws, bst, D, name):
    T = P.shape[0]
    G, ch, _ = ws.shape
    gw = D // G

    def body(p_ref, g_ref, b_ref, ws_ref, bs_ref, a_ref):
        uv, vv, zv = (p_ref[:, k * D:(k + 1) * D].astype(F32) for k in range(3))
        xh, _ = _layer_norm_stats(_gelu(vv))
        vn = (xh * g_ref[...] + b_ref[...]).astype(MXU)
        gate = _gelu(uv) * _silu(zv)
        for g in range(G):
            cols = slice(g * gw, (g + 1) * gw)
            s = lax.dot_general(_tril(ws_ref[g]).astype(MXU), vn[:, cols], NN, preferred_element_type=F32)
            a_ref[:, cols] = (gate[:, cols] * (s + bs_ref[:, g:g + 1])).astype(a_ref.dtype)

    vec = pl.BlockSpec((1, D), lambda i: (0, 0))
    return pl.pallas_call(
        body, name=name, out_shape=jax.ShapeDtypeStruct((T, D), MXU), grid=(T // ch,),
        in_specs=[pl.BlockSpec((ch, 3 * D), lambda i: (i, 0)), vec, vec,
                  pl.BlockSpec((G, ch, ch), lambda i: (0, 0, 0)), pl.BlockSpec((ch, G), lambda i: (0, 0))],
        out_specs=pl.BlockSpec((ch, D), lambda i: (i, 0)), compiler_params=_cp("parallel"),
    )(P, lg, lb, ws, bst)


def _gm_bwd(P, da, lg, lb, ws, bst, D, name):
    T = P.shape[0]
    G, ch, _ = ws.shape
    gw = D // G

    def body(p_ref, da_ref, g_ref, b_ref, ws_ref, bs_ref, dp_ref, dg_ref, db_ref, dws_ref, dbs_ref, dvn_s):
        @pl.when(pl.program_id(0) == 0)
        def _():
            dg_ref[...] = jnp.zeros_like(dg_ref)
            db_ref[...] = jnp.zeros_like(db_ref)
            dws_ref[...] = jnp.zeros_like(dws_ref)
            dbs_ref[...] = jnp.zeros_like(dbs_ref)

        uv, vv, zv = (p_ref[:, k * D:(k + 1) * D].astype(F32) for k in range(3))
        dav = da_ref[...].astype(F32)
        xh, rstd = _layer_norm_stats(_gelu(vv))
        vn = (xh * g_ref[...] + b_ref[...]).astype(MXU)
        ug, sz = _gelu(uv), _silu(zv)
        ds_all = dav * sz * ug
        for g in range(G):
            cols = slice(g * gw, (g + 1) * gw)
            wm = _tril(ws_ref[g]).astype(MXU)
            s = lax.dot_general(wm, vn[:, cols], NN, preferred_element_type=F32) + bs_ref[:, g:g + 1]
            dp_ref[:, g * gw:(g + 1) * gw] = (dav[:, cols] * sz[:, cols] * s * _gelu_grad(uv[:, cols])).astype(dp_ref.dtype)
            dp_ref[:, 2 * D + g * gw:2 * D + (g + 1) * gw] = (
                dav[:, cols] * ug[:, cols] * s * _silu_grad(zv[:, cols])).astype(dp_ref.dtype)
            ds = ds_all[:, cols]
            dsb = ds.astype(MXU)
            dvn_s[:, cols] = lax.dot_general(wm, dsb, TN, preferred_element_type=F32)
            dws_ref[g] += _tril(lax.dot_general(dsb, vn[:, cols], NT, preferred_element_type=F32))
            dbs_ref[g] += jnp.broadcast_to(jnp.sum(ds, axis=1, keepdims=True), (ch, LANE))
        dvn = dvn_s[...]
        dg_ref[...] += jnp.sum(dvn * xh, axis=0, keepdims=True)
        db_ref[...] += jnp.sum(dvn, axis=0, keepdims=True)
        dp_ref[:, D:2 * D] = (_layer_norm_bwd(dvn * g_ref[...], xh, rstd) * _gelu_grad(vv)).astype(dp_ref.dtype)

    vec = pl.BlockSpec((1, D), lambda i: (0, 0))
    return pl.pallas_call(
        body, name=name,
        out_shape=[jax.ShapeDtypeStruct((T, 3 * D), MXU), jax.ShapeDtypeStruct((1, D), F32), jax.ShapeDtypeStruct((1, D), F32),
                   jax.ShapeDtypeStruct((G, ch, ch), F32), jax.ShapeDtypeStruct((G, ch, LANE), F32)],
        grid=(T // ch,),
        in_specs=[pl.BlockSpec((ch, 3 * D), lambda i: (i, 0)), pl.BlockSpec((ch, D), lambda i: (i, 0)), vec, vec,
                  pl.BlockSpec((G, ch, ch), lambda i: (0, 0, 0)), pl.BlockSpec((ch, G), lambda i: (0, 0))],
        out_specs=[pl.BlockSpec((ch, 3 * D), lambda i: (i, 0)), vec, vec,
                   pl.BlockSpec((G, ch, ch), lambda i: (0, 0, 0)), pl.BlockSpec((G, ch, LANE), lambda i: (0, 0, 0))],
        scratch_shapes=[pltpu.VMEM((ch, D), F32)],
        compiler_params=_cp("arbitrary"),
    )(P, da, lg, lb, ws, bst)


def _adam_math(w, g, m, v):
    m = ADAM_B1 * m + (1.0 - ADAM_B1) * g
    v = ADAM_B2 * v + (1.0 - ADAM_B2) * (g * g)
    m_hat = m / (1.0 - ADAM_B1 ** ADAM_STEP)
    v_hat = v / (1.0 - ADAM_B2 ** ADAM_STEP)
    return -ADAM_LR * (m_hat / (jnp.sqrt(v_hat) + ADAM_EPS) + ADAM_WD * w), m, v


def _adam(w, m, v, gparts, name):
    R, C = w.shape
    n = gparts.shape[0]
    tr = _tile(R, max(SUBLANE, (1 << 18) // C), 16 if gparts.dtype != F32 else SUBLANE)

    def body(w_ref, m_ref, v_ref, gp_ref, g_ref, d_ref, mo_ref, vo_ref):
        g = gp_ref[0].astype(F32)
        for k in range(1, n):
            g = g + gp_ref[k].astype(F32)
        g_ref[...] = g
        d_ref[...], mo_ref[...], vo_ref[...] = _adam_math(w_ref[...], g, m_ref[...], v_ref[...])

    blk = pl.BlockSpec((tr, C), lambda i: (i, 0))
    return pl.pallas_call(
        body, name=name, out_shape=[jax.ShapeDtypeStruct((R, C), F32)] * 4, grid=(R // tr,),
        in_specs=[blk, blk, blk, pl.BlockSpec((n, tr, C), lambda i: (0, i, 0))], out_specs=[blk] * 4,
        compiler_params=_cp("parallel"),
    )(w, m, v, gparts)


def _sum_blocks(parts, name):
    n, R, C = parts.shape
    tr = _tile(R, 512, SUBLANE)

    def body(p_ref, o_ref):
        acc = p_ref[0]
        for k in range(1, n):
            acc = acc + p_ref[k]
        o_ref[...] = acc

    return pl.pallas_call(
        body, name=name, out_shape=jax.ShapeDtypeStruct((R, C), F32), grid=(R // tr,),
        in_specs=[pl.BlockSpec((n, tr, C), lambda i: (0, i, 0))], out_specs=pl.BlockSpec((tr, C), lambda i: (i, 0)),
        compiler_params=_cp("parallel"),
    )(parts)


def _pack(arrs):
    flat = jnp.concatenate([a.reshape(-1).astype(F32) for a in arrs])
    pad = -flat.shape[0] % (SUBLANE * LANE)
    return jnp.pad(flat, (0, pad)).reshape(-1, LANE)


def _unpack(buf, shapes):
    flat, out, off = buf.reshape(-1), [], 0
    for s in shapes:
        n = int(np.prod(s))
        out.append(flat[off:off + n].reshape(s))
        off += n
    return out


def _cols_full(g):
    return jnp.transpose(g, (1, 0, 2)).reshape(g.shape[1], N_DEV * g.shape[2])


def _cols_blocks(w):
    R, N = w.shape
    return jnp.transpose(w.reshape(R, N_DEV, N // N_DEV), (1, 0, 2)).astype(WIRE)


def _rows_blocks(w):
    return w.reshape(N_DEV, w.shape[0] // N_DEV, w.shape[1]).astype(WIRE)


def kernel(x, positions, norm_pre, norm_post, w_in_mla, mla_q_norm, w_uq, mla_kv_norm, w_ukv, w_out_mla, w_in_sc, sc_conv, w_out_sc, w_in_gm, gm_ln_g, gm_ln_b, gm_w_s, gm_b_s, w_out_gm, w_in_cf, cf_dw, cf_dw_b, cf_ln_g, cf_ln_b, w_out_cf, loss_target, m_norm_pre, m_norm_post, m_w_in_mla, m_mla_q_norm, m_w_uq, m_mla_kv_norm, m_w_ukv, m_w_out_mla, m_w_in_sc, m_sc_conv, m_w_out_sc, m_w_in_gm, m_gm_ln_g, m_gm_ln_b, m_gm_w_s, m_gm_b_s, m_w_out_gm, m_w_in_cf, m_cf_dw, m_cf_dw_b, m_cf_ln_g, m_cf_ln_b, m_w_out_cf, v_norm_pre, v_norm_post, v_w_in_mla, v_mla_q_norm, v_w_uq, v_mla_kv_norm, v_w_ukv, v_w_out_mla, v_w_in_sc, v_sc_conv, v_w_out_sc, v_w_in_gm, v_gm_ln_g, v_gm_ln_b, v_gm_w_s, v_gm_b_s, v_w_out_gm, v_w_in_cf, v_cf_dw, v_cf_dw_b, v_cf_ln_g, v_cf_ln_b, v_w_out_cf):
    names = ['norm_pre', 'norm_post', 'w_in_mla', 'mla_q_norm', 'w_uq', 'mla_kv_norm', 'w_ukv', 'w_out_mla', 'w_in_sc',
             'sc_conv', 'w_out_sc', 'w_in_gm', 'gm_ln_g', 'gm_ln_b', 'gm_w_s', 'gm_b_s', 'w_out_gm', 'w_in_cf', 'cf_dw',
             'cf_dw_b', 'cf_ln_g', 'cf_ln_b', 'w_out_cf']
    W = dict(zip(names, (norm_pre, norm_post, w_in_mla, mla_q_norm, w_uq, mla_kv_norm, w_ukv, w_out_mla, w_in_sc, sc_conv,
                         w_out_sc, w_in_gm, gm_ln_g, gm_ln_b, gm_w_s, gm_b_s, w_out_gm, w_in_cf, cf_dw, cf_dw_b, cf_ln_g,
                         cf_ln_b, w_out_cf)))
    Mo = dict(zip(names, (m_norm_pre, m_norm_post, m_w_in_mla, m_mla_q_norm, m_w_uq, m_mla_kv_norm, m_w_ukv, m_w_out_mla,
                          m_w_in_sc, m_sc_conv, m_w_out_sc, m_w_in_gm, m_gm_ln_g, m_gm_ln_b, m_gm_w_s, m_gm_b_s, m_w_out_gm,
                          m_w_in_cf, m_cf_dw, m_cf_dw_b, m_cf_ln_g, m_cf_ln_b, m_w_out_cf)))
    Vo = dict(zip(names, (v_norm_pre, v_norm_post, v_w_in_mla, v_mla_q_norm, v_w_uq, v_mla_kv_norm, v_w_ukv, v_w_out_mla,
                          v_w_in_sc, v_sc_conv, v_w_out_sc, v_w_in_gm, v_gm_ln_g, v_gm_ln_b, v_gm_w_s, v_gm_b_s, v_w_out_gm,
                          v_w_in_cf, v_cf_dw, v_cf_dw_b, v_cf_ln_g, v_cf_ln_b, v_w_out_cf)))
    big = ['w_in_mla', 'w_uq', 'w_ukv', 'w_out_mla', 'w_in_sc', 'w_out_sc', 'w_in_gm', 'w_out_gm', 'w_in_cf', 'w_out_cf']
    row_sharded = {'w_out_mla', 'w_out_sc', 'w_out_gm', 'w_out_cf'}
    chan = ['sc_conv', 'gm_ln_g', 'gm_ln_b', 'cf_dw', 'cf_dw_b', 'cf_ln_g', 'cf_ln_b']
    repl = ['norm_pre', 'norm_post', 'mla_q_norm', 'mla_kv_norm', 'gm_w_s', 'gm_b_s']

    T, D = x.shape[1], x.shape[2]
    xs, tgt = x[0], loss_target[0]
    pos = positions.reshape(T, 1)
    qr, kvr = mla_q_norm.shape[-1], mla_kv_norm.shape[-1]
    H = w_uq.shape[-1] * N_DEV // (NOPE + ROPE)
    HV = H * VHEAD
    c3 = qr + kvr + ROPE
    pa = -(-c3 // 512) * 512
    assert qr % kvr == 0 and qr % LANE == 0 and kvr % LANE == 0 and pa >= qr + kvr + LANE
    mx, my, mc = _mesh_pos()
    me = 4 * mx + 2 * my + mc
    core = mc.astype(jnp.int32).reshape(1)
    cs = D // N_DEV

    def wire(nm):
        return W[nm][0].astype(WIRE)

    g_mla = _allgather([wire('w_in_mla'), wire('w_uq'), wire('w_ukv'), wire('w_out_mla')], "ag_mla")
    g_sc = _allgather([wire('w_in_sc'), wire('w_out_sc')], "ag_sc")
    g_gm = _allgather([wire('w_in_gm'), wire('w_out_gm')], "ag_gm")
    g_cf = _allgather([wire('w_in_cf'), wire('w_out_cf')], "ag_cf")
    chan_rows = [W[nm][0].reshape(-1, cs) for nm in chan]
    chan_cnt = [r.shape[0] for r in chan_rows]
    chan_local = jnp.concatenate(chan_rows, axis=0)
    chan_pad = -chan_local.shape[0] % SUBLANE
    chan_full = _cols_full(_allgather([jnp.pad(chan_local, ((0, chan_pad), (0, 0)))], "ag_small")[0])
    offs = np.cumsum([0] + chan_cnt)
    CH = {nm: chan_full[offs[i]:offs[i + 1]] for i, nm in enumerate(chan)}

    def full_rows(g):
        return g.reshape(g.shape[0] * g.shape[1], g.shape[2])

    w_in_full = _cols_full(g_mla[0])
    w_cat = jnp.concatenate([w_in_full[:, :c3], jnp.zeros((D, pa - c3), WIRE), w_in_full[:, c3:]], axis=1)
    w_uq_pad = jnp.pad(_cols_full(g_mla[1]).reshape(qr, H, NOPE + ROPE), ((0, 0), (0, 0), (0, QPAD - NOPE - ROPE))).reshape(qr, H * QPAD)
    w_ukv3 = _cols_full(g_mla[2]).reshape(kvr, H, NOPE + VHEAD)
    w_k_pad = jnp.pad(w_ukv3[:, :, :NOPE], ((0, 0), (0, 0), (0, QPAD - NOPE))).reshape(kvr, H * QPAD)
    w_v = w_ukv3[:, :, NOPE:].reshape(kvr, HV)
    WOUT = [full_rows(g_mla[3]), full_rows(g_sc[1]), full_rows(g_gm[1]), full_rows(g_cf[1])]
    WIN = [w_cat, _cols_full(g_sc[0]), _cols_full(g_gm[0]), _cols_full(g_cf[0])]

    half = ROPE // 2
    invf_np = np.zeros((1, LANE), np.float32)
    invf_np[0, :ROPE] = np.tile(np.float32(ROPE_THETA) ** (-np.arange(half, dtype=np.float32) / np.float32(half)), 2)
    invf = jnp.asarray(invf_np)
    gm_ws = gm_w_s[0]
    gm_bst = jnp.transpose(gm_b_s[0])

    xin, hs, Ps, acts, ys, keep = [xs], [], [], [], [], {}
    for i in range(4):
        h = _rms_fwd(xin[i], norm_pre[i:i + 1], f"pre{i}")
        P = _mm(h, WIN[i], out_dtype=MXU, name=f"in{i}")
        if i == 0:
            cqn, ckvn = _mla_norms_fwd(P, mla_q_norm, mla_kv_norm, qr, kvr, "mla_norms")
            qpad = _mm(cqn, w_uq_pad, out_dtype=MXU, name="mla_q")
            kpad = _mm(ckvn, w_k_pad, out_dtype=MXU, name="mla_k")
            vv = _mm(ckvn, w_v, out_dtype=MXU, name="mla_v")
            qcat, kcat = _rope_fwd(qpad, kpad, P, pos, invf, (qr + kvr) // LANE, H, "rope")
            o, act, lse = _flash_fwd(qcat, kcat, vv, P, pa // VHEAD, H, "attn")
            keep.update(cqn=cqn, ckvn=ckvn, qcat=qcat, kcat=kcat, v=vv, o=o, lse=lse)
        elif i == 1:
            act = _sc_fwd(P, CH['sc_conv'], D, "sc_mix")
        elif i == 2:
            act = _gm_fwd(P, CH['gm_ln_g'], CH['gm_ln_b'], gm_ws, gm_bst, D, "gm_mix")
        else:
            yc = _cf_conv_fwd(P, CH['cf_dw'], CH['cf_dw_b'], D, "cf_conv")
            act = _cf_act_fwd(yc, P, CH['cf_ln_g'], CH['cf_ln_b'], D, "cf_act")
            keep.update(yc=yc)
        y = _mm(act, WOUT[i], out_dtype=F32, name=f"out{i}")
        xin.append(_rms_fwd(y, norm_post[i:i + 1], f"post{i}", resid=xin[i]))
        hs.append(h), Ps.append(P), acts.append(act), ys.append(y)

    dx, loss_part = _loss_head(xin[4], tgt, "loss")
    loss = lax.psum(loss_part[0, 0], ("x", "y", "c"))

    dnorm_pre, dnorm_post, small_g, RS = [None] * 4, [None] * 4, {}, {}
    for i in (3, 2, 1, 0):
        dy, dnorm_post[i] = _rms_bwd(ys[i], dx, norm_post[i:i + 1], f"post_bwd{i}", out_dtype=MXU)
        da = _mm(dy, WOUT[i], tb=True, out_dtype=MXU, name=f"out_bwd{i}")
        dw_out = _mm(acts[i], dy, ta=True, out_dtype=WIRE, name=f"out_dw{i}").reshape(N_DEV, -1, D)
        P = Ps[i]
        if i == 3:
            dyc, d_z, small_g['cf_ln_g'], small_g['cf_ln_b'] = _cf_act_bwd(
                keep['yc'], P, da, CH['cf_ln_g'], CH['cf_ln_b'], D, "cf_act_bwd")
            d_a, d_g, small_g['cf_dw'], small_g['cf_dw_b'] = _cf_conv_bwd(P, dyc, CH['cf_dw'], D, "cf_conv_bwd")
            dP = jnp.concatenate([d_a, d_g, d_z], axis=1)
        elif i == 2:
            dP, small_g['gm_ln_g'], small_g['gm_ln_b'], small_g['gm_w_s'], dbs = _gm_bwd(
                P, da, CH['gm_ln_g'], CH['gm_ln_b'], gm_ws, gm_bst, D, "gm_mix_bwd")
            small_g['gm_b_s'] = dbs[:, :, 0]
        elif i == 1:
            parts, small_g['sc_conv'] = _sc_bwd(P, da, CH['sc_conv'], D, "sc_mix_bwd")
            dP = jnp.concatenate(parts, axis=1)
        else:
            do, d_z, delta = _mla_gate_bwd(da, keep['o'], P, pa, H, "gate_bwd")
            dkcat, dv = _flash_bwd_kv(keep['qcat'], keep['kcat'], keep['v'], do, keep['lse'], delta, H, "attn_bwd_kv")
            dqcat = _flash_bwd_q(keep['qcat'], keep['kcat'], keep['v'], do, keep['lse'], delta, H, "attn_bwd_q")
            dqpad, dkr = _rope_bwd(dqcat, dkcat, pos, invf, H, "rope_bwd")
            dcqn = _mm(dqpad, w_uq_pad, tb=True, out_dtype=F32, name="mla_q_bwd")
            dckvn_k = _mm(dkcat, w_k_pad, tb=True, out_dtype=F32, name="mla_k_bwd")
            dckvn_v = _mm(dv, w_v, tb=True, out_dtype=F32, name="mla_v_bwd")
            dw_uq_pad = _mm(keep['cqn'], dqpad, ta=True, out_dtype=F32, name="mla_q_dw")
            dw_k_pad = _mm(keep['ckvn'], dkcat, ta=True, out_dtype=F32, name="mla_k_dw")
            dw_v = _mm(keep['ckvn'], dv, ta=True, out_dtype=F32, name="mla_v_dw")
            d_pa, small_g['mla_q_norm'], small_g['mla_kv_norm'] = _mla_norms_bwd(
                P, dcqn, dckvn_k, dckvn_v, dkr, mla_q_norm, mla_kv_norm, qr, kvr, pa, "mla_norms_bwd")
            dP = jnp.concatenate([d_pa, d_z], axis=1)
        dh = _mm(dP, WIN[i], tb=True, out_dtype=F32, name=f"in_bwd{i}")
        dx, dnorm_pre[i] = _rms_bwd(xin[i], dh, norm_pre[i:i + 1], f"pre_bwd{i}", resid=dx)
        if i == 0:
            dw_cat = _mm(hs[i], dP, ta=True, out_dtype=F32, name="in_dw0")
            dw_in = _cols_blocks(jnp.concatenate([dw_cat[:, :c3], dw_cat[:, pa:]], axis=1))
            dw_uq = _cols_blocks(dw_uq_pad.reshape(qr, H, QPAD)[:, :, :NOPE + ROPE].reshape(qr, H * (NOPE + ROPE)))
            dw_ukv = _cols_blocks(jnp.concatenate(
                [dw_k_pad.reshape(kvr, H, QPAD)[:, :, :NOPE], dw_v.reshape(kvr, H, VHEAD)], axis=2).reshape(kvr, H * (NOPE + VHEAD)))
            RS[i] = _reduce_scatter([dw_in, dw_uq, dw_ukv, dw_out], core, "rs_mla")
        else:
            dw_in = _mm(hs[i], dP, ta=True, out_dtype=WIRE, name=f"in_dw{i}", blocks=N_DEV)
            RS[i] = _reduce_scatter([dw_in, dw_out], core, ("rs_sc", "rs_gm", "rs_cf")[i - 1])
    grad_x = dx[None]

    small_g['norm_pre'] = jnp.concatenate(dnorm_pre, axis=0)
    small_g['norm_post'] = jnp.concatenate(dnorm_post, axis=0)
    small_names = repl + chan
    small_shapes = [(small_g[nm].shape) for nm in small_names]
    gathered = _allgather([_pack([small_g[nm] for nm in small_names])], "ag_grads")[0]
    totals = dict(zip(small_names, _unpack(_sum_blocks(gathered, "sum_grads"), small_shapes)))
    local_g = []
    for nm in small_names:
        g = totals[nm]
        if nm in chan:
            g = lax.dynamic_slice_in_dim(g.reshape(-1, D), me * cs, cs, axis=1)
        local_g.append(g.reshape(W[nm].shape))
    local_shapes = [W[nm].shape for nm in small_names]
    sm = _adam(_pack([W[nm] for nm in small_names]), _pack([Mo[nm] for nm in small_names]),
               _pack([Vo[nm] for nm in small_names]), _pack(local_g)[None], "adam_small")
    small_out = [dict(zip(small_names, _unpack(buf, local_shapes))) for buf in sm]

    rs_of = {'w_in_mla': RS[0][0], 'w_uq': RS[0][1], 'w_ukv': RS[0][2], 'w_out_mla': RS[0][3],
             'w_in_sc': RS[1][0], 'w_out_sc': RS[1][1], 'w_in_gm': RS[2][0], 'w_out_gm': RS[2][1],
             'w_in_cf': RS[3][0], 'w_out_cf': RS[3][1]}
    big_out = {nm: [o[None] for o in _adam(W[nm][0], Mo[nm][0], Vo[nm][0], rs_of[nm], "adam_" + nm)] for nm in big}

    outs = [loss, grad_x]
    for k in range(4):
        outs += [big_out[nm][k] if nm in big_out else small_out[k][nm] for nm in names]
    return tuple(outs)
```

```python
import functools
import math

import numpy as np
import jax
import jax.numpy as jnp
from jax import lax
from jax.experimental import pallas as pl
from jax.experimental.pallas import tpu as pltpu

F32 = jnp.float32
MXU = jnp.bfloat16
WIRE = jnp.bfloat16
MESH = pl.DeviceIdType.MESH
N_DEV = 8

NORM_EPS = 1e-6
LN_EPS = 1e-5
ROPE_THETA = 10000.0
NOPE, ROPE, VHEAD = 128, 64, 128
QPAD = 256
ATT_SCALE = float((NOPE + ROPE) ** -0.5)
LOG2E = 1.4426950408889634
INV_SQRT2 = 0.7071067811865476
INV_SQRT_2PI = 0.3989422804014327
ADAM_LR, ADAM_B1, ADAM_B2, ADAM_EPS, ADAM_WD, ADAM_STEP = 0.001, 0.9, 0.999, 1e-08, 0.01, 10

LANE = 128
SUBLANE = 8
VMEM_LIMIT = 56 * 1024 * 1024
MM_TILE = 1024
ATT_TILE = 512
ROW_TILE = 128
CONV_CHUNK = 256
CONV_CB = 128
MXU_FLOP_PER_US = 8e8
ATT_FWD_US, ATT_BWD_US = 1500.0, 2000.0
HOST_SLACK = 1.25

NT = (((1,), (1,)), ((), ()))
TN = (((0,), (0,)), ((), ()))
NN = (((1,), (0,)), ((), ()))


def _tile(dim, pref, align):
    t = min(pref, dim)
    t -= t % align
    while t >= align:
        if dim % t == 0:
            return t
        t -= align
    return dim


def _cp(*sem):
    return pltpu.CompilerParams(dimension_semantics=sem, vmem_limit_bytes=VMEM_LIMIT)


def _sig(v):
    return jax.nn.sigmoid(v)


def _silu(v):
    return v * _sig(v)


def _silu_grad(v):
    s = _sig(v)
    return s * (1.0 + v * (1.0 - s))


def _gelu(v):
    return 0.5 * v * (1.0 + lax.erf(v * INV_SQRT2))


def _gelu_grad(v):
    return 0.5 * (1.0 + lax.erf(v * INV_SQRT2)) + v * jnp.exp(-0.5 * v * v) * INV_SQRT_2PI


def _mesh_pos():
    return lax.axis_index("x"), lax.axis_index("y"), lax.axis_index("c")


def _other_chips(x, y):
    return [(1 - x, y), (x, 1 - y), (1 - x, 1 - y)]


class _GatherJob:
    US_PER_MB = 46.0

    def __init__(self, arrs):
        n = len(arrs)
        self.ins = list(arrs)
        self.out_shape = [jax.ShapeDtypeStruct((N_DEV,) + a.shape, a.dtype) for a in arrs]
        self.sems = [pltpu.SemaphoreType.DMA((n, 7)), pltpu.SemaphoreType.DMA((n, 7)), pltpu.SemaphoreType.DMA((n,))]
        self.cost = self.US_PER_MB * sum(a.size * a.dtype.itemsize for a in arrs) / 1e6

    def _parts(self, ins, outs, sems):
        send_sems, recv_sems, loc_sems = sems
        x, y, c = _mesh_pos()
        n = len(ins)

        def copy(a, k, block, to, src=None):
            dst = outs[a].at[4 * block[0] + 2 * block[1] + block[2]]
            return pltpu.make_async_remote_copy(
                src_ref=dst if src is None else src, dst_ref=dst,
                send_sem=send_sems.at[a, k], recv_sem=recv_sems.at[a, k],
                device_id=to, device_id_type=MESH)

        me, sib, chips = (x, y, c), (x, y, 1 - c), _other_chips(x, y)
        locs = [pltpu.make_async_copy(ins[a], outs[a].at[4 * x + 2 * y + c], loc_sems.at[a]) for a in range(n)]
        first = []
        for a in range(n):
            first.append(copy(a, 0, me, sib, src=ins[a]))
            for j, chip in enumerate(chips):
                first.append(copy(a, 1 + j, me, (*chip, c), src=ins[a]))
        passed = [copy(a, 4 + j, (*chip, c), sib) for j, chip in enumerate(chips) for a in range(n)]
        landed = [copy(a, 1 + j, (*chip, c), me) for j, chip in enumerate(chips) for a in range(n)]
        from_sib = [copy(a, 0, sib, me) for a in range(n)]
        from_sib += [copy(a, 4 + j, (*chip, 1 - c), me) for j, chip in enumerate(chips) for a in range(n)]
        return locs, first, landed, passed, from_sib

    def first(self, ins, outs, sems):
        locs, first, _, _, _ = self._parts(ins, outs, sems)
        for cp in locs + first:
            cp.start()

    def mid(self, ins, outs, sems):
        _, _, landed, passed, _ = self._parts(ins, outs, sems)
        for got, fwd in zip(landed, passed):
            got.wait_recv()
            fwd.start()

    def last(self, ins, outs, sems):
        locs, first, _, passed, from_sib = self._parts(ins, outs, sems)
        for cp in from_sib:
            cp.wait_recv()
        for cp in first + passed:
            cp.wait_send()
        for cp in locs:
            cp.wait()


class _SwapJob:
    US_PER_MB = 0.8

    def __init__(self, parts):
        n = len(parts)
        self.ins = list(parts)
        self.out_shape = [jax.ShapeDtypeStruct((4, 1) + p.shape[2:], p.dtype) for p in parts]
        self.sems = [pltpu.SemaphoreType.DMA((n,)), pltpu.SemaphoreType.DMA((n,))]
        self.cost = 5.0 + self.US_PER_MB * sum(p.size * p.dtype.itemsize for p in parts) / 1e6

    def _copies(self, ins, outs, sems):
        send_sems, recv_sems = sems
        x, y, c = _mesh_pos()
        return [pltpu.make_async_remote_copy(
            src_ref=ins[a].at[:, pl.ds(1 - c, 1)], dst_ref=outs[a],
            send_sem=send_sems.at[a], recv_sem=recv_sems.at[a],
            device_id=(x, y, 1 - c), device_id_type=MESH) for a in range(len(ins))]

    def first(self, ins, outs, sems):
        for cp in self._copies(ins, outs, sems):
            cp.start()

    def mid(self, ins, outs, sems):
        pass

    def last(self, ins, outs, sems):
        for cp in self._copies(ins, outs, sems):
            cp.wait()


class _ChipsJob:
    US_PER_MB = 11.0

    def __init__(self, qs):
        n = len(qs)
        self.ins = list(qs)
        self.out_shape = [jax.ShapeDtypeStruct(q.shape, q.dtype) for q in qs]
        self.sems = [pltpu.SemaphoreType.DMA((n, 3)), pltpu.SemaphoreType.DMA((n, 3)), pltpu.SemaphoreType.DMA((n,))]
        self.cost = 5.0 + self.US_PER_MB * sum(q.size * q.dtype.itemsize for q in qs) / 1e6

    def _copies(self, ins, outs, sems):
        send_sems, recv_sems, loc_sems = sems
        x, y, c = _mesh_pos()
        cps = []
        for a in range(len(ins)):
            cps.append(pltpu.make_async_copy(ins[a].at[2 * x + y], outs[a].at[3], loc_sems.at[a]))
            for k, chip in enumerate(_other_chips(x, y)):
                cps.append(pltpu.make_async_remote_copy(
                    src_ref=ins[a].at[2 * chip[0] + chip[1]], dst_ref=outs[a].at[k],
                    send_sem=send_sems.at[a, k], recv_sem=recv_sems.at[a, k],
                    device_id=(*chip, c), device_id_type=MESH))
        return cps

    def first(self, ins, outs, sems):
        for cp in self._copies(ins, outs, sems):
            cp.start()

    def mid(self, ins, outs, sems):
        pass

    def last(self, ins, outs, sems):
        for cp in self._copies(ins, outs, sems):
            cp.wait()


def _call(body, *, name, out_shape, in_specs, out_specs, args, grid=(), scratch_shapes=(), semantics=(), job=None):
    if job is None:
        outs = pl.pallas_call(
            body, name=name, out_shape=out_shape, grid=grid, in_specs=in_specs, out_specs=out_specs,
            scratch_shapes=list(scratch_shapes), compiler_params=_cp(*semantics))(*args)
        return outs, None
    single = not isinstance(out_shape, (list, tuple))
    host_out = [out_shape] if single else list(out_shape)
    host_ospecs = [out_specs] if single else list(out_specs)
    n_in, n_out, n_scr = len(in_specs), len(host_out), len(scratch_shapes)
    j_in, j_out = len(job.ins), len(job.out_shape)
    total = int(np.prod(grid)) if grid else 1
    mid_step = max(total * 7 // 8, 1)

    def full(*refs):
        h_in, jin = refs[:n_in], refs[n_in:n_in + j_in]
        o = n_in + j_in
        h_out, jout = refs[o:o + n_out], refs[o + n_out:o + n_out + j_out]
        o += n_out + j_out
        h_scr, jsem = refs[o:o + n_scr], refs[o + n_scr:]
        if total == 1:
            job.first(jin, jout, jsem)
            body(*h_in, *h_out, *h_scr)
            job.mid(jin, jout, jsem)
            job.last(jin, jout, jsem)
            return
        step = pl.program_id(0)
        for d in range(1, len(grid)):
            step = step * grid[d] + pl.program_id(d)
        pl.when(step == 0)(lambda: job.first(jin, jout, jsem))
        body(*h_in, *h_out, *h_scr)
        pl.when(step == mid_step)(lambda: job.mid(jin, jout, jsem))
        pl.when(step == total - 1)(lambda: job.last(jin, jout, jsem))

    hbm = pl.BlockSpec(memory_space=pltpu.HBM)
    outs = pl.pallas_call(
        full, name=name, out_shape=host_out + list(job.out_shape), grid=grid,
        in_specs=list(in_specs) + [hbm] * j_in, out_specs=host_ospecs + [hbm] * j_out,
        scratch_shapes=list(scratch_shapes) + list(job.sems),
        compiler_params=_cp(*(("arbitrary",) * len(grid))))(*args, *job.ins)
    host = outs[0] if single else list(outs[:n_out])
    return host, list(outs[n_out:])


def _run_job(job, name):
    def body():
        pass

    return _call(body, name=name, out_shape=[], in_specs=[], out_specs=[], args=[], job=job)[1]


class _Hosts:
    def __init__(self):
        self.pending = []
        self.flushed = 0

    def add(self, job, done):
        self.pending.append((job, done))

    def take(self, duration):
        fits = [e for e in self.pending if e[0].cost <= HOST_SLACK * duration]
        if not fits:
            return None
        best = max(fits, key=lambda e: e[0].cost)
        self.pending.remove(best)
        return best

    def flush(self, name):
        while self.pending:
            job, done = self.pending.pop(0)
            done(_run_job(job, f"{name}{self.flushed}"))
            self.flushed += 1


def _pair_add(part4, recv, core, name):
    _, _, R, C = part4.shape
    tr = _tile(R, max(SUBLANE, (1 << 19) // C), 16)

    def body(core_ref, p_ref, r_ref, o_ref):
        o_ref[...] = (p_ref[...].astype(F32) + r_ref[...].astype(F32)).astype(o_ref.dtype)

    return pl.pallas_call(
        body, name=name,
        out_shape=jax.ShapeDtypeStruct((4, R, C), WIRE),
        grid_spec=pltpu.PrefetchScalarGridSpec(
            num_scalar_prefetch=1, grid=(4, R // tr),
            in_specs=[pl.BlockSpec((None, None, tr, C), lambda s, i, cr: (s, cr[0], i, 0)),
                      pl.BlockSpec((None, None, tr, C), lambda s, i, cr: (s, 0, i, 0))],
            out_specs=pl.BlockSpec((None, tr, C), lambda s, i, cr: (s, i, 0))),
        compiler_params=_cp("parallel", "parallel"),
    )(core, part4, recv)


def _queue_reduce_scatter(hosts, parts, core, name, sink):
    keys = list(parts)
    p4 = [parts[k].reshape((4, 2) + parts[k].shape[1:]) for k in keys]

    def swapped(recv):
        qs = [_pair_add(p, r, core, f"{name}_add{i}") for i, (p, r) in enumerate(zip(p4, recv))]
        hosts.add(_ChipsJob(qs), lambda outs: sink.update(zip(keys, outs)))

    hosts.add(_SwapJob(p4), swapped)


def _hosted(hosts, duration, body, **kw):
    entry = hosts.take(duration) if hosts is not None else None
    out, jouts = _call(body, job=entry[0] if entry else None, **kw)
    if entry:
        entry[1](jouts)
    return out


def _mm(a, b, *, ta=False, tb=False, out_dtype, name, blocks=None, hosts=None):
    M, K = (a.shape[1], a.shape[0]) if ta else a.shape
    N = b.shape[0] if tb else b.shape[1]
    assert K == (b.shape[1] if tb else b.shape[0]), (a.shape, b.shape, ta, tb)
    ns = N // blocks if blocks else N
    tm, tk = _tile(M, MM_TILE, LANE), _tile(K, MM_TILE, LANE)
    tn = _tile(ns, MM_TILE, LANE)
    nk = K // tk
    per = ns // tn

    def body(a_ref, b_ref, o_ref, acc_ref):
        k = pl.program_id(2)

        @pl.when(k == 0)
        def _():
            acc_ref[...] = jnp.zeros_like(acc_ref)

        dims = (((0,) if ta else (1,), (1,) if tb else (0,)), ((), ()))
        acc_ref[...] += lax.dot_general(a_ref[...], b_ref[...], dims, preferred_element_type=F32)

        @pl.when(k == nk - 1)
        def _():
            o_ref[...] = acc_ref[...].astype(o_ref.dtype)

    a_spec = pl.BlockSpec((tk, tm), lambda i, j, k: (k, i)) if ta else pl.BlockSpec((tm, tk), lambda i, j, k: (i, k))
    b_spec = pl.BlockSpec((tn, tk), lambda i, j, k: (j, k)) if tb else pl.BlockSpec((tk, tn), lambda i, j, k: (k, j))
    if blocks:
        out_shape = jax.ShapeDtypeStruct((blocks, M, ns), out_dtype)
        o_spec = pl.BlockSpec((None, tm, tn), lambda i, j, k: (j // per, i, j % per))
    else:
        out_shape = jax.ShapeDtypeStruct((M, N), out_dtype)
        o_spec = pl.BlockSpec((tm, tn), lambda i, j, k: (i, j))
    return _hosted(
        hosts, 2.0 * M * N * K / MXU_FLOP_PER_US, body, name=name, out_shape=out_shape, grid=(M // tm, N // tn, nk),
        in_specs=[a_spec, b_spec], out_specs=o_spec, args=[a, b],
        scratch_shapes=[pltpu.VMEM((tm, tn), F32)], semantics=("parallel", "parallel", "arbitrary"))


def _rms_fwd(xin, g, name, resid=None):
    T, D = xin.shape
    tr = _tile(T, ROW_TILE, SUBLANE)

    def body(*refs):
        x_ref, g_ref = refs[0], refs[1]
        o_ref = refs[-1]
        xv = x_ref[...].astype(F32)
        r = lax.rsqrt(jnp.mean(xv * xv, axis=-1, keepdims=True) + NORM_EPS)
        yv = xv * r * g_ref[...]
        if resid is not None:
            yv = refs[2][...] + yv
        o_ref[...] = yv.astype(o_ref.dtype)

    row = pl.BlockSpec((tr, D), lambda i: (i, 0))
    ins = [xin, g] + ([resid] if resid is not None else [])
    return pl.pallas_call(
        body, name=name, out_shape=jax.ShapeDtypeStruct((T, D), F32 if resid is not None else MXU),
        grid=(T // tr,), in_specs=[row, pl.BlockSpec((1, D), lambda i: (0, 0))] + ([row] if resid is not None else []),
        out_specs=row, compiler_params=_cp("parallel"),
    )(*ins)


def _rms_bwd(xin, dout, g, name, resid=None, out_dtype=F32):
    T, D = xin.shape
    tr = _tile(T, ROW_TILE, SUBLANE)

    def body(*refs):
        x_ref, d_ref, g_ref = refs[:3]
        dx_ref, dg_ref = refs[-2:]
        xv = x_ref[...].astype(F32)
        dv = d_ref[...].astype(F32)
        r = lax.rsqrt(jnp.mean(xv * xv, axis=-1, keepdims=True) + NORM_EPS)
        xh = xv * r
        dh = dv * g_ref[...]
        dxv = r * (dh - xh * jnp.mean(dh * xh, axis=-1, keepdims=True))
        if resid is not None:
            dxv = refs[3][...] + dxv
        dx_ref[...] = dxv.astype(dx_ref.dtype)

        @pl.when(pl.program_id(0) == 0)
        def _():
            dg_ref[...] = jnp.zeros_like(dg_ref)

        dg_ref[...] += jnp.sum(dv * xh, axis=0, keepdims=True)

    row = pl.BlockSpec((tr, D), lambda i: (i, 0))
    vec = pl.BlockSpec((1, D), lambda i: (0, 0))
    ins = [xin, dout, g] + ([resid] if resid is not None else [])
    return pl.pallas_call(
        body, name=name,
        out_shape=[jax.ShapeDtypeStruct((T, D), out_dtype), jax.ShapeDtypeStruct((1, D), F32)],
        grid=(T // tr,), in_specs=[row, row, vec] + ([row] if resid is not None else []),
        out_specs=[row, vec], compiler_params=_cp("arbitrary"),
    )(*ins)


def _loss_head(y, target, name):
    T, D = y.shape
    tr = _tile(T, ROW_TILE, SUBLANE)

    def body(y_ref, t_ref, d_ref, l_ref):
        e = y_ref[...] - t_ref[...]
        d_ref[...] = e * (1.0 / D)

        @pl.when(pl.program_id(0) == 0)
        def _():
            l_ref[...] = jnp.zeros_like(l_ref)

        rows = jnp.sum(e * e, axis=-1, keepdims=True) * (1.0 / D)
        l_ref[...] += jnp.broadcast_to(0.5 * jnp.sum(rows, axis=0, keepdims=True), l_ref.shape)

    row = pl.BlockSpec((tr, D), lambda i: (i, 0))
    return pl.pallas_call(
        body, name=name,
        out_shape=[jax.ShapeDtypeStruct((T, D), F32), jax.ShapeDtypeStruct((1, LANE), F32)],
        grid=(T // tr,), in_specs=[row, row], out_specs=[row, pl.BlockSpec((1, LANE), lambda i: (0, 0))],
        compiler_params=_cp("arbitrary"),
    )(y, target)


def _mla_norms_fwd(P, gq, gkv, qr, kvr, name):
    T = P.shape[0]
    tr = _tile(T, ROW_TILE, SUBLANE)

    def body(cq_ref, ckv_ref, gq_ref, gkv_ref, oq_ref, okv_ref):
        for x_ref, g_ref, o_ref in ((cq_ref, gq_ref, oq_ref), (ckv_ref, gkv_ref, okv_ref)):
            xv = x_ref[...].astype(F32)
            r = lax.rsqrt(jnp.mean(xv * xv, axis=-1, keepdims=True) + NORM_EPS)
            o_ref[...] = (xv * r * g_ref[...]).astype(o_ref.dtype)

    return pl.pallas_call(
        body, name=name,
        out_shape=[jax.ShapeDtypeStruct((T, qr), MXU), jax.ShapeDtypeStruct((T, kvr), MXU)],
        grid=(T // tr,),
        in_specs=[pl.BlockSpec((tr, qr), lambda i: (i, 0)), pl.BlockSpec((tr, kvr), lambda i: (i, qr // kvr)),
                  pl.BlockSpec((1, qr), lambda i: (0, 0)), pl.BlockSpec((1, kvr), lambda i: (0, 0))],
        out_specs=[pl.BlockSpec((tr, qr), lambda i: (i, 0)), pl.BlockSpec((tr, kvr), lambda i: (i, 0))],
        compiler_params=_cp("parallel"),
    )(P, P, gq, gkv)


def _mla_norms_bwd(P, dcqn, dckvn_k, dckvn_v, dkr, gq, gkv, qr, kvr, pa, name):
    T = P.shape[0]
    tr = _tile(T, ROW_TILE, SUBLANE)
    c2 = qr + kvr

    def body(cq_ref, ckv_ref, dq_ref, dk_ref, dv_ref, dkr_ref, gq_ref, gkv_ref, dp_ref, dgq_ref, dgkv_ref):
        @pl.when(pl.program_id(0) == 0)
        def _():
            dgq_ref[...] = jnp.zeros_like(dgq_ref)
            dgkv_ref[...] = jnp.zeros_like(dgkv_ref)

        def one(x_ref, dv, g_ref, dg_ref):
            xv = x_ref[...].astype(F32)
            r = lax.rsqrt(jnp.mean(xv * xv, axis=-1, keepdims=True) + NORM_EPS)
            xh = xv * r
            dh = dv * g_ref[...]
            dg_ref[...] += jnp.sum(dv * xh, axis=0, keepdims=True)
            return r * (dh - xh * jnp.mean(dh * xh, axis=-1, keepdims=True))

        dp_ref[:, 0:qr] = one(cq_ref, dq_ref[...], gq_ref, dgq_ref).astype(dp_ref.dtype)
        dp_ref[:, qr:c2] = one(ckv_ref, dk_ref[...] + dv_ref[...], gkv_ref, dgkv_ref).astype(dp_ref.dtype)
        dp_ref[:, c2:c2 + LANE] = dkr_ref[...].astype(dp_ref.dtype)
        if pa > c2 + LANE:
            dp_ref[:, c2 + LANE:pa] = jnp.zeros((tr, pa - c2 - LANE), dp_ref.dtype)

    return pl.pallas_call(
        body, name=name,
        out_shape=[jax.ShapeDtypeStruct((T, pa), MXU), jax.ShapeDtypeStruct((1, qr), F32),
                   jax.ShapeDtypeStruct((1, kvr), F32)],
        grid=(T // tr,),
        in_specs=[pl.BlockSpec((tr, qr), lambda i: (i, 0)), pl.BlockSpec((tr, kvr), lambda i: (i, qr // kvr)),
                  pl.BlockSpec((tr, qr), lambda i: (i, 0)), pl.BlockSpec((tr, kvr), lambda i: (i, 0)),
                  pl.BlockSpec((tr, kvr), lambda i: (i, 0)), pl.BlockSpec((tr, LANE), lambda i: (i, 0)),
                  pl.BlockSpec((1, qr), lambda i: (0, 0)), pl.BlockSpec((1, kvr), lambda i: (0, 0))],
        out_specs=[pl.BlockSpec((tr, pa), lambda i: (i, 0)), pl.BlockSpec((1, qr), lambda i: (0, 0)),
                   pl.BlockSpec((1, kvr), lambda i: (0, 0))],
        compiler_params=_cp("arbitrary"),
    )(P, P, dcqn, dckvn_k, dckvn_v, dkr, gq, gkv)


def _rope_tables(pos_ref, invf_ref):
    ang = pos_ref[...].astype(F32) * invf_ref[...]
    lane = lax.broadcasted_iota(jnp.int32, ang.shape, 1)
    cos, sin = jnp.cos(ang), jnp.sin(ang)
    half = ROPE // 2
    c = jnp.where(lane < ROPE, cos, 0.0)
    s1 = jnp.where(lane < half, -sin, 0.0)
    s2 = jnp.where((lane >= half) & (lane < ROPE), sin, 0.0)
    return c, s1, s2


def _rope_fwd(qpad, kpad, P, pos, invf, kr_blk, H, name):
    T = qpad.shape[0]
    tr = _tile(T, ROW_TILE, SUBLANE)

    def body(q_ref, k_ref, kr_ref, pos_ref, invf_ref, qo_ref, ko_ref):
        c, s1, s2 = _rope_tables(pos_ref, invf_ref)

        def rot(v):
            return v * c + pltpu.roll(v, LANE - ROPE // 2, 1) * s1 + pltpu.roll(v, ROPE // 2, 1) * s2

        kr = rot(kr_ref[...].astype(F32)).astype(ko_ref.dtype)
        for h in range(H):
            lo = h * QPAD
            qo_ref[:, lo:lo + NOPE] = q_ref[:, lo:lo + NOPE]
            qo_ref[:, lo + NOPE:lo + QPAD] = rot(q_ref[:, lo + NOPE:lo + QPAD].astype(F32)).astype(qo_ref.dtype)
            ko_ref[:, lo:lo + NOPE] = k_ref[:, lo:lo + NOPE]
            ko_ref[:, lo + NOPE:lo + QPAD] = kr

    wide = pl.BlockSpec((tr, H * QPAD), lambda i: (i, 0))
    return pl.pallas_call(
        body, name=name,
        out_shape=[jax.ShapeDtypeStruct(qpad.shape, MXU), jax.ShapeDtypeStruct(kpad.shape, MXU)],
        grid=(T // tr,),
        in_specs=[wide, wide, pl.BlockSpec((tr, LANE), lambda i: (i, kr_blk)),
                  pl.BlockSpec((tr, 1), lambda i: (i, 0)), pl.BlockSpec((1, LANE), lambda i: (0, 0))],
        out_specs=[wide, wide], compiler_params=_cp("parallel"),
    )(qpad, kpad, P, pos, invf)


def _rope_bwd(dqcat, dkcat, pos, invf, H, name):
    T = dqcat.shape[0]
    tr = _tile(T, ROW_TILE, SUBLANE)

    def body(dq_ref, dk_ref, pos_ref, invf_ref, dqo_ref, dkr_ref):
        c, s1, s2 = _rope_tables(pos_ref, invf_ref)

        def rot_t(v):
            return v * c + pltpu.roll(v * s1, ROPE // 2, 1) + pltpu.roll(v * s2, LANE - ROPE // 2, 1)

        acc = jnp.zeros((tr, LANE), F32)
        for h in range(H):
            lo = h * QPAD
            dqo_ref[:, lo:lo + NOPE] = dq_ref[:, lo:lo + NOPE]
            dqo_ref[:, lo + NOPE:lo + QPAD] = rot_t(dq_ref[:, lo + NOPE:lo + QPAD].astype(F32)).astype(dqo_ref.dtype)
            acc = acc + dk_ref[:, lo + NOPE:lo + QPAD].astype(F32)
        dkr_ref[...] = rot_t(acc)

    wide = pl.BlockSpec((tr, H * QPAD), lambda i: (i, 0))
    return pl.pallas_call(
        body, name=name,
        out_shape=[jax.ShapeDtypeStruct(dqcat.shape, MXU), jax.ShapeDtypeStruct((T, LANE), F32)],
        grid=(T // tr,),
        in_specs=[wide, wide, pl.BlockSpec((tr, 1), lambda i: (i, 0)), pl.BlockSpec((1, LANE), lambda i: (0, 0))],
        out_specs=[wide, pl.BlockSpec((tr, LANE), lambda i: (i, 0))], compiler_params=_cp("parallel"),
    )(dqcat, dkcat, pos, invf)


def _causal_mask(s):
    row = lax.broadcasted_iota(jnp.int32, s.shape, 0)
    col = lax.broadcasted_iota(jnp.int32, s.shape, 1)
    return jnp.where(col <= row, s, -1e30)


def _flash_fwd(qcat, kcat, v, P, z_blk, H, name, hosts=None):
    T = qcat.shape[0]
    tq = _tile(T, ATT_TILE, LANE)
    hp = 2 if H % 2 == 0 and z_blk % 2 == 0 else 1
    c2 = ATT_SCALE * LOG2E

    def body(q_ref, k_ref, v_ref, z_ref, o_ref, a_ref, lse_ref, m_s, l_s, acc_s):
        i = pl.program_id(1)
        m_s[...] = jnp.full_like(m_s, -1e30)
        l_s[...] = jnp.zeros_like(l_s)
        acc_s[...] = jnp.zeros_like(acc_s)

        def step(j, masked):
            r0 = pl.multiple_of(j * tq, tq)
            ones = jnp.ones((tq, LANE), MXU)
            for h in range(hp):
                q = q_ref[:, h * QPAD:(h + 1) * QPAD]
                kb = k_ref[pl.ds(r0, tq), h * QPAD:(h + 1) * QPAD]
                vb = v_ref[pl.ds(r0, tq), h * VHEAD:(h + 1) * VHEAD]
                s = lax.dot_general(q, kb, NT, preferred_element_type=F32)
                if masked:
                    s = _causal_mask(s)
                m_prev = m_s[h]
                m_new = jnp.maximum(m_prev, jnp.max(s, axis=1, keepdims=True))
                p = jnp.exp2((s - m_new) * c2).astype(MXU)
                alpha = jnp.exp2((m_prev - m_new) * c2)
                l_s[h] = alpha * l_s[h] + lax.dot_general(p, ones, NN, preferred_element_type=F32)
                acc_s[h] = alpha * acc_s[h] + lax.dot_general(p, vb, NN, preferred_element_type=F32)
                m_s[h] = m_new

        def loop(j, carry):
            step(j, False)
            return carry

        lax.fori_loop(0, i, loop, 0)
        step(i, True)
        for h in range(hp):
            l = l_s[h]
            o = acc_s[h] / l
            cols = slice(h * VHEAD, (h + 1) * VHEAD)
            o_ref[:, cols] = o.astype(o_ref.dtype)
            a_ref[:, cols] = (o * _silu(z_ref[:, cols].astype(F32))).astype(a_ref.dtype)
            lse_ref[h] = m_s[h] * ATT_SCALE + jnp.log(l)

    return _hosted(
        hosts, ATT_FWD_US * (T / 4096.0) ** 2 * (H / 32.0), body, name=name,
        out_shape=[jax.ShapeDtypeStruct((T, H * VHEAD), MXU), jax.ShapeDtypeStruct((T, H * VHEAD), MXU),
                   jax.ShapeDtypeStruct((H, T, LANE), F32)],
        grid=(H // hp, T // tq),
        in_specs=[pl.BlockSpec((tq, hp * QPAD), lambda h, i: (i, h)), pl.BlockSpec((T, hp * QPAD), lambda h, i: (0, h)),
                  pl.BlockSpec((T, hp * VHEAD), lambda h, i: (0, h)),
                  pl.BlockSpec((tq, hp * VHEAD), lambda h, i: (i, z_blk // hp + h))],
        out_specs=[pl.BlockSpec((tq, hp * VHEAD), lambda h, i: (i, h)), pl.BlockSpec((tq, hp * VHEAD), lambda h, i: (i, h)),
                   pl.BlockSpec((hp, tq, LANE), lambda h, i: (h, i, 0))],
        scratch_shapes=[pltpu.VMEM((hp, tq, 1), F32), pltpu.VMEM((hp, tq, LANE), F32), pltpu.VMEM((hp, tq, VHEAD), F32)],
        args=[qcat, kcat, v, P], semantics=("parallel", "arbitrary"))


def _flash_bwd(qcat, kcat, v, do, lse, delta, H, name, hosts=None):
    T = qcat.shape[0]
    tq = _tile(T, ATT_TILE, LANE)
    nq = T // tq
    c2 = ATT_SCALE * LOG2E

    def body(q_ref, do_ref, lse_ref, dl_ref, k_ref, v_ref, dq_ref, dk_ref, dv_ref, dq_s, dk_s, dv_s):
        j = pl.program_id(1)
        kb, vb = k_ref[...], v_ref[...]
        dk_s[...] = jnp.zeros_like(dk_s)
        dv_s[...] = jnp.zeros_like(dv_s)

        @pl.when(j == 0)
        def _():
            dq_s[...] = jnp.zeros_like(dq_s)

        def step(i, masked):
            rows = pl.ds(pl.multiple_of(i * tq, tq), tq)
            qb = q_ref[rows, :]
            dob = do_ref[rows, :]
            s = lax.dot_general(qb, kb, NT, preferred_element_type=F32)
            if masked:
                s = _causal_mask(s)
            p = jnp.exp2(s * c2 - lse_ref[rows, 0:1] * LOG2E)
            dv_s[...] += lax.dot_general(p.astype(dob.dtype), dob, TN, preferred_element_type=F32)
            dp = lax.dot_general(dob, vb, NT, preferred_element_type=F32)
            ds = (p * (dp - dl_ref[rows, 0:1]) * ATT_SCALE).astype(qb.dtype)
            dk_s[...] += lax.dot_general(ds, qb, TN, preferred_element_type=F32)
            dq_s[rows, :] += lax.dot_general(ds, kb, NN, preferred_element_type=F32)

        def loop(i, carry):
            step(i, False)
            return carry

        step(j, True)
        lax.fori_loop(j + 1, nq, loop, 0)
        dk_ref[...] = dk_s[...].astype(dk_ref.dtype)
        dv_ref[...] = dv_s[...].astype(dv_ref.dtype)

        @pl.when(j == nq - 1)
        def _():
            dq_ref[...] = dq_s[...].astype(dq_ref.dtype)

    return _hosted(
        hosts, ATT_BWD_US * (T / 4096.0) ** 2 * (H / 32.0), body, name=name,
        out_shape=[jax.ShapeDtypeStruct(qcat.shape, MXU), jax.ShapeDtypeStruct(kcat.shape, MXU),
                   jax.ShapeDtypeStruct(v.shape, MXU)],
        grid=(H, nq),
        in_specs=[pl.BlockSpec((T, QPAD), lambda h, j: (0, h)), pl.BlockSpec((T, VHEAD), lambda h, j: (0, h)),
                  pl.BlockSpec((None, T, LANE), lambda h, j: (h, 0, 0)), pl.BlockSpec((None, T, LANE), lambda h, j: (h, 0, 0)),
                  pl.BlockSpec((tq, QPAD), lambda h, j: (j, h)), pl.BlockSpec((tq, VHEAD), lambda h, j: (j, h))],
        out_specs=[pl.BlockSpec((T, QPAD), lambda h, j: (0, h)), pl.BlockSpec((tq, QPAD), lambda h, j: (j, h)),
                   pl.BlockSpec((tq, VHEAD), lambda h, j: (j, h))],
        scratch_shapes=[pltpu.VMEM((T, QPAD), F32), pltpu.VMEM((tq, QPAD), F32), pltpu.VMEM((tq, VHEAD), F32)],
        args=[qcat, do, lse, delta, kcat, v], semantics=("parallel", "arbitrary"))


def _mla_gate_bwd(da, o, P, pa, H, name):
    T, HV = da.shape
    tr = _tile(T, ROW_TILE, SUBLANE)
    cw = _tile(math.gcd(pa, HV), 512, LANE)
    hb = cw // VHEAD

    def body(da_ref, o_ref, z_ref, do_ref, dz_ref, dl_ref):
        dav, ov, zv = da_ref[...].astype(F32), o_ref[...].astype(F32), z_ref[...].astype(F32)
        dov = dav * _silu(zv)
        do_ref[...] = dov.astype(do_ref.dtype)
        dz_ref[...] = (dav * ov * _silu_grad(zv)).astype(dz_ref.dtype)
        prod = dov * ov
        for h in range(hb):
            dl_ref[h] = jnp.broadcast_to(jnp.sum(prod[:, h * VHEAD:(h + 1) * VHEAD], axis=1, keepdims=True), (tr, LANE))

    blk = pl.BlockSpec((tr, cw), lambda i, j: (i, j))
    shifted = pl.BlockSpec((tr, cw), lambda i, j: (i, pa // cw + j))
    return pl.pallas_call(
        body, name=name,
        out_shape=[jax.ShapeDtypeStruct((T, HV), MXU), jax.ShapeDtypeStruct((T, HV), MXU),
                   jax.ShapeDtypeStruct((H, T, LANE), F32)],
        grid=(T // tr, HV // cw),
        in_specs=[blk, blk, shifted],
        out_specs=[blk, blk, pl.BlockSpec((hb, tr, LANE), lambda i, j: (j, i, 0))],
        compiler_params=_cp("parallel", "parallel"),
    )(da, o, P)


def _bank_rows(width):
    return min(SUBLANE, width), -(-(width - 1) // SUBLANE) * SUBLANE


def _bank_fill(bank_ref, val, width, T, causal):
    nr, hp = _bank_rows(width)
    rows = lax.broadcasted_iota(jnp.int32, val.shape, 0)
    zero = jnp.zeros((hp,) + val.shape[1:], F32)
    for r in range(nr):
        if causal:
            bank_ref[r, 0:hp, :] = zero
            bank_ref[r, hp:hp + T, :] = val if r == 0 else jnp.where(rows >= r, pltpu.roll(val, r, 0), 0.0)
        else:
            bank_ref[r, T:T + hp, :] = zero
            bank_ref[r, 0:T, :] = val if r == 0 else jnp.where(rows < T - r, pltpu.roll(val, T - r, 0), 0.0)


def _bank_tap(bank_ref, s, t0, tc, width, causal):
    _, hp = _bank_rows(width)
    q, r = divmod(s, SUBLANE)
    off = hp - SUBLANE * q if causal else SUBLANE * q
    return bank_ref[r, pl.ds(pl.multiple_of(t0 + off, SUBLANE), tc), :]


def _conv_chunk(bank_ref, w_ref, t0, tc, width):
    acc = None
    for s in range(width):
        k = width - 1 - s
        term = w_ref[k:k + 1, :] * _bank_tap(bank_ref, s, t0, tc, width, True)
        acc = term if acc is None else acc + term
    return acc


def _conv_bwd_chunk(bank_ref, w_ref, xin, dw_ref, t0, tc, width):
    acc = None
    for s in range(width):
        k = width - 1 - s
        tap = _bank_tap(bank_ref, s, t0, tc, width, False)
        term = w_ref[k:k + 1, :] * tap
        acc = term if acc is None else acc + term
        dw_ref[k] += jnp.sum((tap * xin).reshape(tc // SUBLANE, SUBLANE, xin.shape[1]), axis=0)
    return acc


def _full_t(T, cb, col):
    return pl.BlockSpec((T, cb), lambda j, col=col: (0, col + j))


def _sc_fwd(P, w, D, name):
    T = P.shape[0]
    cb = _tile(D, CONV_CB, LANE)
    nb = D // cb
    width = w.shape[0]
    tc = _tile(T, CONV_CHUNK, SUBLANE)
    nr, hp = _bank_rows(width)

    def body(b_ref, c_ref, u_ref, z_ref, w_ref, a_ref, bank):
        _bank_fill(bank, c_ref[...].astype(F32) * u_ref[...].astype(F32), width, T, True)

        def chunk(ci, carry):
            t0 = pl.multiple_of(ci * tc, tc)
            rows = pl.ds(t0, tc)
            v = _conv_chunk(bank, w_ref, t0, tc, width)
            a_ref[rows, :] = (b_ref[rows, :].astype(F32) * v * _silu(z_ref[rows, :].astype(F32))).astype(a_ref.dtype)
            return carry

        lax.fori_loop(0, T // tc, chunk, 0)

    return pl.pallas_call(
        body, name=name, out_shape=jax.ShapeDtypeStruct((T, D), MXU), grid=(nb,),
        in_specs=[_full_t(T, cb, 0), _full_t(T, cb, nb), _full_t(T, cb, 2 * nb), _full_t(T, cb, 3 * nb),
                  pl.BlockSpec((width, cb), lambda j: (0, j))],
        out_specs=_full_t(T, cb, 0),
        scratch_shapes=[pltpu.VMEM((nr, T + hp, cb), F32)],
        compiler_params=_cp("parallel"),
    )(P, P, P, P, w)


def _sc_bwd(P, da, w, D, name):
    T = P.shape[0]
    cb = _tile(D, CONV_CB, LANE)
    nb = D // cb
    width = w.shape[0]
    tc = _tile(T, CONV_CHUNK, SUBLANE)
    nr, hp = _bank_rows(width)

    def body(b_ref, c_ref, u_ref, z_ref, da_ref, w_ref, db_ref, dc_ref, du_ref, dz_ref, dw_ref, bank, dv_s, dw_s):
        _bank_fill(bank, c_ref[...].astype(F32) * u_ref[...].astype(F32), width, T, True)

        def first(ci, carry):
            t0 = pl.multiple_of(ci * tc, tc)
            rows = pl.ds(t0, tc)
            v = _conv_chunk(bank, w_ref, t0, tc, width)
            bv, zv, dav = b_ref[rows, :].astype(F32), z_ref[rows, :].astype(F32), da_ref[rows, :].astype(F32)
            dyb = dav * _silu(zv)
            dz_ref[rows, :] = (dav * bv * v * _silu_grad(zv)).astype(dz_ref.dtype)
            db_ref[rows, :] = (dyb * v).astype(db_ref.dtype)
            dv_s[rows, :] = dyb * bv
            return carry

        lax.fori_loop(0, T // tc, first, 0)
        _bank_fill(bank, dv_s[...], width, T, False)
        dw_s[...] = jnp.zeros_like(dw_s)

        def second(ci, carry):
            t0 = pl.multiple_of(ci * tc, tc)
            rows = pl.ds(t0, tc)
            cv, uv = c_ref[rows, :].astype(F32), u_ref[rows, :].astype(F32)
            dcu = _conv_bwd_chunk(bank, w_ref, cv * uv, dw_s, t0, tc, width)
            dc_ref[rows, :] = (dcu * uv).astype(dc_ref.dtype)
            du_ref[rows, :] = (dcu * cv).astype(du_ref.dtype)
            return carry

        lax.fori_loop(0, T // tc, second, 0)
        dw_ref[...] = jnp.sum(dw_s[...], axis=1)

    outs = pl.pallas_call(
        body, name=name,
        out_shape=[jax.ShapeDtypeStruct((T, D), MXU)] * 4 + [jax.ShapeDtypeStruct((width, D), F32)],
        grid=(nb,),
        in_specs=[_full_t(T, cb, 0), _full_t(T, cb, nb), _full_t(T, cb, 2 * nb), _full_t(T, cb, 3 * nb),
                  _full_t(T, cb, 0), pl.BlockSpec((width, cb), lambda j: (0, j))],
        out_specs=[_full_t(T, cb, 0)] * 4 + [pl.BlockSpec((width, cb), lambda j: (0, j))],
        scratch_shapes=[pltpu.VMEM((nr, T + hp, cb), F32), pltpu.VMEM((T, cb), F32),
                        pltpu.VMEM((width, SUBLANE, cb), F32)],
        compiler_params=_cp("parallel"),
    )(P, P, P, P, da, w)
    return outs[:4], outs[4]


def _cf_conv_fwd(P, w, bias, D, name):
    T = P.shape[0]
    cb = _tile(D, CONV_CB, LANE)
    nb = D // cb
    width = w.shape[0]
    tc = _tile(T, CONV_CHUNK, SUBLANE)
    nr, hp = _bank_rows(width)

    def body(a_ref, g_ref, w_ref, b_ref, y_ref, bank):
        _bank_fill(bank, a_ref[...].astype(F32) * _sig(g_ref[...].astype(F32)), width, T, True)

        def chunk(ci, carry):
            t0 = pl.multiple_of(ci * tc, tc)
            y_ref[pl.ds(t0, tc), :] = (_conv_chunk(bank, w_ref, t0, tc, width) + b_ref[...]).astype(y_ref.dtype)
            return carry

        lax.fori_loop(0, T // tc, chunk, 0)

    return pl.pallas_call(
        body, name=name, out_shape=jax.ShapeDtypeStruct((T, D), F32), grid=(nb,),
        in_specs=[_full_t(T, cb, 0), _full_t(T, cb, nb), pl.BlockSpec((width, cb), lambda j: (0, j)),
                  pl.BlockSpec((1, cb), lambda j: (0, j))],
        out_specs=_full_t(T, cb, 0),
        scratch_shapes=[pltpu.VMEM((nr, T + hp, cb), F32)],
        compiler_params=_cp("parallel"),
    )(P, P, w, bias)


def _cf_conv_bwd(P, dyc, w, D, name):
    T = P.shape[0]
    cb = _tile(D, CONV_CB, LANE)
    nb = D // cb
    width = w.shape[0]
    tc = _tile(T, CONV_CHUNK, SUBLANE)
    nr, hp = _bank_rows(width)

    def body(a_ref, g_ref, dy_ref, w_ref, da_ref, dg_ref, dw_ref, db_ref, bank, dw_s):
        dyv = dy_ref[...]
        db_ref[...] = jnp.sum(dyv, axis=0, keepdims=True)
        _bank_fill(bank, dyv, width, T, False)
        dw_s[...] = jnp.zeros_like(dw_s)

        def chunk(ci, carry):
            t0 = pl.multiple_of(ci * tc, tc)
            rows = pl.ds(t0, tc)
            av, sg = a_ref[rows, :].astype(F32), _sig(g_ref[rows, :].astype(F32))
            dyg = _conv_bwd_chunk(bank, w_ref, av * sg, dw_s, t0, tc, width)
            da_ref[rows, :] = (dyg * sg).astype(da_ref.dtype)
            dg_ref[rows, :] = (dyg * av * sg * (1.0 - sg)).astype(dg_ref.dtype)
            return carry

        lax.fori_loop(0, T // tc, chunk, 0)
        dw_ref[...] = jnp.sum(dw_s[...], axis=1)

    return pl.pallas_call(
        body, name=name,
        out_shape=[jax.ShapeDtypeStruct((T, D), MXU), jax.ShapeDtypeStruct((T, D), MXU),
                   jax.ShapeDtypeStruct((width, D), F32), jax.ShapeDtypeStruct((1, D), F32)],
        grid=(nb,),
        in_specs=[_full_t(T, cb, 0), _full_t(T, cb, nb), _full_t(T, cb, 0),
                  pl.BlockSpec((width, cb), lambda j: (0, j))],
        out_specs=[_full_t(T, cb, 0), _full_t(T, cb, 0), pl.BlockSpec((width, cb), lambda j: (0, j)),
                   pl.BlockSpec((1, cb), lambda j: (0, j))],
        scratch_shapes=[pltpu.VMEM((nr, T + hp, cb), F32), pltpu.VMEM((width, SUBLANE, cb), F32)],
        compiler_params=_cp("parallel"),
    )(P, P, dyc, w)


def _layer_norm_stats(v):
    mu = jnp.mean(v, axis=-1, keepdims=True)
    cen = v - mu
    rstd = lax.rsqrt(jnp.mean(cen * cen, axis=-1, keepdims=True) + LN_EPS)
    return cen * rstd, rstd


def _layer_norm_bwd(dxh, xh, rstd):
    return rstd * (dxh - jnp.mean(dxh, axis=-1, keepdims=True) - xh * jnp.mean(dxh * xh, axis=-1, keepdims=True))


def _cf_act_fwd(yc, P, lg, lb, D, name):
    T = yc.shape[0]
    tr = _tile(T, ROW_TILE, SUBLANE)

    def body(y_ref, z_ref, g_ref, b_ref, o_ref):
        xh, _ = _layer_norm_stats(y_ref[...])
        yl = xh * g_ref[...] + b_ref[...]
        o_ref[...] = (_silu(yl) * _silu(z_ref[...].astype(F32))).astype(o_ref.dtype)

    row = pl.BlockSpec((tr, D), lambda i: (i, 0))
    vec = pl.BlockSpec((1, D), lambda i: (0, 0))
    return pl.pallas_call(
        body, name=name, out_shape=jax.ShapeDtypeStruct((T, D), MXU), grid=(T // tr,),
        in_specs=[row, pl.BlockSpec((tr, D), lambda i: (i, 2)), vec, vec], out_specs=row,
        compiler_params=_cp("parallel"),
    )(yc, P, lg, lb)


def _cf_act_bwd(yc, P, da, lg, lb, D, name):
    T = yc.shape[0]
    tr = _tile(T, ROW_TILE, SUBLANE)

    def body(y_ref, z_ref, da_ref, g_ref, b_ref, dy_ref, dz_ref, dg_ref, db_ref):
        @pl.when(pl.program_id(0) == 0)
        def _():
            dg_ref[...] = jnp.zeros_like(dg_ref)
            db_ref[...] = jnp.zeros_like(db_ref)

        xh, rstd = _layer_norm_stats(y_ref[...])
        yl = xh * g_ref[...] + b_ref[...]
        zv, dav = z_ref[...].astype(F32), da_ref[...].astype(F32)
        dz_ref[...] = (dav * _silu(yl) * _silu_grad(zv)).astype(dz_ref.dtype)
        dyl = dav * _silu(zv) * _silu_grad(yl)
        dg_ref[...] += jnp.sum(dyl * xh, axis=0, keepdims=True)
        db_ref[...] += jnp.sum(dyl, axis=0, keepdims=True)
        dy_ref[...] = _layer_norm_bwd(dyl * g_ref[...], xh, rstd)

    row = pl.BlockSpec((tr, D), lambda i: (i, 0))
    zcol = pl.BlockSpec((tr, D), lambda i: (i, 2))
    vec = pl.BlockSpec((1, D), lambda i: (0, 0))
    return pl.pallas_call(
        body, name=name,
        out_shape=[jax.ShapeDtypeStruct((T, D), F32), jax.ShapeDtypeStruct((T, D), MXU),
                   jax.ShapeDtypeStruct((1, D), F32), jax.ShapeDtypeStruct((1, D), F32)],
        grid=(T // tr,), in_specs=[row, zcol, row, vec, vec], out_specs=[row, row, vec, vec],
        compiler_params=_cp("arbitrary"),
    )(yc, P, da, lg, lb)


def _tril(w):
    row = lax.broadcasted_iota(jnp.int32, w.shape, 0)
    col = lax.broadcasted_iota(jnp.int32, w.shape, 1)
    return jnp.where(col <= row, w, 0.0)


def _gm_fwd(P, lg, lb, ws, bst, D, name):
    T = P.shape[0]
    G, ch, _ = ws.shape
    gw = D // G

    def body(p_ref, g_ref, b_ref, ws_ref, bs_ref, a_ref):
        uv, vv, zv = (p_ref[:, k * D:(k + 1) * D].astype(F32) for k in range(3))
        xh, _ = _layer_norm_stats(_gelu(vv))
        vn = (xh * g_ref[...] + b_ref[...]).astype(MXU)
        gate = _gelu(uv) * _silu(zv)
        for g in range(G):
            cols = slice(g * gw, (g + 1) * gw)
            s = lax.dot_general(_tril(ws_ref[g]).astype(MXU), vn[:, cols], NN, preferred_element_type=F32)
            a_ref[:, cols] = (gate[:, cols] * (s + bs_ref[:, g:g + 1])).astype(a_ref.dtype)

    vec = pl.BlockSpec((1, D), lambda i: (0, 0))
    return pl.pallas_call(
        body, name=name, out_shape=jax.ShapeDtypeStruct((T, D), MXU), grid=(T // ch,),
        in_specs=[pl.BlockSpec((ch, 3 * D), lambda i: (i, 0)), vec, vec,
                  pl.BlockSpec((G, ch, ch), lambda i: (0, 0, 0)), pl.BlockSpec((ch, G), lambda i: (0, 0))],
        out_specs=pl.BlockSpec((ch, D), lambda i: (i, 0)), compiler_params=_cp("parallel"),
    )(P, lg, lb, ws, bst)


def _gm_bwd(P, da, lg, lb, ws, bst, D, name):
    T = P.shape[0]
    G, ch, _ = ws.shape
    gw = D // G

    def body(p_ref, da_ref, g_ref, b_ref, ws_ref, bs_ref, dp_ref, dg_ref, db_ref, dws_ref, dbs_ref, dvn_s):
        @pl.when(pl.program_id(0) == 0)
        def _():
            dg_ref[...] = jnp.zeros_like(dg_ref)
            db_ref[...] = jnp.zeros_like(db_ref)
            dws_ref[...] = jnp.zeros_like(dws_ref)
            dbs_ref[...] = jnp.zeros_like(dbs_ref)

        uv, vv, zv = (p_ref[:, k * D:(k + 1) * D].astype(F32) for k in range(3))
        dav = da_ref[...].astype(F32)
        xh, rstd = _layer_norm_stats(_gelu(vv))
        vn = (xh * g_ref[...] + b_ref[...]).astype(MXU)
        ug, sz = _gelu(uv), _silu(zv)
        ds_all = dav * sz * ug
        for g in range(G):
            cols = slice(g * gw, (g + 1) * gw)
            wm = _tril(ws_ref[g]).astype(MXU)
            s = lax.dot_general(wm, vn[:, cols], NN, preferred_element_type=F32) + bs_ref[:, g:g + 1]
            dp_ref[:, g * gw:(g + 1) * gw] = (dav[:, cols] * sz[:, cols] * s * _gelu_grad(uv[:, cols])).astype(dp_ref.dtype)
            dp_ref[:, 2 * D + g * gw:2 * D + (g + 1) * gw] = (
                dav[:, cols] * ug[:, cols] * s * _silu_grad(zv[:, cols])).astype(dp_ref.dtype)
            ds = ds_all[:, cols]
            dsb = ds.astype(MXU)
            dvn_s[:, cols] = lax.dot_general(wm, dsb, TN, preferred_element_type=F32)
            dws_ref[g] += _tril(lax.dot_general(dsb, vn[:, cols], NT, preferred_element_type=F32))
            dbs_ref[g] += jnp.broadcast_to(jnp.sum(ds, axis=1, keepdims=True), (ch, LANE))
        dvn = dvn_s[...]
        dg_ref[...] += jnp.sum(dvn * xh, axis=0, keepdims=True)
        db_ref[...] += jnp.sum(dvn, axis=0, keepdims=True)
        dp_ref[:, D:2 * D] = (_layer_norm_bwd(dvn * g_ref[...], xh, rstd) * _gelu_grad(vv)).astype(dp_ref.dtype)

    vec = pl.BlockSpec((1, D), lambda i: (0, 0))
    return pl.pallas_call(
        body, name=name,
        out_shape=[jax.ShapeDtypeStruct((T, 3 * D), MXU), jax.ShapeDtypeStruct((1, D), F32), jax.ShapeDtypeStruct((1, D), F32),
                   jax.ShapeDtypeStruct((G, ch, ch), F32), jax.ShapeDtypeStruct((G, ch, LANE), F32)],
        grid=(T // ch,),
        in_specs=[pl.BlockSpec((ch, 3 * D), lambda i: (i, 0)), pl.BlockSpec((ch, D), lambda i: (i, 0)), vec, vec,
                  pl.BlockSpec((G, ch, ch), lambda i: (0, 0, 0)), pl.BlockSpec((ch, G), lambda i: (0, 0))],
        out_specs=[pl.BlockSpec((ch, 3 * D), lambda i: (i, 0)), vec, vec,
                   pl.BlockSpec((G, ch, ch), lambda i: (0, 0, 0)), pl.BlockSpec((G, ch, LANE), lambda i: (0, 0, 0))],
        scratch_shapes=[pltpu.VMEM((ch, D), F32)],
        compiler_params=_cp("arbitrary"),
    )(P, da, lg, lb, ws, bst)


def _adam_math(w, g, m, v):
    m = ADAM_B1 * m + (1.0 - ADAM_B1) * g
    v = ADAM_B2 * v + (1.0 - ADAM_B2) * (g * g)
    m_hat = m / (1.0 - ADAM_B1 ** ADAM_STEP)
    v_hat = v / (1.0 - ADAM_B2 ** ADAM_STEP)
    return -ADAM_LR * (m_hat / (jnp.sqrt(v_hat) + ADAM_EPS) + ADAM_WD * w), m, v


def _adam(w, m, v, gparts, name):
    R, C = w.shape
    n = gparts.shape[0]
    tr = _tile(R, max(SUBLANE, (1 << 18) // C), 16 if gparts.dtype != F32 else SUBLANE)

    def body(w_ref, m_ref, v_ref, gp_ref, g_ref, d_ref, mo_ref, vo_ref):
        g = gp_ref[0].astype(F32)
        for k in range(1, n):
            g = g + gp_ref[k].astype(F32)
        g_ref[...] = g
        d_ref[...], mo_ref[...], vo_ref[...] = _adam_math(w_ref[...], g, m_ref[...], v_ref[...])

    blk = pl.BlockSpec((tr, C), lambda i: (i, 0))
    return pl.pallas_call(
        body, name=name, out_shape=[jax.ShapeDtypeStruct((R, C), F32)] * 4, grid=(R // tr,),
        in_specs=[blk, blk, blk, pl.BlockSpec((n, tr, C), lambda i: (0, i, 0))], out_specs=[blk] * 4,
        compiler_params=_cp("parallel"),
    )(w, m, v, gparts)


def _sum_blocks(parts, name):
    n, R, C = parts.shape
    tr = _tile(R, 512, SUBLANE)

    def body(p_ref, o_ref):
        acc = p_ref[0]
        for k in range(1, n):
            acc = acc + p_ref[k]
        o_ref[...] = acc

    return pl.pallas_call(
        body, name=name, out_shape=jax.ShapeDtypeStruct((R, C), F32), grid=(R // tr,),
        in_specs=[pl.BlockSpec((n, tr, C), lambda i: (0, i, 0))], out_specs=pl.BlockSpec((tr, C), lambda i: (i, 0)),
        compiler_params=_cp("parallel"),
    )(parts)


def _pack(arrs):
    flat = jnp.concatenate([a.reshape(-1).astype(F32) for a in arrs])
    pad = -flat.shape[0] % (SUBLANE * LANE)
    return jnp.pad(flat, (0, pad)).reshape(-1, LANE)


def _unpack(buf, shapes):
    flat, out, off = buf.reshape(-1), [], 0
    for s in shapes:
        n = int(np.prod(s))
        out.append(flat[off:off + n].reshape(s))
        off += n
    return out


def _cols_full(g):
    return jnp.transpose(g, (1, 0, 2)).reshape(g.shape[1], N_DEV * g.shape[2])


def _cols_blocks(w):
    R, N = w.shape
    return jnp.transpose(w.reshape(R, N_DEV, N // N_DEV), (1, 0, 2)).astype(WIRE)


def _rows_blocks(w):
    return w.reshape(N_DEV, w.shape[0] // N_DEV, w.shape[1]).astype(WIRE)


def kernel(x, positions, norm_pre, norm_post, w_in_mla, mla_q_norm, w_uq, mla_kv_norm, w_ukv, w_out_mla, w_in_sc, sc_conv, w_out_sc, w_in_gm, gm_ln_g, gm_ln_b, gm_w_s, gm_b_s, w_out_gm, w_in_cf, cf_dw, cf_dw_b, cf_ln_g, cf_ln_b, w_out_cf, loss_target, m_norm_pre, m_norm_post, m_w_in_mla, m_mla_q_norm, m_w_uq, m_mla_kv_norm, m_w_ukv, m_w_out_mla, m_w_in_sc, m_sc_conv, m_w_out_sc, m_w_in_gm, m_gm_ln_g, m_gm_ln_b, m_gm_w_s, m_gm_b_s, m_w_out_gm, m_w_in_cf, m_cf_dw, m_cf_dw_b, m_cf_ln_g, m_cf_ln_b, m_w_out_cf, v_norm_pre, v_norm_post, v_w_in_mla, v_mla_q_norm, v_w_uq, v_mla_kv_norm, v_w_ukv, v_w_out_mla, v_w_in_sc, v_sc_conv, v_w_out_sc, v_w_in_gm, v_gm_ln_g, v_gm_ln_b, v_gm_w_s, v_gm_b_s, v_w_out_gm, v_w_in_cf, v_cf_dw, v_cf_dw_b, v_cf_ln_g, v_cf_ln_b, v_w_out_cf):
    names = ['norm_pre', 'norm_post', 'w_in_mla', 'mla_q_norm', 'w_uq', 'mla_kv_norm', 'w_ukv', 'w_out_mla', 'w_in_sc',
             'sc_conv', 'w_out_sc', 'w_in_gm', 'gm_ln_g', 'gm_ln_b', 'gm_w_s', 'gm_b_s', 'w_out_gm', 'w_in_cf', 'cf_dw',
             'cf_dw_b', 'cf_ln_g', 'cf_ln_b', 'w_out_cf']
    W = dict(zip(names, (norm_pre, norm_post, w_in_mla, mla_q_norm, w_uq, mla_kv_norm, w_ukv, w_out_mla, w_in_sc, sc_conv,
                         w_out_sc, w_in_gm, gm_ln_g, gm_ln_b, gm_w_s, gm_b_s, w_out_gm, w_in_cf, cf_dw, cf_dw_b, cf_ln_g,
                         cf_ln_b, w_out_cf)))
    Mo = dict(zip(names, (m_norm_pre, m_norm_post, m_w_in_mla, m_mla_q_norm, m_w_uq, m_mla_kv_norm, m_w_ukv, m_w_out_mla,
                          m_w_in_sc, m_sc_conv, m_w_out_sc, m_w_in_gm, m_gm_ln_g, m_gm_ln_b, m_gm_w_s, m_gm_b_s, m_w_out_gm,
                          m_w_in_cf, m_cf_dw, m_cf_dw_b, m_cf_ln_g, m_cf_ln_b, m_w_out_cf)))
    Vo = dict(zip(names, (v_norm_pre, v_norm_post, v_w_in_mla, v_mla_q_norm, v_w_uq, v_mla_kv_norm, v_w_ukv, v_w_out_mla,
                          v_w_in_sc, v_sc_conv, v_w_out_sc, v_w_in_gm, v_gm_ln_g, v_gm_ln_b, v_gm_w_s, v_gm_b_s, v_w_out_gm,
                          v_w_in_cf, v_cf_dw, v_cf_dw_b, v_cf_ln_g, v_cf_ln_b, v_w_out_cf)))
    big = ['w_in_mla', 'w_uq', 'w_ukv', 'w_out_mla', 'w_in_sc', 'w_out_sc', 'w_in_gm', 'w_out_gm', 'w_in_cf', 'w_out_cf']
    row_sharded = {'w_out_mla', 'w_out_sc', 'w_out_gm', 'w_out_cf'}
    chan = ['sc_conv', 'gm_ln_g', 'gm_ln_b', 'cf_dw', 'cf_dw_b', 'cf_ln_g', 'cf_ln_b']
    repl = ['norm_pre', 'norm_post', 'mla_q_norm', 'mla_kv_norm', 'gm_w_s', 'gm_b_s']

    T, D = x.shape[1], x.shape[2]
    xs, tgt = x[0], loss_target[0]
    pos = positions.reshape(T, 1)
    qr, kvr = mla_q_norm.shape[-1], mla_kv_norm.shape[-1]
    H = w_uq.shape[-1] * N_DEV // (NOPE + ROPE)
    HV = H * VHEAD
    c3 = qr + kvr + ROPE
    pa = -(-c3 // 512) * 512
    assert qr % kvr == 0 and qr % LANE == 0 and kvr % LANE == 0 and pa >= qr + kvr + LANE
    mx, my, mc = _mesh_pos()
    me = 4 * mx + 2 * my + mc
    core = mc.astype(jnp.int32).reshape(1)
    cs = D // N_DEV

    def wire(nm):
        return W[nm][0].astype(WIRE)

    hosts = _Hosts()
    G = {}

    def gather(nms, name=None):
        job = _GatherJob([wire(nm) for nm in nms])
        if name is None:
            hosts.add(job, lambda outs: G.update(zip(nms, outs)))
        else:
            G.update(zip(nms, _run_job(job, name)))

    def gathered_rows(nm):
        if nm not in G:
            hosts.flush("ag_late")
        return G[nm].reshape(G[nm].shape[0] * G[nm].shape[1], G[nm].shape[2])

    def gathered_cols(nm):
        if nm not in G:
            hosts.flush("ag_late")
        return _cols_full(G[nm])

    gather(['w_in_mla', 'w_uq', 'w_ukv'], "ag_mla")
    chan_rows = [W[nm][0].reshape(-1, cs) for nm in chan]
    chan_cnt = [r.shape[0] for r in chan_rows]
    chan_local = jnp.concatenate(chan_rows, axis=0)
    chan_pad = -chan_local.shape[0] % SUBLANE
    chan_full = _cols_full(_run_job(_GatherJob([jnp.pad(chan_local, ((0, chan_pad), (0, 0)))]), "ag_small")[0])
    offs = np.cumsum([0] + chan_cnt)
    CH = {nm: chan_full[offs[i]:offs[i + 1]] for i, nm in enumerate(chan)}

    w_in_full = gathered_cols('w_in_mla')
    w_cat = jnp.concatenate([w_in_full[:, :c3], jnp.zeros((D, pa - c3), WIRE), w_in_full[:, c3:]], axis=1)
    w_uq_pad = jnp.pad(gathered_cols('w_uq').reshape(qr, H, NOPE + ROPE), ((0, 0), (0, 0), (0, QPAD - NOPE - ROPE))).reshape(qr, H * QPAD)
    w_ukv3 = gathered_cols('w_ukv').reshape(kvr, H, NOPE + VHEAD)
    w_k_pad = jnp.pad(w_ukv3[:, :, :NOPE], ((0, 0), (0, 0), (0, QPAD - NOPE))).reshape(kvr, H * QPAD)
    w_v = w_ukv3[:, :, NOPE:].reshape(kvr, HV)
    in_names = ['w_in_mla', 'w_in_sc', 'w_in_gm', 'w_in_cf']
    out_names = ['w_out_mla', 'w_out_sc', 'w_out_gm', 'w_out_cf']
    WIN, WOUT = [w_cat], []

    half = ROPE // 2
    invf_np = np.zeros((1, LANE), np.float32)
    invf_np[0, :ROPE] = np.tile(np.float32(ROPE_THETA) ** (-np.arange(half, dtype=np.float32) / np.float32(half)), 2)
    invf = jnp.asarray(invf_np)
    gm_ws = gm_w_s[0]
    gm_bst = jnp.transpose(gm_b_s[0])

    xin, hs, Ps, acts, ys, keep = [xs], [], [], [], [], {}
    for i in range(4):
        h = _rms_fwd(xin[i], norm_pre[i:i + 1], f"pre{i}")
        if i == 0:
            gather([out_names[0]])
        else:
            WIN.append(gathered_cols(in_names[i]))
            if i < 3:
                gather([in_names[i + 1]])
        P = _mm(h, WIN[i], out_dtype=MXU, name=f"in{i}", hosts=hosts)
        if i == 0:
            cqn, ckvn = _mla_norms_fwd(P, mla_q_norm, mla_kv_norm, qr, kvr, "mla_norms")
            qpad = _mm(cqn, w_uq_pad, out_dtype=MXU, name="mla_q")
            kpad = _mm(ckvn, w_k_pad, out_dtype=MXU, name="mla_k")
            vv = _mm(ckvn, w_v, out_dtype=MXU, name="mla_v")
            qcat, kcat = _rope_fwd(qpad, kpad, P, pos, invf, (qr + kvr) // LANE, H, "rope")
            gather([in_names[1], out_names[1]])
            o, act, lse = _flash_fwd(qcat, kcat, vv, P, pa // VHEAD, H, "attn", hosts=hosts)
            keep.update(cqn=cqn, ckvn=ckvn, qcat=qcat, kcat=kcat, v=vv, o=o, lse=lse)
        elif i == 1:
            act = _sc_fwd(P, CH['sc_conv'], D, "sc_mix")
        elif i == 2:
            act = _gm_fwd(P, CH['gm_ln_g'], CH['gm_ln_b'], gm_ws, gm_bst, D, "gm_mix")
        else:
            yc = _cf_conv_fwd(P, CH['cf_dw'], CH['cf_dw_b'], D, "cf_conv")
            act = _cf_act_fwd(yc, P, CH['cf_ln_g'], CH['cf_ln_b'], D, "cf_act")
            keep.update(yc=yc)
        WOUT.append(gathered_rows(out_names[i]))
        if 1 <= i < 3:
            gather([out_names[i + 1]])
        y = _mm(act, WOUT[i], out_dtype=F32, name=f"out{i}", hosts=hosts)
        xin.append(_rms_fwd(y, norm_post[i:i + 1], f"post{i}", resid=xin[i]))
        hs.append(h), Ps.append(P), acts.append(act), ys.append(y)

    dx, loss_part = _loss_head(xin[4], tgt, "loss")
    loss = lax.psum(loss_part[0, 0], ("x", "y", "c"))

    hosts.flush("ag_late")
    dnorm_pre, dnorm_post, small_g, RS = [None] * 4, [None] * 4, {}, {}
    for i in (3, 2, 1, 0):
        dy, dnorm_post[i] = _rms_bwd(ys[i], dx, norm_post[i:i + 1], f"post_bwd{i}", out_dtype=MXU)
        da = _mm(dy, WOUT[i], tb=True, out_dtype=MXU, name=f"out_bwd{i}", hosts=hosts)
        dw_out = _mm(acts[i], dy, ta=True, out_dtype=WIRE, name=f"out_dw{i}", hosts=hosts).reshape(N_DEV, -1, D)
        _queue_reduce_scatter(hosts, {out_names[i]: dw_out}, core, f"rs_out{i}", RS)
        P = Ps[i]
        if i == 3:
            dyc, d_z, small_g['cf_ln_g'], small_g['cf_ln_b'] = _cf_act_bwd(
                keep['yc'], P, da, CH['cf_ln_g'], CH['cf_ln_b'], D, "cf_act_bwd")
            d_a, d_g, small_g['cf_dw'], small_g['cf_dw_b'] = _cf_conv_bwd(P, dyc, CH['cf_dw'], D, "cf_conv_bwd")
            dP = jnp.concatenate([d_a, d_g, d_z], axis=1)
        elif i == 2:
            dP, small_g['gm_ln_g'], small_g['gm_ln_b'], small_g['gm_w_s'], dbs = _gm_bwd(
                P, da, CH['gm_ln_g'], CH['gm_ln_b'], gm_ws, gm_bst, D, "gm_mix_bwd")
            small_g['gm_b_s'] = dbs[:, :, 0]
        elif i == 1:
            parts, small_g['sc_conv'] = _sc_bwd(P, da, CH['sc_conv'], D, "sc_mix_bwd")
            dP = jnp.concatenate(parts, axis=1)
        else:
            do, d_z, delta = _mla_gate_bwd(da, keep['o'], P, pa, H, "gate_bwd")
            dqcat, dkcat, dv = _flash_bwd(keep['qcat'], keep['kcat'], keep['v'], do, keep['lse'], delta, H, "attn_bwd",
                                          hosts=hosts)
            dqpad, dkr = _rope_bwd(dqcat, dkcat, pos, invf, H, "rope_bwd")
            dcqn = _mm(dqpad, w_uq_pad, tb=True, out_dtype=F32, name="mla_q_bwd", hosts=hosts)
            dckvn_k = _mm(dkcat, w_k_pad, tb=True, out_dtype=F32, name="mla_k_bwd", hosts=hosts)
            dckvn_v = _mm(dv, w_v, tb=True, out_dtype=F32, name="mla_v_bwd", hosts=hosts)
            dw_uq_pad = _mm(keep['cqn'], dqpad, ta=True, out_dtype=F32, name="mla_q_dw", hosts=hosts)
            dw_k_pad = _mm(keep['ckvn'], dkcat, ta=True, out_dtype=F32, name="mla_k_dw", hosts=hosts)
            dw_v = _mm(keep['ckvn'], dv, ta=True, out_dtype=F32, name="mla_v_dw", hosts=hosts)
            dw_uq = _cols_blocks(dw_uq_pad.reshape(qr, H, QPAD)[:, :, :NOPE + ROPE].reshape(qr, H * (NOPE + ROPE)))
            dw_ukv = _cols_blocks(jnp.concatenate(
                [dw_k_pad.reshape(kvr, H, QPAD)[:, :, :NOPE], dw_v.reshape(kvr, H, VHEAD)], axis=2).reshape(kvr, H * (NOPE + VHEAD)))
            _queue_reduce_scatter(hosts, {'w_uq': dw_uq, 'w_ukv': dw_ukv}, core, "rs_lat", RS)
            d_pa, small_g['mla_q_norm'], small_g['mla_kv_norm'] = _mla_norms_bwd(
                P, dcqn, dckvn_k, dckvn_v, dkr, mla_q_norm, mla_kv_norm, qr, kvr, pa, "mla_norms_bwd")
            dP = jnp.concatenate([d_pa, d_z], axis=1)
        dh = _mm(dP, WIN[i], tb=True, out_dtype=F32, name=f"in_bwd{i}", hosts=hosts)
        dx, dnorm_pre[i] = _rms_bwd(xin[i], dh, norm_pre[i:i + 1], f"pre_bwd{i}", resid=dx)
        if i == 0:
            dw_cat = _mm(hs[i], dP, ta=True, out_dtype=F32, name="in_dw0", hosts=hosts)
            dw_in = _cols_blocks(jnp.concatenate([dw_cat[:, :c3], dw_cat[:, pa:]], axis=1))
        else:
            dw_in = _mm(hs[i], dP, ta=True, out_dtype=WIRE, name=f"in_dw{i}", blocks=N_DEV, hosts=hosts)
        _queue_reduce_scatter(hosts, {in_names[i]: dw_in}, core, f"rs_in{i}", RS)
    grad_x = dx[None]

    small_g['norm_pre'] = jnp.concatenate(dnorm_pre, axis=0)
    small_g['norm_post'] = jnp.concatenate(dnorm_post, axis=0)
    small_names = repl + chan
    small_shapes = [(small_g[nm].shape) for nm in small_names]
    hosts.flush("rs_late")
    gathered = _run_job(_GatherJob([_pack([small_g[nm] for nm in small_names])]), "ag_grads")[0]
    totals = dict(zip(small_names, _unpack(_sum_blocks(gathered, "sum_grads"), small_shapes)))
    local_g = []
    for nm in small_names:
        g = totals[nm]
        if nm in chan:
            g = lax.dynamic_slice_in_dim(g.reshape(-1, D), me * cs, cs, axis=1)
        local_g.append(g.reshape(W[nm].shape))
    local_shapes = [W[nm].shape for nm in small_names]
    sm = _adam(_pack([W[nm] for nm in small_names]), _pack([Mo[nm] for nm in small_names]),
               _pack([Vo[nm] for nm in small_names]), _pack(local_g)[None], "adam_small")
    small_out = [dict(zip(small_names, _unpack(buf, local_shapes))) for buf in sm]

    big_out = {nm: [o[None] for o in _adam(W[nm][0], Mo[nm][0], Vo[nm][0], RS[nm], "adam_" + nm)] for nm in big}

    outs = [loss, grad_x]
    for k in range(4):
        outs += [big_out[nm][k] if nm in big_out else small_out[k][nm] for nm in names]
    return tuple(outs)
```

```python
import functools
import math

import numpy as np
import jax
import jax.numpy as jnp
from jax import lax
from jax.experimental import pallas as pl
from jax.experimental.pallas import tpu as pltpu

F32 = jnp.float32
MXU = jnp.bfloat16
WIRE = jnp.bfloat16
MESH = pl.DeviceIdType.MESH
N_DEV = 8

NORM_EPS = 1e-6
LN_EPS = 1e-5
ROPE_THETA = 10000.0
NOPE, ROPE, VHEAD = 128, 64, 128
QPAD = 256
ATT_SCALE = float((NOPE + ROPE) ** -0.5)
LOG2E = 1.4426950408889634
INV_SQRT2 = 0.7071067811865476
INV_SQRT_2PI = 0.3989422804014327
ADAM_LR, ADAM_B1, ADAM_B2, ADAM_EPS, ADAM_WD, ADAM_STEP = 0.001, 0.9, 0.999, 1e-08, 0.01, 10

LANE = 128
SUBLANE = 8
VMEM_LIMIT = 56 * 1024 * 1024
MM_TILE = 1024
MM_TK = 4096
ATT_TILE = 512
ROW_TILE = 128
CONV_CHUNK = 256
CONV_CB = 128
MXU_FLOP_PER_US = 8e8
HBM_BYTES_PER_US = 3.0e6
ATT_FWD_US, ATT_BWD_US = 1500.0, 2000.0
HOST_SLACK = 1.25

NT = (((1,), (1,)), ((), ()))
TN = (((0,), (0,)), ((), ()))
NN = (((1,), (0,)), ((), ()))


def _tile(dim, pref, align):
    t = min(pref, dim)
    t -= t % align
    while t >= align:
        if dim % t == 0:
            return t
        t -= align
    return dim


def _cp(*sem):
    return pltpu.CompilerParams(dimension_semantics=sem, vmem_limit_bytes=VMEM_LIMIT)


def _sig(v):
    return jax.nn.sigmoid(v)


def _silu(v):
    return v * _sig(v)


def _silu_grad(v):
    s = _sig(v)
    return s * (1.0 + v * (1.0 - s))


def _gelu(v):
    return 0.5 * v * (1.0 + lax.erf(v * INV_SQRT2))


def _gelu_grad(v):
    return 0.5 * (1.0 + lax.erf(v * INV_SQRT2)) + v * jnp.exp(-0.5 * v * v) * INV_SQRT_2PI


def _mesh_pos():
    return lax.axis_index("x"), lax.axis_index("y"), lax.axis_index("c")


def _other_chips(x, y):
    return [(1 - x, y), (x, 1 - y), (1 - x, 1 - y)]


class _GatherJob:
    US_PER_MB = 46.0

    def __init__(self, arrs):
        n = len(arrs)
        self.ins = list(arrs)
        self.out_shape = [jax.ShapeDtypeStruct((N_DEV,) + a.shape, a.dtype) for a in arrs]
        self.sems = [pltpu.SemaphoreType.DMA((n, 7)), pltpu.SemaphoreType.DMA((n, 7)), pltpu.SemaphoreType.DMA((n,))]
        self.cost = self.US_PER_MB * sum(a.size * a.dtype.itemsize for a in arrs) / 1e6

    def _parts(self, ins, outs, sems):
        send_sems, recv_sems, loc_sems = sems
        x, y, c = _mesh_pos()
        n = len(ins)

        def copy(a, k, block, to, src=None):
            dst = outs[a].at[4 * block[0] + 2 * block[1] + block[2]]
            return pltpu.make_async_remote_copy(
                src_ref=dst if src is None else src, dst_ref=dst,
                send_sem=send_sems.at[a, k], recv_sem=recv_sems.at[a, k],
                device_id=to, device_id_type=MESH)

        me, sib, chips = (x, y, c), (x, y, 1 - c), _other_chips(x, y)
        locs = [pltpu.make_async_copy(ins[a], outs[a].at[4 * x + 2 * y + c], loc_sems.at[a]) for a in range(n)]
        first = []
        for a in range(n):
            first.append(copy(a, 0, me, sib, src=ins[a]))
            for j, chip in enumerate(chips):
                first.append(copy(a, 1 + j, me, (*chip, c), src=ins[a]))
        passed = [copy(a, 4 + j, (*chip, c), sib) for j, chip in enumerate(chips) for a in range(n)]
        landed = [copy(a, 1 + j, (*chip, c), me) for j, chip in enumerate(chips) for a in range(n)]
        from_sib = [copy(a, 0, sib, me) for a in range(n)]
        from_sib += [copy(a, 4 + j, (*chip, 1 - c), me) for j, chip in enumerate(chips) for a in range(n)]
        return locs, first, landed, passed, from_sib

    def first(self, ins, outs, sems):
        locs, first, _, _, _ = self._parts(ins, outs, sems)
        for cp in locs + first:
            cp.start()

    def mid(self, ins, outs, sems):
        _, _, landed, passed, _ = self._parts(ins, outs, sems)
        for got, fwd in zip(landed, passed):
            got.wait_recv()
            fwd.start()

    def last(self, ins, outs, sems):
        locs, first, _, passed, from_sib = self._parts(ins, outs, sems)
        for cp in from_sib:
            cp.wait_recv()
        for cp in first + passed:
            cp.wait_send()
        for cp in locs:
            cp.wait()


class _SwapJob:
    US_PER_MB = 0.8

    def __init__(self, parts):
        n = len(parts)
        self.ins = list(parts)
        self.out_shape = [jax.ShapeDtypeStruct((4, 1) + p.shape[2:], p.dtype) for p in parts]
        self.sems = [pltpu.SemaphoreType.DMA((n,)), pltpu.SemaphoreType.DMA((n,))]
        self.cost = 5.0 + self.US_PER_MB * sum(p.size * p.dtype.itemsize for p in parts) / 1e6

    def _copies(self, ins, outs, sems):
        send_sems, recv_sems = sems
        x, y, c = _mesh_pos()
        return [pltpu.make_async_remote_copy(
            src_ref=ins[a].at[:, pl.ds(1 - c, 1)], dst_ref=outs[a],
            send_sem=send_sems.at[a], recv_sem=recv_sems.at[a],
            device_id=(x, y, 1 - c), device_id_type=MESH) for a in range(len(ins))]

    def first(self, ins, outs, sems):
        for cp in self._copies(ins, outs, sems):
            cp.start()

    def mid(self, ins, outs, sems):
        pass

    def last(self, ins, outs, sems):
        for cp in self._copies(ins, outs, sems):
            cp.wait()


class _ChipsJob:
    US_PER_MB = 11.0

    def __init__(self, qs):
        n = len(qs)
        self.ins = list(qs)
        self.out_shape = [jax.ShapeDtypeStruct(q.shape, q.dtype) for q in qs]
        self.sems = [pltpu.SemaphoreType.DMA((n, 3)), pltpu.SemaphoreType.DMA((n, 3)), pltpu.SemaphoreType.DMA((n,))]
        self.cost = 5.0 + self.US_PER_MB * sum(q.size * q.dtype.itemsize for q in qs) / 1e6

    def _copies(self, ins, outs, sems):
        send_sems, recv_sems, loc_sems = sems
        x, y, c = _mesh_pos()
        cps = []
        for a in range(len(ins)):
            cps.append(pltpu.make_async_copy(ins[a].at[2 * x + y], outs[a].at[3], loc_sems.at[a]))
            for k, chip in enumerate(_other_chips(x, y)):
                cps.append(pltpu.make_async_remote_copy(
                    src_ref=ins[a].at[2 * chip[0] + chip[1]], dst_ref=outs[a].at[k],
                    send_sem=send_sems.at[a, k], recv_sem=recv_sems.at[a, k],
                    device_id=(*chip, c), device_id_type=MESH))
        return cps

    def first(self, ins, outs, sems):
        for cp in self._copies(ins, outs, sems):
            cp.start()

    def mid(self, ins, outs, sems):
        pass

    def last(self, ins, outs, sems):
        for cp in self._copies(ins, outs, sems):
            cp.wait()


def _call(body, *, name, out_shape, in_specs, out_specs, args, grid=(), scratch_shapes=(), semantics=(), job=None):
    if job is None:
        outs = pl.pallas_call(
            body, name=name, out_shape=out_shape, grid=grid, in_specs=in_specs, out_specs=out_specs,
            scratch_shapes=list(scratch_shapes), compiler_params=_cp(*semantics))(*args)
        return outs, None
    single = not isinstance(out_shape, (list, tuple))
    host_out = [out_shape] if single else list(out_shape)
    host_ospecs = [out_specs] if single else list(out_specs)
    n_in, n_out, n_scr = len(in_specs), len(host_out), len(scratch_shapes)
    j_in, j_out = len(job.ins), len(job.out_shape)
    total = int(np.prod(grid)) if grid else 1
    mid_step = max(total * 7 // 8, 1)

    def full(*refs):
        h_in, jin = refs[:n_in], refs[n_in:n_in + j_in]
        o = n_in + j_in
        h_out, jout = refs[o:o + n_out], refs[o + n_out:o + n_out + j_out]
        o += n_out + j_out
        h_scr, jsem = refs[o:o + n_scr], refs[o + n_scr:]
        if total == 1:
            job.first(jin, jout, jsem)
            body(*h_in, *h_out, *h_scr)
            job.mid(jin, jout, jsem)
            job.last(jin, jout, jsem)
            return
        step = pl.program_id(0)
        for d in range(1, len(grid)):
            step = step * grid[d] + pl.program_id(d)
        pl.when(step == 0)(lambda: job.first(jin, jout, jsem))
        body(*h_in, *h_out, *h_scr)
        pl.when(step == mid_step)(lambda: job.mid(jin, jout, jsem))
        pl.when(step == total - 1)(lambda: job.last(jin, jout, jsem))

    hbm = pl.BlockSpec(memory_space=pltpu.HBM)
    outs = pl.pallas_call(
        full, name=name, out_shape=host_out + list(job.out_shape), grid=grid,
        in_specs=list(in_specs) + [hbm] * j_in, out_specs=host_ospecs + [hbm] * j_out,
        scratch_shapes=list(scratch_shapes) + list(job.sems),
        compiler_params=_cp(*(("arbitrary",) * len(grid))))(*args, *job.ins)
    host = outs[0] if single else list(outs[:n_out])
    return host, list(outs[n_out:])


def _run_job(job, name):
    def body():
        pass

    return _call(body, name=name, out_shape=[], in_specs=[], out_specs=[], args=[], job=job)[1]


class _Hosts:
    def __init__(self):
        self.pending = []
        self.flushed = 0

    def add(self, job, done):
        self.pending.append((job, done))

    def take(self, duration, force=False):
        fits = [e for e in self.pending if e[0].cost <= HOST_SLACK * duration]
        if not fits and not (force and self.pending):
            return None
        best = max(fits, key=lambda e: e[0].cost) if fits else min(self.pending, key=lambda e: e[0].cost)
        self.pending.remove(best)
        return best

    def flush(self, name):
        while self.pending:
            job, done = self.pending.pop(0)
            done(_run_job(job, f"{name}{self.flushed}"))
            self.flushed += 1


def _pair_add(part4, recv, core, name):
    _, _, R, C = part4.shape
    tr = _tile(R, max(SUBLANE, (1 << 19) // C), 16)

    def body(core_ref, p_ref, r_ref, o_ref):
        o_ref[...] = (p_ref[...].astype(F32) + r_ref[...].astype(F32)).astype(o_ref.dtype)

    return pl.pallas_call(
        body, name=name,
        out_shape=jax.ShapeDtypeStruct((4, R, C), WIRE),
        grid_spec=pltpu.PrefetchScalarGridSpec(
            num_scalar_prefetch=1, grid=(4, R // tr),
            in_specs=[pl.BlockSpec((None, None, tr, C), lambda s, i, cr: (s, cr[0], i, 0)),
                      pl.BlockSpec((None, None, tr, C), lambda s, i, cr: (s, 0, i, 0))],
            out_specs=pl.BlockSpec((None, tr, C), lambda s, i, cr: (s, i, 0))),
        compiler_params=_cp("parallel", "parallel"),
    )(core, part4, recv)


def _queue_reduce_scatter(hosts, parts, core, name, sink):
    keys = list(parts)
    p4 = [parts[k].reshape((4, 2) + parts[k].shape[1:]) for k in keys]

    def swapped(recv):
        qs = [_pair_add(p, r, core, f"{name}_add{i}") for i, (p, r) in enumerate(zip(p4, recv))]
        hosts.add(_ChipsJob(qs), lambda outs: sink.update(zip(keys, outs)))

    hosts.add(_SwapJob(p4), swapped)


def _hosted(hosts, duration, body, force=False, **kw):
    entry = hosts.take(duration, force) if hosts is not None else None
    out, jouts = _call(body, job=entry[0] if entry else None, **kw)
    if entry:
        entry[1](jouts)
    return out


def _mm(a, b, *, ta=False, tb=False, out_dtype, name, blocks=None, hosts=None):
    M, K = (a.shape[1], a.shape[0]) if ta else a.shape
    N = b.shape[0] if tb else b.shape[1]
    assert K == (b.shape[1] if tb else b.shape[0]), (a.shape, b.shape, ta, tb)
    ns = N // blocks if blocks else N
    tm, tk = _tile(M, MM_TILE, LANE), _tile(K, MM_TK, LANE)
    tn = _tile(ns, MM_TILE, LANE)
    nk = K // tk
    per = ns // tn

    dims = (((0,) if ta else (1,), (1,) if tb else (0,)), ((), ()))

    def body(a_ref, b_ref, o_ref, *acc):
        part = lax.dot_general(a_ref[...], b_ref[...], dims, preferred_element_type=F32)
        if nk == 1:
            o_ref[...] = part.astype(o_ref.dtype)
            return
        acc_ref, k = acc[0], pl.program_id(2)

        @pl.when(k == 0)
        def _():
            acc_ref[...] = part

        @pl.when(k > 0)
        def _():
            acc_ref[...] += part

        @pl.when(k == nk - 1)
        def _():
            o_ref[...] = acc_ref[...].astype(o_ref.dtype)

    a_spec = pl.BlockSpec((tk, tm), lambda i, j, k: (k, i)) if ta else pl.BlockSpec((tm, tk), lambda i, j, k: (i, k))
    b_spec = pl.BlockSpec((tn, tk), lambda i, j, k: (j, k)) if tb else pl.BlockSpec((tk, tn), lambda i, j, k: (k, j))
    if blocks:
        out_shape = jax.ShapeDtypeStruct((blocks, M, ns), out_dtype)
        o_spec = pl.BlockSpec((None, tm, tn), lambda i, j, k: (j // per, i, j % per))
    else:
        out_shape = jax.ShapeDtypeStruct((M, N), out_dtype)
        o_spec = pl.BlockSpec((tm, tn), lambda i, j, k: (i, j))
    return _hosted(
        hosts, 2.0 * M * N * K / MXU_FLOP_PER_US, body, name=name, out_shape=out_shape, grid=(M // tm, N // tn, nk),
        in_specs=[a_spec, b_spec], out_specs=o_spec, args=[a, b],
        scratch_shapes=[pltpu.VMEM((tm, tn), F32)] if nk > 1 else [], semantics=("parallel", "parallel", "arbitrary"))


def _rms_fwd(xin, g, name):
    T, D = xin.shape
    tr = _tile(T, ROW_TILE, SUBLANE)

    def body(x_ref, g_ref, o_ref):
        xv = x_ref[...]
        r = lax.rsqrt(jnp.mean(xv * xv, axis=-1, keepdims=True) + NORM_EPS)
        o_ref[...] = (xv * r * g_ref[...]).astype(o_ref.dtype)

    row = pl.BlockSpec((tr, D), lambda i: (i, 0))
    return pl.pallas_call(
        body, name=name, out_shape=jax.ShapeDtypeStruct((T, D), MXU),
        grid=(T // tr,), in_specs=[row, pl.BlockSpec((1, D), lambda i: (0, 0))],
        out_specs=row, compiler_params=_cp("parallel"),
    )(xin, g)


def _rms_bwd(xin, dout, g, name, resid=None, out_dtype=F32):
    T, D = xin.shape
    tr = _tile(T, ROW_TILE, SUBLANE)

    def body(*refs):
        x_ref, d_ref, g_ref = refs[:3]
        dx_ref, dg_ref = refs[-2:]
        xv = x_ref[...].astype(F32)
        dv = d_ref[...].astype(F32)
        r = lax.rsqrt(jnp.mean(xv * xv, axis=-1, keepdims=True) + NORM_EPS)
        xh = xv * r
        dh = dv * g_ref[...]
        dxv = r * (dh - xh * jnp.mean(dh * xh, axis=-1, keepdims=True))
        if resid is not None:
            dxv = refs[3][...] + dxv
        dx_ref[...] = dxv.astype(dx_ref.dtype)

        @pl.when(pl.program_id(0) == 0)
        def _():
            dg_ref[...] = jnp.zeros_like(dg_ref)

        dg_ref[...] += jnp.sum(dv * xh, axis=0, keepdims=True)

    row = pl.BlockSpec((tr, D), lambda i: (i, 0))
    vec = pl.BlockSpec((1, D), lambda i: (0, 0))
    ins = [xin, dout, g] + ([resid] if resid is not None else [])
    return pl.pallas_call(
        body, name=name,
        out_shape=[jax.ShapeDtypeStruct((T, D), out_dtype), jax.ShapeDtypeStruct((1, D), F32)],
        grid=(T // tr,), in_specs=[row, row, vec] + ([row] if resid is not None else []),
        out_specs=[row, vec], compiler_params=_cp("arbitrary"),
    )(*ins)


def _post_fwd(y, g, resid, name, next_gain=None, target=None):
    T, D = y.shape
    tr = _tile(T, ROW_TILE, SUBLANE)

    def body(y_ref, g_ref, r_ref, e_ref, o1_ref, o2_ref):
        yv = y_ref[...]
        r = lax.rsqrt(jnp.mean(yv * yv, axis=-1, keepdims=True) + NORM_EPS)
        xv = r_ref[...] + yv * r * g_ref[...]
        if target is None:
            o1_ref[...] = xv
            r2 = lax.rsqrt(jnp.mean(xv * xv, axis=-1, keepdims=True) + NORM_EPS)
            o2_ref[...] = (xv * r2 * e_ref[...]).astype(o2_ref.dtype)
        else:
            e = xv - e_ref[...]
            o1_ref[...] = e * (1.0 / D)

            @pl.when(pl.program_id(0) == 0)
            def _():
                o2_ref[...] = jnp.zeros_like(o2_ref)

            rows = jnp.sum(e * e, axis=-1, keepdims=True) * (1.0 / D)
            o2_ref[...] += jnp.broadcast_to(0.5 * jnp.sum(rows, axis=0, keepdims=True), o2_ref.shape)

    row = pl.BlockSpec((tr, D), lambda i: (i, 0))
    vec = pl.BlockSpec((1, D), lambda i: (0, 0))
    if target is None:
        extra, e_spec = next_gain, vec
        out2, o2_spec = jax.ShapeDtypeStruct((T, D), MXU), row
    else:
        extra, e_spec = target, row
        out2, o2_spec = jax.ShapeDtypeStruct((1, LANE), F32), pl.BlockSpec((1, LANE), lambda i: (0, 0))
    return pl.pallas_call(
        body, name=name, out_shape=[jax.ShapeDtypeStruct((T, D), F32), out2],
        grid=(T // tr,), in_specs=[row, vec, row, e_spec], out_specs=[row, o2_spec],
        compiler_params=_cp("arbitrary" if target is not None else "parallel"),
    )(y, g, resid, extra)


def _mla_norms_fwd(P, gq, gkv, qr, kvr, name):
    T = P.shape[0]
    tr = _tile(T, ROW_TILE, SUBLANE)

    def body(cq_ref, ckv_ref, gq_ref, gkv_ref, oq_ref, okv_ref):
        for x_ref, g_ref, o_ref in ((cq_ref, gq_ref, oq_ref), (ckv_ref, gkv_ref, okv_ref)):
            xv = x_ref[...].astype(F32)
            r = lax.rsqrt(jnp.mean(xv * xv, axis=-1, keepdims=True) + NORM_EPS)
            o_ref[...] = (xv * r * g_ref[...]).astype(o_ref.dtype)

    return pl.pallas_call(
        body, name=name,
        out_shape=[jax.ShapeDtypeStruct((T, qr), MXU), jax.ShapeDtypeStruct((T, kvr), MXU)],
        grid=(T // tr,),
        in_specs=[pl.BlockSpec((tr, qr), lambda i: (i, 0)), pl.BlockSpec((tr, kvr), lambda i: (i, qr // kvr)),
                  pl.BlockSpec((1, qr), lambda i: (0, 0)), pl.BlockSpec((1, kvr), lambda i: (0, 0))],
        out_specs=[pl.BlockSpec((tr, qr), lambda i: (i, 0)), pl.BlockSpec((tr, kvr), lambda i: (i, 0))],
        compiler_params=_cp("parallel"),
    )(P, P, gq, gkv)


def _mla_norms_bwd(P, dcqn, dckvn_k, dckvn_v, dkr, gq, gkv, qr, kvr, pa, name):
    T = P.shape[0]
    tr = _tile(T, ROW_TILE, SUBLANE)
    c2 = qr + kvr

    def body(cq_ref, ckv_ref, dq_ref, dk_ref, dv_ref, dkr_ref, gq_ref, gkv_ref, dp_ref, dgq_ref, dgkv_ref):
        @pl.when(pl.program_id(0) == 0)
        def _():
            dgq_ref[...] = jnp.zeros_like(dgq_ref)
            dgkv_ref[...] = jnp.zeros_like(dgkv_ref)

        def one(x_ref, dv, g_ref, dg_ref):
            xv = x_ref[...].astype(F32)
            r = lax.rsqrt(jnp.mean(xv * xv, axis=-1, keepdims=True) + NORM_EPS)
            xh = xv * r
            dh = dv * g_ref[...]
            dg_ref[...] += jnp.sum(dv * xh, axis=0, keepdims=True)
            return r * (dh - xh * jnp.mean(dh * xh, axis=-1, keepdims=True))

        dp_ref[:, 0:qr] = one(cq_ref, dq_ref[...], gq_ref, dgq_ref).astype(dp_ref.dtype)
        dp_ref[:, qr:c2] = one(ckv_ref, dk_ref[...] + dv_ref[...], gkv_ref, dgkv_ref).astype(dp_ref.dtype)
        dp_ref[:, c2:c2 + LANE] = dkr_ref[...].astype(dp_ref.dtype)
        if pa > c2 + LANE:
            dp_ref[:, c2 + LANE:pa] = jnp.zeros((tr, pa - c2 - LANE), dp_ref.dtype)

    return pl.pallas_call(
        body, name=name,
        out_shape=[jax.ShapeDtypeStruct((T, pa), MXU), jax.ShapeDtypeStruct((1, qr), F32),
                   jax.ShapeDtypeStruct((1, kvr), F32)],
        grid=(T // tr,),
        in_specs=[pl.BlockSpec((tr, qr), lambda i: (i, 0)), pl.BlockSpec((tr, kvr), lambda i: (i, qr // kvr)),
                  pl.BlockSpec((tr, qr), lambda i: (i, 0)), pl.BlockSpec((tr, kvr), lambda i: (i, 0)),
                  pl.BlockSpec((tr, kvr), lambda i: (i, 0)), pl.BlockSpec((tr, LANE), lambda i: (i, 0)),
                  pl.BlockSpec((1, qr), lambda i: (0, 0)), pl.BlockSpec((1, kvr), lambda i: (0, 0))],
        out_specs=[pl.BlockSpec((tr, pa), lambda i: (i, 0)), pl.BlockSpec((1, qr), lambda i: (0, 0)),
                   pl.BlockSpec((1, kvr), lambda i: (0, 0))],
        compiler_params=_cp("arbitrary"),
    )(P, P, dcqn, dckvn_k, dckvn_v, dkr, gq, gkv)


def _rope_tables(pos_ref, invf_ref):
    ang = pos_ref[...].astype(F32) * invf_ref[...]
    lane = lax.broadcasted_iota(jnp.int32, ang.shape, 1)
    cos, sin = jnp.cos(ang), jnp.sin(ang)
    half = ROPE // 2
    c = jnp.where(lane < ROPE, cos, 0.0)
    s1 = jnp.where(lane < half, -sin, 0.0)
    s2 = jnp.where((lane >= half) & (lane < ROPE), sin, 0.0)
    return c, s1, s2


def _rope_fwd(qpad, kpad, P, pos, invf, kr_blk, H, name):
    T = qpad.shape[0]
    tr = _tile(T, ROW_TILE, SUBLANE)

    def body(q_ref, k_ref, kr_ref, pos_ref, invf_ref, qo_ref, ko_ref):
        c, s1, s2 = _rope_tables(pos_ref, invf_ref)

        def rot(v):
            return v * c + pltpu.roll(v, LANE - ROPE // 2, 1) * s1 + pltpu.roll(v, ROPE // 2, 1) * s2

        kr = rot(kr_ref[...].astype(F32)).astype(ko_ref.dtype)
        for h in range(H):
            lo = h * QPAD
            qo_ref[:, lo:lo + NOPE] = q_ref[:, lo:lo + NOPE]
            qo_ref[:, lo + NOPE:lo + QPAD] = rot(q_ref[:, lo + NOPE:lo + QPAD].astype(F32)).astype(qo_ref.dtype)
            ko_ref[:, lo:lo + NOPE] = k_ref[:, lo:lo + NOPE]
            ko_ref[:, lo + NOPE:lo + QPAD] = kr

    wide = pl.BlockSpec((tr, H * QPAD), lambda i: (i, 0))
    return pl.pallas_call(
        body, name=name,
        out_shape=[jax.ShapeDtypeStruct(qpad.shape, MXU), jax.ShapeDtypeStruct(kpad.shape, MXU)],
        grid=(T // tr,),
        in_specs=[wide, wide, pl.BlockSpec((tr, LANE), lambda i: (i, kr_blk)),
                  pl.BlockSpec((tr, 1), lambda i: (i, 0)), pl.BlockSpec((1, LANE), lambda i: (0, 0))],
        out_specs=[wide, wide], compiler_params=_cp("parallel"),
    )(qpad, kpad, P, pos, invf)


def _rope_bwd(dqcat, dkcat, pos, invf, H, name):
    T = dqcat.shape[0]
    tr = _tile(T, ROW_TILE, SUBLANE)

    def body(dq_ref, dk_ref, pos_ref, invf_ref, dqo_ref, dkr_ref):
        c, s1, s2 = _rope_tables(pos_ref, invf_ref)

        def rot_t(v):
            return v * c + pltpu.roll(v * s1, ROPE // 2, 1) + pltpu.roll(v * s2, LANE - ROPE // 2, 1)

        acc = jnp.zeros((tr, LANE), F32)
        for h in range(H):
            lo = h * QPAD
            dqo_ref[:, lo:lo + NOPE] = dq_ref[:, lo:lo + NOPE]
            dqo_ref[:, lo + NOPE:lo + QPAD] = rot_t(dq_ref[:, lo + NOPE:lo + QPAD].astype(F32)).astype(dqo_ref.dtype)
            acc = acc + dk_ref[:, lo + NOPE:lo + QPAD].astype(F32)
        dkr_ref[...] = rot_t(acc)

    wide = pl.BlockSpec((tr, H * QPAD), lambda i: (i, 0))
    return pl.pallas_call(
        body, name=name,
        out_shape=[jax.ShapeDtypeStruct(dqcat.shape, MXU), jax.ShapeDtypeStruct((T, LANE), F32)],
        grid=(T // tr,),
        in_specs=[wide, wide, pl.BlockSpec((tr, 1), lambda i: (i, 0)), pl.BlockSpec((1, LANE), lambda i: (0, 0))],
        out_specs=[wide, pl.BlockSpec((tr, LANE), lambda i: (i, 0))], compiler_params=_cp("parallel"),
    )(dqcat, dkcat, pos, invf)


def _causal_mask(s):
    row = lax.broadcasted_iota(jnp.int32, s.shape, 0)
    col = lax.broadcasted_iota(jnp.int32, s.shape, 1)
    return jnp.where(col <= row, s, -1e30)


def _flash_fwd(qcat, kcat, v, P, z_blk, H, name, hosts=None):
    T = qcat.shape[0]
    tq = _tile(T, ATT_TILE, LANE)
    hp = 2 if H % 2 == 0 and z_blk % 2 == 0 else 1
    c2 = ATT_SCALE * LOG2E

    def body(q_ref, k_ref, v_ref, z_ref, o_ref, a_ref, lse_ref, m_s, l_s, acc_s):
        i = pl.program_id(1)
        m_s[...] = jnp.full_like(m_s, -1e30)
        l_s[...] = jnp.zeros_like(l_s)
        acc_s[...] = jnp.zeros_like(acc_s)

        def step(j, masked):
            r0 = pl.multiple_of(j * tq, tq)
            ones = jnp.ones((tq, LANE), MXU)
            for h in range(hp):
                q = q_ref[:, h * QPAD:(h + 1) * QPAD]
                kb = k_ref[pl.ds(r0, tq), h * QPAD:(h + 1) * QPAD]
                vb = v_ref[pl.ds(r0, tq), h * VHEAD:(h + 1) * VHEAD]
                s = lax.dot_general(q, kb, NT, preferred_element_type=F32)
                if masked:
                    s = _causal_mask(s)
                m_prev = m_s[h]
                m_new = jnp.maximum(m_prev, jnp.max(s, axis=1, keepdims=True))
                p = jnp.exp2((s - m_new) * c2).astype(MXU)
                alpha = jnp.exp2((m_prev - m_new) * c2)
                l_s[h] = alpha * l_s[h] + lax.dot_general(p, ones, NN, preferred_element_type=F32)
                acc_s[h] = alpha * acc_s[h] + lax.dot_general(p, vb, NN, preferred_element_type=F32)
                m_s[h] = m_new

        def loop(j, carry):
            step(j, False)
            return carry

        lax.fori_loop(0, i, loop, 0)
        step(i, True)
        for h in range(hp):
            l = l_s[h]
            o = acc_s[h] / l
            cols = slice(h * VHEAD, (h + 1) * VHEAD)
            o_ref[:, cols] = o.astype(o_ref.dtype)
            a_ref[:, cols] = (o * _silu(z_ref[:, cols].astype(F32))).astype(a_ref.dtype)
            lse_ref[h] = m_s[h] * ATT_SCALE + jnp.log(l)

    return _hosted(
        hosts, ATT_FWD_US * (T / 4096.0) ** 2 * (H / 32.0), body, name=name,
        out_shape=[jax.ShapeDtypeStruct((T, H * VHEAD), MXU), jax.ShapeDtypeStruct((T, H * VHEAD), MXU),
                   jax.ShapeDtypeStruct((H, T, LANE), F32)],
        grid=(H // hp, T // tq),
        in_specs=[pl.BlockSpec((tq, hp * QPAD), lambda h, i: (i, h)), pl.BlockSpec((T, hp * QPAD), lambda h, i: (0, h)),
                  pl.BlockSpec((T, hp * VHEAD), lambda h, i: (0, h)),
                  pl.BlockSpec((tq, hp * VHEAD), lambda h, i: (i, z_blk // hp + h))],
        out_specs=[pl.BlockSpec((tq, hp * VHEAD), lambda h, i: (i, h)), pl.BlockSpec((tq, hp * VHEAD), lambda h, i: (i, h)),
                   pl.BlockSpec((hp, tq, LANE), lambda h, i: (h, i, 0))],
        scratch_shapes=[pltpu.VMEM((hp, tq, 1), F32), pltpu.VMEM((hp, tq, LANE), F32), pltpu.VMEM((hp, tq, VHEAD), F32)],
        args=[qcat, kcat, v, P], semantics=("parallel", "arbitrary"))


def _flash_bwd(qcat, kcat, v, do, lse, delta, H, name, hosts=None):
    T = qcat.shape[0]
    tq = _tile(T, ATT_TILE, LANE)
    nq = T // tq
    c2 = ATT_SCALE * LOG2E

    def body(q_ref, do_ref, lse_ref, dl_ref, k_ref, v_ref, dq_ref, dk_ref, dv_ref, dq_s, dk_s, dv_s):
        j = pl.program_id(1)
        kb, vb = k_ref[...], v_ref[...]
        dk_s[...] = jnp.zeros_like(dk_s)
        dv_s[...] = jnp.zeros_like(dv_s)

        @pl.when(j == 0)
        def _():
            dq_s[...] = jnp.zeros_like(dq_s)

        def step(i, masked):
            rows = pl.ds(pl.multiple_of(i * tq, tq), tq)
            qb = q_ref[rows, :]
            dob = do_ref[rows, :]
            s = lax.dot_general(qb, kb, NT, preferred_element_type=F32)
            if masked:
                s = _causal_mask(s)
            p = jnp.exp2(s * c2 - lse_ref[rows, 0:1] * LOG2E)
            dv_s[...] += lax.dot_general(p.astype(dob.dtype), dob, TN, preferred_element_type=F32)
            dp = lax.dot_general(dob, vb, NT, preferred_element_type=F32)
            ds = (p * (dp - dl_ref[rows, 0:1]) * ATT_SCALE).astype(qb.dtype)
            dk_s[...] += lax.dot_general(ds, qb, TN, preferred_element_type=F32)
            dq_s[rows, :] += lax.dot_general(ds, kb, NN, preferred_element_type=F32)

        def loop(i, carry):
            step(i, False)
            return carry

        step(j, True)
        lax.fori_loop(j + 1, nq, loop, 0)
        dk_ref[...] = dk_s[...].astype(dk_ref.dtype)
        dv_ref[...] = dv_s[...].astype(dv_ref.dtype)

        @pl.when(j == nq - 1)
        def _():
            dq_ref[...] = dq_s[...].astype(dq_ref.dtype)

    return _hosted(
        hosts, ATT_BWD_US * (T / 4096.0) ** 2 * (H / 32.0), body, name=name,
        out_shape=[jax.ShapeDtypeStruct(qcat.shape, MXU), jax.ShapeDtypeStruct(kcat.shape, MXU),
                   jax.ShapeDtypeStruct(v.shape, MXU)],
        grid=(H, nq),
        in_specs=[pl.BlockSpec((T, QPAD), lambda h, j: (0, h)), pl.BlockSpec((T, VHEAD), lambda h, j: (0, h)),
                  pl.BlockSpec((None, T, LANE), lambda h, j: (h, 0, 0)), pl.BlockSpec((None, T, LANE), lambda h, j: (h, 0, 0)),
                  pl.BlockSpec((tq, QPAD), lambda h, j: (j, h)), pl.BlockSpec((tq, VHEAD), lambda h, j: (j, h))],
        out_specs=[pl.BlockSpec((T, QPAD), lambda h, j: (0, h)), pl.BlockSpec((tq, QPAD), lambda h, j: (j, h)),
                   pl.BlockSpec((tq, VHEAD), lambda h, j: (j, h))],
        scratch_shapes=[pltpu.VMEM((T, QPAD), F32), pltpu.VMEM((tq, QPAD), F32), pltpu.VMEM((tq, VHEAD), F32)],
        args=[qcat, do, lse, delta, kcat, v], semantics=("parallel", "arbitrary"))


def _mla_gate_bwd(da, o, P, pa, H, name):
    T, HV = da.shape
    tr = _tile(T, ROW_TILE, SUBLANE)
    cw = _tile(math.gcd(pa, HV), 512, LANE)
    hb = cw // VHEAD

    def body(da_ref, o_ref, z_ref, do_ref, dz_ref, dl_ref):
        dav, ov, zv = da_ref[...].astype(F32), o_ref[...].astype(F32), z_ref[...].astype(F32)
        dov = dav * _silu(zv)
        do_ref[...] = dov.astype(do_ref.dtype)
        dz_ref[...] = (dav * ov * _silu_grad(zv)).astype(dz_ref.dtype)
        prod = dov * ov
        for h in range(hb):
            dl_ref[h] = jnp.broadcast_to(jnp.sum(prod[:, h * VHEAD:(h + 1) * VHEAD], axis=1, keepdims=True), (tr, LANE))

    blk = pl.BlockSpec((tr, cw), lambda i, j: (i, j))
    shifted = pl.BlockSpec((tr, cw), lambda i, j: (i, pa // cw + j))
    return pl.pallas_call(
        body, name=name,
        out_shape=[jax.ShapeDtypeStruct((T, HV), MXU), jax.ShapeDtypeStruct((T, HV), MXU),
                   jax.ShapeDtypeStruct((H, T, LANE), F32)],
        grid=(T // tr, HV // cw),
        in_specs=[blk, blk, shifted],
        out_specs=[blk, blk, pl.BlockSpec((hb, tr, LANE), lambda i, j: (j, i, 0))],
        compiler_params=_cp("parallel", "parallel"),
    )(da, o, P)


def _bank_rows(width):
    return min(SUBLANE, width), -(-(width - 1) // SUBLANE) * SUBLANE


def _bank_fill(bank_ref, val, width, T, causal):
    nr, hp = _bank_rows(width)
    rows = lax.broadcasted_iota(jnp.int32, val.shape, 0)
    zero = jnp.zeros((hp,) + val.shape[1:], F32)
    for r in range(nr):
        if causal:
            bank_ref[r, 0:hp, :] = zero
            bank_ref[r, hp:hp + T, :] = val if r == 0 else jnp.where(rows >= r, pltpu.roll(val, r, 0), 0.0)
        else:
            bank_ref[r, T:T + hp, :] = zero
            bank_ref[r, 0:T, :] = val if r == 0 else jnp.where(rows < T - r, pltpu.roll(val, T - r, 0), 0.0)


def _bank_tap(bank_ref, s, t0, tc, width, causal):
    _, hp = _bank_rows(width)
    q, r = divmod(s, SUBLANE)
    off = hp - SUBLANE * q if causal else SUBLANE * q
    return bank_ref[r, pl.ds(pl.multiple_of(t0 + off, SUBLANE), tc), :]


def _conv_chunk(bank_ref, w_ref, t0, tc, width):
    acc = None
    for s in range(width):
        k = width - 1 - s
        term = w_ref[k:k + 1, :] * _bank_tap(bank_ref, s, t0, tc, width, True)
        acc = term if acc is None else acc + term
    return acc


def _conv_bwd_chunk(bank_ref, w_ref, xin, dw_ref, t0, tc, width):
    acc = None
    for s in range(width):
        k = width - 1 - s
        tap = _bank_tap(bank_ref, s, t0, tc, width, False)
        term = w_ref[k:k + 1, :] * tap
        acc = term if acc is None else acc + term
        dw_ref[k] += jnp.sum((tap * xin).reshape(tc // SUBLANE, SUBLANE, xin.shape[1]), axis=0)
    return acc


def _full_t(T, cb, col):
    return pl.BlockSpec((T, cb), lambda j, col=col: (0, col + j))


def _sc_fwd(P, w, D, name):
    T = P.shape[0]
    cb = _tile(D, CONV_CB, LANE)
    nb = D // cb
    width = w.shape[0]
    tc = _tile(T, CONV_CHUNK, SUBLANE)
    nr, hp = _bank_rows(width)

    def body(b_ref, c_ref, u_ref, z_ref, w_ref, a_ref, bank):
        _bank_fill(bank, c_ref[...].astype(F32) * u_ref[...].astype(F32), width, T, True)

        def chunk(ci, carry):
            t0 = pl.multiple_of(ci * tc, tc)
            rows = pl.ds(t0, tc)
            v = _conv_chunk(bank, w_ref, t0, tc, width)
            a_ref[rows, :] = (b_ref[rows, :].astype(F32) * v * _silu(z_ref[rows, :].astype(F32))).astype(a_ref.dtype)
            return carry

        lax.fori_loop(0, T // tc, chunk, 0)

    return pl.pallas_call(
        body, name=name, out_shape=jax.ShapeDtypeStruct((T, D), MXU), grid=(nb,),
        in_specs=[_full_t(T, cb, 0), _full_t(T, cb, nb), _full_t(T, cb, 2 * nb), _full_t(T, cb, 3 * nb),
                  pl.BlockSpec((width, cb), lambda j: (0, j))],
        out_specs=_full_t(T, cb, 0),
        scratch_shapes=[pltpu.VMEM((nr, T + hp, cb), F32)],
        compiler_params=_cp("parallel"),
    )(P, P, P, P, w)


def _sc_bwd(P, da, w, D, name):
    T = P.shape[0]
    cb = _tile(D, CONV_CB, LANE)
    nb = D // cb
    width = w.shape[0]
    tc = _tile(T, CONV_CHUNK, SUBLANE)
    nr, hp = _bank_rows(width)

    def body(b_ref, c_ref, u_ref, z_ref, da_ref, w_ref, db_ref, dc_ref, du_ref, dz_ref, dw_ref, bank, dv_s, dw_s):
        _bank_fill(bank, c_ref[...].astype(F32) * u_ref[...].astype(F32), width, T, True)

        def first(ci, carry):
            t0 = pl.multiple_of(ci * tc, tc)
            rows = pl.ds(t0, tc)
            v = _conv_chunk(bank, w_ref, t0, tc, width)
            bv, zv, dav = b_ref[rows, :].astype(F32), z_ref[rows, :].astype(F32), da_ref[rows, :].astype(F32)
            dyb = dav * _silu(zv)
            dz_ref[rows, :] = (dav * bv * v * _silu_grad(zv)).astype(dz_ref.dtype)
            db_ref[rows, :] = (dyb * v).astype(db_ref.dtype)
            dv_s[rows, :] = dyb * bv
            return carry

        lax.fori_loop(0, T // tc, first, 0)
        _bank_fill(bank, dv_s[...], width, T, False)
        dw_s[...] = jnp.zeros_like(dw_s)

        def second(ci, carry):
            t0 = pl.multiple_of(ci * tc, tc)
            rows = pl.ds(t0, tc)
            cv, uv = c_ref[rows, :].astype(F32), u_ref[rows, :].astype(F32)
            dcu = _conv_bwd_chunk(bank, w_ref, cv * uv, dw_s, t0, tc, width)
            dc_ref[rows, :] = (dcu * uv).astype(dc_ref.dtype)
            du_ref[rows, :] = (dcu * cv).astype(du_ref.dtype)
            return carry

        lax.fori_loop(0, T // tc, second, 0)
        dw_ref[...] = jnp.sum(dw_s[...], axis=1)

    outs = pl.pallas_call(
        body, name=name,
        out_shape=[jax.ShapeDtypeStruct((T, D), MXU)] * 4 + [jax.ShapeDtypeStruct((width, D), F32)],
        grid=(nb,),
        in_specs=[_full_t(T, cb, 0), _full_t(T, cb, nb), _full_t(T, cb, 2 * nb), _full_t(T, cb, 3 * nb),
                  _full_t(T, cb, 0), pl.BlockSpec((width, cb), lambda j: (0, j))],
        out_specs=[_full_t(T, cb, 0)] * 4 + [pl.BlockSpec((width, cb), lambda j: (0, j))],
        scratch_shapes=[pltpu.VMEM((nr, T + hp, cb), F32), pltpu.VMEM((T, cb), F32),
                        pltpu.VMEM((width, SUBLANE, cb), F32)],
        compiler_params=_cp("parallel"),
    )(P, P, P, P, da, w)
    return outs[:4], outs[4]


def _cf_conv_fwd(P, w, bias, D, name):
    T = P.shape[0]
    cb = _tile(D, CONV_CB, LANE)
    nb = D // cb
    width = w.shape[0]
    tc = _tile(T, CONV_CHUNK, SUBLANE)
    nr, hp = _bank_rows(width)

    def body(a_ref, g_ref, w_ref, b_ref, y_ref, bank):
        _bank_fill(bank, a_ref[...].astype(F32) * _sig(g_ref[...].astype(F32)), width, T, True)

        def chunk(ci, carry):
            t0 = pl.multiple_of(ci * tc, tc)
            y_ref[pl.ds(t0, tc), :] = (_conv_chunk(bank, w_ref, t0, tc, width) + b_ref[...]).astype(y_ref.dtype)
            return carry

        lax.fori_loop(0, T // tc, chunk, 0)

    return pl.pallas_call(
        body, name=name, out_shape=jax.ShapeDtypeStruct((T, D), F32), grid=(nb,),
        in_specs=[_full_t(T, cb, 0), _full_t(T, cb, nb), pl.BlockSpec((width, cb), lambda j: (0, j)),
                  pl.BlockSpec((1, cb), lambda j: (0, j))],
        out_specs=_full_t(T, cb, 0),
        scratch_shapes=[pltpu.VMEM((nr, T + hp, cb), F32)],
        compiler_params=_cp("parallel"),
    )(P, P, w, bias)


def _cf_conv_bwd(P, dyc, w, D, name):
    T = P.shape[0]
    cb = _tile(D, CONV_CB, LANE)
    nb = D // cb
    width = w.shape[0]
    tc = _tile(T, CONV_CHUNK, SUBLANE)
    nr, hp = _bank_rows(width)

    def body(a_ref, g_ref, dy_ref, w_ref, da_ref, dg_ref, dw_ref, db_ref, bank, dw_s):
        dyv = dy_ref[...]
        db_ref[...] = jnp.sum(dyv, axis=0, keepdims=True)
        _bank_fill(bank, dyv, width, T, False)
        dw_s[...] = jnp.zeros_like(dw_s)

        def chunk(ci, carry):
            t0 = pl.multiple_of(ci * tc, tc)
            rows = pl.ds(t0, tc)
            av, sg = a_ref[rows, :].astype(F32), _sig(g_ref[rows, :].astype(F32))
            dyg = _conv_bwd_chunk(bank, w_ref, av * sg, dw_s, t0, tc, width)
            da_ref[rows, :] = (dyg * sg).astype(da_ref.dtype)
            dg_ref[rows, :] = (dyg * av * sg * (1.0 - sg)).astype(dg_ref.dtype)
            return carry

        lax.fori_loop(0, T // tc, chunk, 0)
        dw_ref[...] = jnp.sum(dw_s[...], axis=1)

    return pl.pallas_call(
        body, name=name,
        out_shape=[jax.ShapeDtypeStruct((T, D), MXU), jax.ShapeDtypeStruct((T, D), MXU),
                   jax.ShapeDtypeStruct((width, D), F32), jax.ShapeDtypeStruct((1, D), F32)],
        grid=(nb,),
        in_specs=[_full_t(T, cb, 0), _full_t(T, cb, nb), _full_t(T, cb, 0),
                  pl.BlockSpec((width, cb), lambda j: (0, j))],
        out_specs=[_full_t(T, cb, 0), _full_t(T, cb, 0), pl.BlockSpec((width, cb), lambda j: (0, j)),
                   pl.BlockSpec((1, cb), lambda j: (0, j))],
        scratch_shapes=[pltpu.VMEM((nr, T + hp, cb), F32), pltpu.VMEM((width, SUBLANE, cb), F32)],
        compiler_params=_cp("parallel"),
    )(P, P, dyc, w)


def _layer_norm_stats(v):
    mu = jnp.mean(v, axis=-1, keepdims=True)
    cen = v - mu
    rstd = lax.rsqrt(jnp.mean(cen * cen, axis=-1, keepdims=True) + LN_EPS)
    return cen * rstd, rstd


def _layer_norm_bwd(dxh, xh, rstd):
    return rstd * (dxh - jnp.mean(dxh, axis=-1, keepdims=True) - xh * jnp.mean(dxh * xh, axis=-1, keepdims=True))


def _cf_act_fwd(yc, P, lg, lb, D, name):
    T = yc.shape[0]
    tr = _tile(T, ROW_TILE, SUBLANE)

    def body(y_ref, z_ref, g_ref, b_ref, o_ref):
        xh, _ = _layer_norm_stats(y_ref[...])
        yl = xh * g_ref[...] + b_ref[...]
        o_ref[...] = (_silu(yl) * _silu(z_ref[...].astype(F32))).astype(o_ref.dtype)

    row = pl.BlockSpec((tr, D), lambda i: (i, 0))
    vec = pl.BlockSpec((1, D), lambda i: (0, 0))
    return pl.pallas_call(
        body, name=name, out_shape=jax.ShapeDtypeStruct((T, D), MXU), grid=(T // tr,),
        in_specs=[row, pl.BlockSpec((tr, D), lambda i: (i, 2)), vec, vec], out_specs=row,
        compiler_params=_cp("parallel"),
    )(yc, P, lg, lb)


def _cf_act_bwd(yc, P, da, lg, lb, D, name):
    T = yc.shape[0]
    tr = _tile(T, ROW_TILE, SUBLANE)

    def body(y_ref, z_ref, da_ref, g_ref, b_ref, dy_ref, dz_ref, dg_ref, db_ref):
        @pl.when(pl.program_id(0) == 0)
        def _():
            dg_ref[...] = jnp.zeros_like(dg_ref)
            db_ref[...] = jnp.zeros_like(db_ref)

        xh, rstd = _layer_norm_stats(y_ref[...])
        yl = xh * g_ref[...] + b_ref[...]
        zv, dav = z_ref[...].astype(F32), da_ref[...].astype(F32)
        dz_ref[...] = (dav * _silu(yl) * _silu_grad(zv)).astype(dz_ref.dtype)
        dyl = dav * _silu(zv) * _silu_grad(yl)
        dg_ref[...] += jnp.sum(dyl * xh, axis=0, keepdims=True)
        db_ref[...] += jnp.sum(dyl, axis=0, keepdims=True)
        dy_ref[...] = _layer_norm_bwd(dyl * g_ref[...], xh, rstd)

    row = pl.BlockSpec((tr, D), lambda i: (i, 0))
    zcol = pl.BlockSpec((tr, D), lambda i: (i, 2))
    vec = pl.BlockSpec((1, D), lambda i: (0, 0))
    return pl.pallas_call(
        body, name=name,
        out_shape=[jax.ShapeDtypeStruct((T, D), F32), jax.ShapeDtypeStruct((T, D), MXU),
                   jax.ShapeDtypeStruct((1, D), F32), jax.ShapeDtypeStruct((1, D), F32)],
        grid=(T // tr,), in_specs=[row, zcol, row, vec, vec], out_specs=[row, row, vec, vec],
        compiler_params=_cp("arbitrary"),
    )(yc, P, da, lg, lb)


def _tril(w):
    row = lax.broadcasted_iota(jnp.int32, w.shape, 0)
    col = lax.broadcasted_iota(jnp.int32, w.shape, 1)
    return jnp.where(col <= row, w, 0.0)


def _gm_fwd(P, lg, lb, ws, bst, D, name):
    T = P.shape[0]
    G, ch, _ = ws.shape
    gw = D // G

    def body(p_ref, g_ref, b_ref, ws_ref, bs_ref, a_ref):
        uv, vv, zv = (p_ref[:, k * D:(k + 1) * D].astype(F32) for k in range(3))
        xh, _ = _layer_norm_stats(_gelu(vv))
        vn = (xh * g_ref[...] + b_ref[...]).astype(MXU)
        gate = _gelu(uv) * _silu(zv)
        for g in range(G):
            cols = slice(g * gw, (g + 1) * gw)
            s = lax.dot_general(_tril(ws_ref[g]).astype(MXU), vn[:, cols], NN, preferred_element_type=F32)
            a_ref[:, cols] = (gate[:, cols] * (s + bs_ref[:, g:g + 1])).astype(a_ref.dtype)

    vec = pl.BlockSpec((1, D), lambda i: (0, 0))
    return pl.pallas_call(
        body, name=name, out_shape=jax.ShapeDtypeStruct((T, D), MXU), grid=(T // ch,),
        in_specs=[pl.BlockSpec((ch, 3 * D), lambda i: (i, 0)), vec, vec,
                  pl.BlockSpec((G, ch, ch), lambda i: (0, 0, 0)), pl.BlockSpec((ch, G), lambda i: (0, 0))],
        out_specs=pl.BlockSpec((ch, D), lambda i: (i, 0)), compiler_params=_cp("parallel"),
    )(P, lg, lb, ws, bst)


def _gm_bwd(P, da, lg, lb, ws, bst, D, name):
    T = P.shape[0]
    G, ch, _ = ws.shape
    gw = D // G

    def body(p_ref, da_ref, g_ref, b_ref, ws_ref, bs_ref, dp_ref, dg_ref, db_ref, dws_ref, dbs_ref, dvn_s):
        @pl.when(pl.program_id(0) == 0)
        def _():
            dg_ref[...] = jnp.zeros_like(dg_ref)
            db_ref[...] = jnp.zeros_like(db_ref)
            dws_ref[...] = jnp.zeros_like(dws_ref)
            dbs_ref[...] = jnp.zeros_like(dbs_ref)

        uv, vv, zv = (p_ref[:, k * D:(k + 1) * D].astype(F32) for k in range(3))
        dav = da_ref[...].astype(F32)
        xh, rstd = _layer_norm_stats(_gelu(vv))
        vn = (xh * g_ref[...] + b_ref[...]).astype(MXU)
        ug, sz = _gelu(uv), _silu(zv)
        ds_all = dav * sz * ug
        for g in range(G):
            cols = slice(g * gw, (g + 1) * gw)
            wm = _tril(ws_ref[g]).astype(MXU)
            s = lax.dot_general(wm, vn[:, cols], NN, preferred_element_type=F32) + bs_ref[:, g:g + 1]
            dp_ref[:, g * gw:(g + 1) * gw] = (dav[:, cols] * sz[:, cols] * s * _gelu_grad(uv[:, cols])).astype(dp_ref.dtype)
            dp_ref[:, 2 * D + g * gw:2 * D + (g + 1) * gw] = (
                dav[:, cols] * ug[:, cols] * s * _silu_grad(zv[:, cols])).astype(dp_ref.dtype)
            ds = ds_all[:, cols]
            dsb = ds.astype(MXU)
            dvn_s[:, cols] = lax.dot_general(wm, dsb, TN, preferred_element_type=F32)
            dws_ref[g] += _tril(lax.dot_general(dsb, vn[:, cols], NT, preferred_element_type=F32))
            dbs_ref[g] += jnp.broadcast_to(jnp.sum(ds, axis=1, keepdims=True), (ch, LANE))
        dvn = dvn_s[...]
        dg_ref[...] += jnp.sum(dvn * xh, axis=0, keepdims=True)
        db_ref[...] += jnp.sum(dvn, axis=0, keepdims=True)
        dp_ref[:, D:2 * D] = (_layer_norm_bwd(dvn * g_ref[...], xh, rstd) * _gelu_grad(vv)).astype(dp_ref.dtype)

    vec = pl.BlockSpec((1, D), lambda i: (0, 0))
    return pl.pallas_call(
        body, name=name,
        out_shape=[jax.ShapeDtypeStruct((T, 3 * D), MXU), jax.ShapeDtypeStruct((1, D), F32), jax.ShapeDtypeStruct((1, D), F32),
                   jax.ShapeDtypeStruct((G, ch, ch), F32), jax.ShapeDtypeStruct((G, ch, LANE), F32)],
        grid=(T // ch,),
        in_specs=[pl.BlockSpec((ch, 3 * D), lambda i: (i, 0)), pl.BlockSpec((ch, D), lambda i: (i, 0)), vec, vec,
                  pl.BlockSpec((G, ch, ch), lambda i: (0, 0, 0)), pl.BlockSpec((ch, G), lambda i: (0, 0))],
        out_specs=[pl.BlockSpec((ch, 3 * D), lambda i: (i, 0)), vec, vec,
                   pl.BlockSpec((G, ch, ch), lambda i: (0, 0, 0)), pl.BlockSpec((G, ch, LANE), lambda i: (0, 0, 0))],
        scratch_shapes=[pltpu.VMEM((ch, D), F32)],
        compiler_params=_cp("arbitrary"),
    )(P, da, lg, lb, ws, bst)


def _adam_math(w, g, m, v):
    m = ADAM_B1 * m + (1.0 - ADAM_B1) * g
    v = ADAM_B2 * v + (1.0 - ADAM_B2) * (g * g)
    m_hat = m / (1.0 - ADAM_B1 ** ADAM_STEP)
    v_hat = v / (1.0 - ADAM_B2 ** ADAM_STEP)
    return -ADAM_LR * (m_hat / (jnp.sqrt(v_hat) + ADAM_EPS) + ADAM_WD * w), m, v


def _adam(w, m, v, gparts, name, hosts=None):
    R, C = w.shape
    n = gparts.shape[0]
    tr = _tile(R, max(SUBLANE, (1 << 18) // C), 16 if gparts.dtype != F32 else SUBLANE)

    def body(w_ref, m_ref, v_ref, gp_ref, g_ref, d_ref, mo_ref, vo_ref):
        g = gp_ref[0].astype(F32)
        for k in range(1, n):
            g = g + gp_ref[k].astype(F32)
        g_ref[...] = g
        d_ref[...], mo_ref[...], vo_ref[...] = _adam_math(w_ref[...], g, m_ref[...], v_ref[...])

    blk = pl.BlockSpec((tr, C), lambda i: (i, 0))
    return _hosted(
        hosts, R * C * 36.0 / HBM_BYTES_PER_US, body, force=True, name=name,
        out_shape=[jax.ShapeDtypeStruct((R, C), F32)] * 4, grid=(R // tr,),
        in_specs=[blk, blk, blk, pl.BlockSpec((n, tr, C), lambda i: (0, i, 0))], out_specs=[blk] * 4,
        args=[w, m, v, gparts], semantics=("parallel",))


def _sum_blocks(parts, name):
    n, R, C = parts.shape
    tr = _tile(R, 512, SUBLANE)

    def body(p_ref, o_ref):
        acc = p_ref[0]
        for k in range(1, n):
            acc = acc + p_ref[k]
        o_ref[...] = acc

    return pl.pallas_call(
        body, name=name, out_shape=jax.ShapeDtypeStruct((R, C), F32), grid=(R // tr,),
        in_specs=[pl.BlockSpec((n, tr, C), lambda i: (0, i, 0))], out_specs=pl.BlockSpec((tr, C), lambda i: (i, 0)),
        compiler_params=_cp("parallel"),
    )(parts)


def _pack_rows(shape):
    return -(-int(np.prod(shape)) // (SUBLANE * LANE)) * SUBLANE


def _pack(arrs):
    parts = []
    for a in arrs:
        flat = a.reshape(-1).astype(F32)
        rows = _pack_rows(a.shape)
        parts.append(jnp.pad(flat, (0, rows * LANE - flat.shape[0])).reshape(rows, LANE))
    return jnp.concatenate(parts, axis=0)


def _unpack(buf, shapes):
    out, off = [], 0
    for s in shapes:
        rows, n = _pack_rows(s), int(np.prod(s))
        out.append(buf[off:off + rows].reshape(-1)[:n].reshape(s))
        off += rows
    return out


def _cols_full(g):
    return jnp.transpose(g, (1, 0, 2)).reshape(g.shape[1], N_DEV * g.shape[2])


def _cols_blocks(w):
    R, N = w.shape
    return jnp.transpose(w.reshape(R, N_DEV, N // N_DEV), (1, 0, 2)).astype(WIRE)


def _rows_blocks(w):
    return w.reshape(N_DEV, w.shape[0] // N_DEV, w.shape[1]).astype(WIRE)


def kernel(x, positions, norm_pre, norm_post, w_in_mla, mla_q_norm, w_uq, mla_kv_norm, w_ukv, w_out_mla, w_in_sc, sc_conv, w_out_sc, w_in_gm, gm_ln_g, gm_ln_b, gm_w_s, gm_b_s, w_out_gm, w_in_cf, cf_dw, cf_dw_b, cf_ln_g, cf_ln_b, w_out_cf, loss_target, m_norm_pre, m_norm_post, m_w_in_mla, m_mla_q_norm, m_w_uq, m_mla_kv_norm, m_w_ukv, m_w_out_mla, m_w_in_sc, m_sc_conv, m_w_out_sc, m_w_in_gm, m_gm_ln_g, m_gm_ln_b, m_gm_w_s, m_gm_b_s, m_w_out_gm, m_w_in_cf, m_cf_dw, m_cf_dw_b, m_cf_ln_g, m_cf_ln_b, m_w_out_cf, v_norm_pre, v_norm_post, v_w_in_mla, v_mla_q_norm, v_w_uq, v_mla_kv_norm, v_w_ukv, v_w_out_mla, v_w_in_sc, v_sc_conv, v_w_out_sc, v_w_in_gm, v_gm_ln_g, v_gm_ln_b, v_gm_w_s, v_gm_b_s, v_w_out_gm, v_w_in_cf, v_cf_dw, v_cf_dw_b, v_cf_ln_g, v_cf_ln_b, v_w_out_cf):
    names = ['norm_pre', 'norm_post', 'w_in_mla', 'mla_q_norm', 'w_uq', 'mla_kv_norm', 'w_ukv', 'w_out_mla', 'w_in_sc',
             'sc_conv', 'w_out_sc', 'w_in_gm', 'gm_ln_g', 'gm_ln_b', 'gm_w_s', 'gm_b_s', 'w_out_gm', 'w_in_cf', 'cf_dw',
             'cf_dw_b', 'cf_ln_g', 'cf_ln_b', 'w_out_cf']
    W = dict(zip(names, (norm_pre, norm_post, w_in_mla, mla_q_norm, w_uq, mla_kv_norm, w_ukv, w_out_mla, w_in_sc, sc_conv,
                         w_out_sc, w_in_gm, gm_ln_g, gm_ln_b, gm_w_s, gm_b_s, w_out_gm, w_in_cf, cf_dw, cf_dw_b, cf_ln_g,
                         cf_ln_b, w_out_cf)))
    Mo = dict(zip(names, (m_norm_pre, m_norm_post, m_w_in_mla, m_mla_q_norm, m_w_uq, m_mla_kv_norm, m_w_ukv, m_w_out_mla,
                          m_w_in_sc, m_sc_conv, m_w_out_sc, m_w_in_gm, m_gm_ln_g, m_gm_ln_b, m_gm_w_s, m_gm_b_s, m_w_out_gm,
                          m_w_in_cf, m_cf_dw, m_cf_dw_b, m_cf_ln_g, m_cf_ln_b, m_w_out_cf)))
    Vo = dict(zip(names, (v_norm_pre, v_norm_post, v_w_in_mla, v_mla_q_norm, v_w_uq, v_mla_kv_norm, v_w_ukv, v_w_out_mla,
                          v_w_in_sc, v_sc_conv, v_w_out_sc, v_w_in_gm, v_gm_ln_g, v_gm_ln_b, v_gm_w_s, v_gm_b_s, v_w_out_gm,
                          v_w_in_cf, v_cf_dw, v_cf_dw_b, v_cf_ln_g, v_cf_ln_b, v_w_out_cf)))
    big = ['w_in_mla', 'w_uq', 'w_ukv', 'w_out_mla', 'w_in_sc', 'w_out_sc', 'w_in_gm', 'w_out_gm', 'w_in_cf', 'w_out_cf']
    row_sharded = {'w_out_mla', 'w_out_sc', 'w_out_gm', 'w_out_cf'}
    chan = ['sc_conv', 'gm_ln_g', 'gm_ln_b', 'cf_dw', 'cf_dw_b', 'cf_ln_g', 'cf_ln_b']
    repl = ['norm_pre', 'norm_post', 'mla_q_norm', 'mla_kv_norm', 'gm_w_s', 'gm_b_s']

    T, D = x.shape[1], x.shape[2]
    xs, tgt = x[0], loss_target[0]
    pos = positions.reshape(T, 1)
    qr, kvr = mla_q_norm.shape[-1], mla_kv_norm.shape[-1]
    H = w_uq.shape[-1] * N_DEV // (NOPE + ROPE)
    HV = H * VHEAD
    c3 = qr + kvr + ROPE
    pa = -(-c3 // 512) * 512
    assert qr % kvr == 0 and qr % LANE == 0 and kvr % LANE == 0 and pa >= qr + kvr + LANE
    mx, my, mc = _mesh_pos()
    me = 4 * mx + 2 * my + mc
    core = mc.astype(jnp.int32).reshape(1)
    cs = D // N_DEV

    def wire(nm):
        return W[nm][0].astype(WIRE)

    hosts = _Hosts()
    G = {}

    def gather(nms, name=None):
        job = _GatherJob([wire(nm) for nm in nms])
        if name is None:
            hosts.add(job, lambda outs: G.update(zip(nms, outs)))
        else:
            G.update(zip(nms, _run_job(job, name)))

    def gathered_rows(nm):
        if nm not in G:
            hosts.flush("ag_late")
        return G[nm].reshape(G[nm].shape[0] * G[nm].shape[1], G[nm].shape[2])

    def gathered_cols(nm):
        if nm not in G:
            hosts.flush("ag_late")
        return _cols_full(G[nm])

    gather(['w_in_mla'], "ag_mla")
    chan_rows = [W[nm][0].reshape(-1, cs) for nm in chan]
    chan_cnt = [r.shape[0] for r in chan_rows]
    chan_local = jnp.concatenate(chan_rows, axis=0)
    chan_pad = -chan_local.shape[0] % SUBLANE
    chan_full = _cols_full(_run_job(_GatherJob([jnp.pad(chan_local, ((0, chan_pad), (0, 0)))]), "ag_small")[0])
    offs = np.cumsum([0] + chan_cnt)
    CH = {nm: chan_full[offs[i]:offs[i + 1]] for i, nm in enumerate(chan)}

    w_in_full = gathered_cols('w_in_mla')
    w_cat = jnp.concatenate([w_in_full[:, :c3], jnp.zeros((D, pa - c3), WIRE), w_in_full[:, c3:]], axis=1)
    in_names = ['w_in_mla', 'w_in_sc', 'w_in_gm', 'w_in_cf']
    out_names = ['w_out_mla', 'w_out_sc', 'w_out_gm', 'w_out_cf']
    WIN, WOUT = [w_cat], []

    half = ROPE // 2
    invf_np = np.zeros((1, LANE), np.float32)
    invf_np[0, :ROPE] = np.tile(np.float32(ROPE_THETA) ** (-np.arange(half, dtype=np.float32) / np.float32(half)), 2)
    invf = jnp.asarray(invf_np)
    gm_ws = gm_w_s[0]
    gm_bst = jnp.transpose(gm_b_s[0])

    xin, hs, Ps, acts, ys, keep = [xs], [], [], [], [], {}
    h = _rms_fwd(xs, norm_pre[0:1], "pre0")
    for i in range(4):
        if i == 0:
            gather(['w_uq', 'w_ukv'])
        else:
            WIN.append(gathered_cols(in_names[i]))
            if i < 3:
                gather([in_names[i + 1]])
        P = _mm(h, WIN[i], out_dtype=MXU, name=f"in{i}", hosts=hosts)
        if i == 0:
            w_uq_pad = jnp.pad(gathered_cols('w_uq').reshape(qr, H, NOPE + ROPE),
                               ((0, 0), (0, 0), (0, QPAD - NOPE - ROPE))).reshape(qr, H * QPAD)
            w_ukv3 = gathered_cols('w_ukv').reshape(kvr, H, NOPE + VHEAD)
            w_k_pad = jnp.pad(w_ukv3[:, :, :NOPE], ((0, 0), (0, 0), (0, QPAD - NOPE))).reshape(kvr, H * QPAD)
            w_v = w_ukv3[:, :, NOPE:].reshape(kvr, HV)
            cqn, ckvn = _mla_norms_fwd(P, mla_q_norm, mla_kv_norm, qr, kvr, "mla_norms")
            qpad = _mm(cqn, w_uq_pad, out_dtype=MXU, name="mla_q")
            kpad = _mm(ckvn, w_k_pad, out_dtype=MXU, name="mla_k")
            vv = _mm(ckvn, w_v, out_dtype=MXU, name="mla_v")
            qcat, kcat = _rope_fwd(qpad, kpad, P, pos, invf, (qr + kvr) // LANE, H, "rope")
            gather([in_names[1], out_names[1], out_names[0]])
            o, act, lse = _flash_fwd(qcat, kcat, vv, P, pa // VHEAD, H, "attn", hosts=hosts)
            keep.update(cqn=cqn, ckvn=ckvn, qcat=qcat, kcat=kcat, v=vv, o=o, lse=lse)
        elif i == 1:
            act = _sc_fwd(P, CH['sc_conv'], D, "sc_mix")
        elif i == 2:
            act = _gm_fwd(P, CH['gm_ln_g'], CH['gm_ln_b'], gm_ws, gm_bst, D, "gm_mix")
        else:
            yc = _cf_conv_fwd(P, CH['cf_dw'], CH['cf_dw_b'], D, "cf_conv")
            act = _cf_act_fwd(yc, P, CH['cf_ln_g'], CH['cf_ln_b'], D, "cf_act")
            keep.update(yc=yc)
        WOUT.append(gathered_rows(out_names[i]))
        if 1 <= i < 3:
            gather([out_names[i + 1]])
        y = _mm(act, WOUT[i], out_dtype=F32, name=f"out{i}", hosts=hosts)
        hs.append(h), Ps.append(P), acts.append(act), ys.append(y)
        if i < 3:
            x_next, h = _post_fwd(y, norm_post[i:i + 1], xin[i], f"post{i}", next_gain=norm_pre[i + 1:i + 2])
            xin.append(x_next)
        else:
            dx, loss_part = _post_fwd(y, norm_post[i:i + 1], xin[i], f"post{i}", target=tgt)
    loss = lax.psum(loss_part[0, 0], ("x", "y", "c"))

    hosts.flush("ag_late")
    dnorm_pre, dnorm_post, small_g, RS = [None] * 4, [None] * 4, {}, {}
    for i in (3, 2, 1, 0):
        dy, dnorm_post[i] = _rms_bwd(ys[i], dx, norm_post[i:i + 1], f"post_bwd{i}", out_dtype=MXU)
        da = _mm(dy, WOUT[i], tb=True, out_dtype=MXU, name=f"out_bwd{i}", hosts=hosts)
        dw_out = _mm(acts[i], dy, ta=True, out_dtype=WIRE, name=f"out_dw{i}", hosts=hosts).reshape(N_DEV, -1, D)
        _queue_reduce_scatter(hosts, {out_names[i]: dw_out}, core, f"rs_out{i}", RS)
        P = Ps[i]
        if i == 3:
            dyc, d_z, small_g['cf_ln_g'], small_g['cf_ln_b'] = _cf_act_bwd(
                keep['yc'], P, da, CH['cf_ln_g'], CH['cf_ln_b'], D, "cf_act_bwd")
            d_a, d_g, small_g['cf_dw'], small_g['cf_dw_b'] = _cf_conv_bwd(P, dyc, CH['cf_dw'], D, "cf_conv_bwd")
            dP = jnp.concatenate([d_a, d_g, d_z], axis=1)
        elif i == 2:
            dP, small_g['gm_ln_g'], small_g['gm_ln_b'], small_g['gm_w_s'], dbs = _gm_bwd(
                P, da, CH['gm_ln_g'], CH['gm_ln_b'], gm_ws, gm_bst, D, "gm_mix_bwd")
            small_g['gm_b_s'] = dbs[:, :, 0]
        elif i == 1:
            parts, small_g['sc_conv'] = _sc_bwd(P, da, CH['sc_conv'], D, "sc_mix_bwd")
            dP = jnp.concatenate(parts, axis=1)
        else:
            do, d_z, delta = _mla_gate_bwd(da, keep['o'], P, pa, H, "gate_bwd")
            dqcat, dkcat, dv = _flash_bwd(keep['qcat'], keep['kcat'], keep['v'], do, keep['lse'], delta, H, "attn_bwd",
                                          hosts=hosts)
            dqpad, dkr = _rope_bwd(dqcat, dkcat, pos, invf, H, "rope_bwd")
            dcqn = _mm(dqpad, w_uq_pad, tb=True, out_dtype=F32, name="mla_q_bwd", hosts=hosts)
            dckvn_k = _mm(dkcat, w_k_pad, tb=True, out_dtype=F32, name="mla_k_bwd", hosts=hosts)
            dckvn_v = _mm(dv, w_v, tb=True, out_dtype=F32, name="mla_v_bwd", hosts=hosts)
            dw_uq_pad = _mm(keep['cqn'], dqpad, ta=True, out_dtype=F32, name="mla_q_dw", hosts=hosts)
            dw_k_pad = _mm(keep['ckvn'], dkcat, ta=True, out_dtype=F32, name="mla_k_dw", hosts=hosts)
            dw_v = _mm(keep['ckvn'], dv, ta=True, out_dtype=F32, name="mla_v_dw", hosts=hosts)
            dw_uq = _cols_blocks(dw_uq_pad.reshape(qr, H, QPAD)[:, :, :NOPE + ROPE].reshape(qr, H * (NOPE + ROPE)))
            dw_ukv = _cols_blocks(jnp.concatenate(
                [dw_k_pad.reshape(kvr, H, QPAD)[:, :, :NOPE], dw_v.reshape(kvr, H, VHEAD)], axis=2).reshape(kvr, H * (NOPE + VHEAD)))
            _queue_reduce_scatter(hosts, {'w_uq': dw_uq, 'w_ukv': dw_ukv}, core, "rs_lat", RS)
            d_pa, small_g['mla_q_norm'], small_g['mla_kv_norm'] = _mla_norms_bwd(
                P, dcqn, dckvn_k, dckvn_v, dkr, mla_q_norm, mla_kv_norm, qr, kvr, pa, "mla_norms_bwd")
            dP = jnp.concatenate([d_pa, d_z], axis=1)
        dh = _mm(dP, WIN[i], tb=True, out_dtype=F32, name=f"in_bwd{i}", hosts=hosts)
        dx, dnorm_pre[i] = _rms_bwd(xin[i], dh, norm_pre[i:i + 1], f"pre_bwd{i}", resid=dx)
        if i == 0:
            dw_cat = _mm(hs[i], dP, ta=True, out_dtype=F32, name="in_dw0", hosts=hosts)
            dw_in = _cols_blocks(jnp.concatenate([dw_cat[:, :c3], dw_cat[:, pa:]], axis=1))
        else:
            dw_in = _mm(hs[i], dP, ta=True, out_dtype=WIRE, name=f"in_dw{i}", blocks=N_DEV, hosts=hosts)
        _queue_reduce_scatter(hosts, {in_names[i]: dw_in}, core, f"rs_in{i}", RS)
    grad_x = dx[None]

    small_g['norm_pre'] = jnp.concatenate(dnorm_pre, axis=0)
    small_g['norm_post'] = jnp.concatenate(dnorm_post, axis=0)
    small_names = repl + chan
    small_shapes = [(small_g[nm].shape) for nm in small_names]
    tail, big_out = {}, {}
    hosts.add(_GatherJob([_pack([small_g[nm] for nm in small_names])]), lambda outs: tail.update(grads=outs[0]))

    def adam_big(nm, carrier=None):
        if nm not in RS:
            hosts.flush("rs_late")
        big_out[nm] = [o[None] for o in _adam(W[nm][0], Mo[nm][0], Vo[nm][0], RS[nm], "adam_" + nm, hosts=carrier)]

    for nm in ('w_in_sc', 'w_in_gm', 'w_in_cf', 'w_out_sc', 'w_out_gm', 'w_out_cf'):
        adam_big(nm, hosts)
    hosts.flush("rs_late")
    for nm in ('w_in_mla', 'w_uq', 'w_ukv', 'w_out_mla'):
        adam_big(nm)
    totals = dict(zip(small_names, _unpack(_sum_blocks(tail['grads'], "sum_grads"), small_shapes)))
    local_g = []
    for nm in small_names:
        g = totals[nm]
        if nm in chan:
            g = lax.dynamic_slice_in_dim(g.reshape(-1, D), me * cs, cs, axis=1)
        local_g.append(g.reshape(W[nm].shape))
    local_shapes = [W[nm].shape for nm in small_names]
    sm = _adam(_pack([W[nm] for nm in small_names]), _pack([Mo[nm] for nm in small_names]),
               _pack([Vo[nm] for nm in small_names]), _pack(local_g)[None], "adam_small")
    small_out = [dict(zip(small_names, _unpack(buf, local_shapes))) for buf in sm]

    outs = [loss, grad_x]
    for k in range(4):
        outs += [big_out[nm][k] if nm in big_out else small_out[k][nm] for nm in names]
    return tuple(outs)
```

```python
import functools
import math

import numpy as np
import jax
import jax.numpy as jnp
from jax import lax
from jax.experimental import pallas as pl
from jax.experimental.pallas import tpu as pltpu

F32 = jnp.float32
MXU = jnp.bfloat16
WIRE = jnp.bfloat16
MESH = pl.DeviceIdType.MESH
N_DEV = 8

NORM_EPS = 1e-6
LN_EPS = 1e-5
ROPE_THETA = 10000.0
NOPE, ROPE, VHEAD = 128, 64, 128
QPAD = 256
ATT_SCALE = float((NOPE + ROPE) ** -0.5)
LOG2E = 1.4426950408889634
INV_SQRT2 = 0.7071067811865476
INV_SQRT_2PI = 0.3989422804014327
ADAM_LR, ADAM_B1, ADAM_B2, ADAM_EPS, ADAM_WD, ADAM_STEP = 0.001, 0.9, 0.999, 1e-08, 0.01, 10

LANE = 128
SUBLANE = 8
VMEM_LIMIT = 56 * 1024 * 1024
MM_TILE = 1024
MM_TK = 4096
ATT_TILE = 512
ROW_TILE = 128
CONV_CHUNK = 256
CONV_CB = 128
MXU_FLOP_PER_US = 9.2e8
HBM_BYTES_PER_US = 3.0e6
ATT_FWD_US, ATT_BWD_US = 1150.0, 1600.0
HOST_SLACK = 1.2
GATHER_PIECE = 512

NT = (((1,), (1,)), ((), ()))
TN = (((0,), (0,)), ((), ()))
NN = (((1,), (0,)), ((), ()))


def _tile(dim, pref, align):
    t = min(pref, dim)
    t -= t % align
    while t >= align:
        if dim % t == 0:
            return t
        t -= align
    return dim


def _cp(*sem):
    return pltpu.CompilerParams(dimension_semantics=sem, vmem_limit_bytes=VMEM_LIMIT)


def _sig(v):
    return jax.nn.sigmoid(v)


def _silu(v):
    return v * _sig(v)


def _silu_grad(v):
    s = _sig(v)
    return s * (1.0 + v * (1.0 - s))


def _gelu(v):
    return 0.5 * v * (1.0 + lax.erf(v * INV_SQRT2))


def _gelu_grad(v):
    return 0.5 * (1.0 + lax.erf(v * INV_SQRT2)) + v * jnp.exp(-0.5 * v * v) * INV_SQRT_2PI


def _mesh_pos():
    return lax.axis_index("x"), lax.axis_index("y"), lax.axis_index("c")


def _other_chips(x, y):
    return [(1 - x, y), (x, 1 - y), (1 - x, 1 - y)]


class _GatherJob:
    US_PER_MB = 46.0

    def __init__(self, arrs):
        n = len(arrs)
        self.ins = list(arrs)
        self.out_shape = [jax.ShapeDtypeStruct((N_DEV,) + a.shape, a.dtype) for a in arrs]
        self.sems = [pltpu.SemaphoreType.DMA((n, 7)), pltpu.SemaphoreType.DMA((n, 7)), pltpu.SemaphoreType.DMA((n,))]
        self.cost = self.US_PER_MB * sum(a.size * a.dtype.itemsize for a in arrs) / 1e6

    def _parts(self, ins, outs, sems):
        send_sems, recv_sems, loc_sems = sems
        x, y, c = _mesh_pos()
        n = len(ins)

        def copy(a, k, block, to, src=None):
            dst = outs[a].at[4 * block[0] + 2 * block[1] + block[2]]
            return pltpu.make_async_remote_copy(
                src_ref=dst if src is None else src, dst_ref=dst,
                send_sem=send_sems.at[a, k], recv_sem=recv_sems.at[a, k],
                device_id=to, device_id_type=MESH)

        me, sib, chips = (x, y, c), (x, y, 1 - c), _other_chips(x, y)
        locs = [pltpu.make_async_copy(ins[a], outs[a].at[4 * x + 2 * y + c], loc_sems.at[a]) for a in range(n)]
        first = []
        for a in range(n):
            first.append(copy(a, 0, me, sib, src=ins[a]))
            for j, chip in enumerate(chips):
                first.append(copy(a, 1 + j, me, (*chip, c), src=ins[a]))
        passed = [copy(a, 4 + j, (*chip, c), sib) for j, chip in enumerate(chips) for a in range(n)]
        landed = [copy(a, 1 + j, (*chip, c), me) for j, chip in enumerate(chips) for a in range(n)]
        from_sib = [copy(a, 0, sib, me) for a in range(n)]
        from_sib += [copy(a, 4 + j, (*chip, 1 - c), me) for j, chip in enumerate(chips) for a in range(n)]
        return locs, first, landed, passed, from_sib

    def first(self, ins, outs, sems):
        locs, first, _, _, _ = self._parts(ins, outs, sems)
        for cp in locs + first:
            cp.start()

    def mid(self, ins, outs, sems):
        _, _, landed, passed, _ = self._parts(ins, outs, sems)
        for got, fwd in zip(landed, passed):
            got.wait_recv()
            fwd.start()

    def last(self, ins, outs, sems):
        locs, first, _, passed, from_sib = self._parts(ins, outs, sems)
        for cp in from_sib:
            cp.wait_recv()
        for cp in first + passed:
            cp.wait_send()
        for cp in locs:
            cp.wait()


class _SwapJob:
    US_PER_MB = 0.8

    def __init__(self, parts):
        n = len(parts)
        self.ins = list(parts)
        self.out_shape = [jax.ShapeDtypeStruct((4, 1) + p.shape[2:], p.dtype) for p in parts]
        self.sems = [pltpu.SemaphoreType.DMA((n,)), pltpu.SemaphoreType.DMA((n,))]
        self.cost = 5.0 + self.US_PER_MB * sum(p.size * p.dtype.itemsize for p in parts) / 1e6

    def _copies(self, ins, outs, sems):
        send_sems, recv_sems = sems
        x, y, c = _mesh_pos()
        return [pltpu.make_async_remote_copy(
            src_ref=ins[a].at[:, pl.ds(1 - c, 1)], dst_ref=outs[a],
            send_sem=send_sems.at[a], recv_sem=recv_sems.at[a],
            device_id=(x, y, 1 - c), device_id_type=MESH) for a in range(len(ins))]

    def first(self, ins, outs, sems):
        for cp in self._copies(ins, outs, sems):
            cp.start()

    def mid(self, ins, outs, sems):
        pass

    def last(self, ins, outs, sems):
        for cp in self._copies(ins, outs, sems):
            cp.wait()


class _ChipsJob:
    US_PER_MB = 11.0

    def __init__(self, qs):
        n = len(qs)
        self.ins = list(qs)
        self.out_shape = [jax.ShapeDtypeStruct(q.shape, q.dtype) for q in qs]
        self.sems = [pltpu.SemaphoreType.DMA((n, 3)), pltpu.SemaphoreType.DMA((n, 3)), pltpu.SemaphoreType.DMA((n,))]
        self.cost = 5.0 + self.US_PER_MB * sum(q.size * q.dtype.itemsize for q in qs) / 1e6

    def _copies(self, ins, outs, sems):
        send_sems, recv_sems, loc_sems = sems
        x, y, c = _mesh_pos()
        cps = []
        for a in range(len(ins)):
            cps.append(pltpu.make_async_copy(ins[a].at[2 * x + y], outs[a].at[3], loc_sems.at[a]))
            for k, chip in enumerate(_other_chips(x, y)):
                cps.append(pltpu.make_async_remote_copy(
                    src_ref=ins[a].at[2 * chip[0] + chip[1]], dst_ref=outs[a].at[k],
                    send_sem=send_sems.at[a, k], recv_sem=recv_sems.at[a, k],
                    device_id=(*chip, c), device_id_type=MESH))
        return cps

    def first(self, ins, outs, sems):
        for cp in self._copies(ins, outs, sems):
            cp.start()

    def mid(self, ins, outs, sems):
        pass

    def last(self, ins, outs, sems):
        for cp in self._copies(ins, outs, sems):
            cp.wait()


def _call(body, *, name, out_shape, in_specs, out_specs, args, grid=(), scratch_shapes=(), semantics=(), job=None):
    if job is None:
        outs = pl.pallas_call(
            body, name=name, out_shape=out_shape, grid=grid, in_specs=in_specs, out_specs=out_specs,
            scratch_shapes=list(scratch_shapes), compiler_params=_cp(*semantics))(*args)
        return outs, None
    single = not isinstance(out_shape, (list, tuple))
    host_out = [out_shape] if single else list(out_shape)
    host_ospecs = [out_specs] if single else list(out_specs)
    n_in, n_out, n_scr = len(in_specs), len(host_out), len(scratch_shapes)
    j_in, j_out = len(job.ins), len(job.out_shape)
    total = int(np.prod(grid)) if grid else 1
    mid_step = max(total * 7 // 8, 1)

    def full(*refs):
        h_in, jin = refs[:n_in], refs[n_in:n_in + j_in]
        o = n_in + j_in
        h_out, jout = refs[o:o + n_out], refs[o + n_out:o + n_out + j_out]
        o += n_out + j_out
        h_scr, jsem = refs[o:o + n_scr], refs[o + n_scr:]
        if total == 1:
            job.first(jin, jout, jsem)
            body(*h_in, *h_out, *h_scr)
            job.mid(jin, jout, jsem)
            job.last(jin, jout, jsem)
            return
        step = pl.program_id(0)
        for d in range(1, len(grid)):
            step = step * grid[d] + pl.program_id(d)
        pl.when(step == 0)(lambda: job.first(jin, jout, jsem))
        body(*h_in, *h_out, *h_scr)
        pl.when(step == mid_step)(lambda: job.mid(jin, jout, jsem))
        pl.when(step == total - 1)(lambda: job.last(jin, jout, jsem))

    hbm = pl.BlockSpec(memory_space=pltpu.HBM)
    outs = pl.pallas_call(
        full, name=name, out_shape=host_out + list(job.out_shape), grid=grid,
        in_specs=list(in_specs) + [hbm] * j_in, out_specs=host_ospecs + [hbm] * j_out,
        scratch_shapes=list(scratch_shapes) + list(job.sems),
        compiler_params=_cp(*(("arbitrary",) * len(grid))))(*args, *job.ins)
    host = outs[0] if single else list(outs[:n_out])
    return host, list(outs[n_out:])


def _run_job(job, name):
    def body():
        pass

    return _call(body, name=name, out_shape=[], in_specs=[], out_specs=[], args=[], job=job)[1]


class _Hosts:
    def __init__(self):
        self.pending = []
        self.flushed = 0

    def add(self, job, done):
        self.pending.append((job, done))

    def take(self, duration, force=False):
        taken, room = [], HOST_SLACK * duration
        for e in list(self.pending):
            if e[0].cost <= room:
                taken.append(e)
                room -= e[0].cost
        if not taken and force and self.pending:
            taken = [min(self.pending, key=lambda e: e[0].cost)]
        for e in taken:
            self.pending.remove(e)
        return taken

    def flush(self, name):
        while self.pending:
            job, done = self.pending.pop(0)
            done(_run_job(job, f"{name}{self.flushed}"))
            self.flushed += 1


def _pair_add(part4, recv, core, name):
    _, _, R, C = part4.shape
    tr = _tile(R, max(SUBLANE, (1 << 19) // C), 16)

    def body(core_ref, p_ref, r_ref, o_ref):
        o_ref[...] = (p_ref[...].astype(F32) + r_ref[...].astype(F32)).astype(o_ref.dtype)

    return pl.pallas_call(
        body, name=name,
        out_shape=jax.ShapeDtypeStruct((4, R, C), WIRE),
        grid_spec=pltpu.PrefetchScalarGridSpec(
            num_scalar_prefetch=1, grid=(4, R // tr),
            in_specs=[pl.BlockSpec((None, None, tr, C), lambda s, i, cr: (s, cr[0], i, 0)),
                      pl.BlockSpec((None, None, tr, C), lambda s, i, cr: (s, 0, i, 0))],
            out_specs=pl.BlockSpec((None, tr, C), lambda s, i, cr: (s, i, 0))),
        compiler_params=_cp("parallel", "parallel"),
    )(core, part4, recv)


def _queue_reduce_scatter(hosts, parts, core, name, sink):
    keys = list(parts)
    p4 = [parts[k].reshape((4, 2) + parts[k].shape[1:]) for k in keys]

    def swapped(recv):
        qs = [_pair_add(p, r, core, f"{name}_add{i}") for i, (p, r) in enumerate(zip(p4, recv))]
        hosts.add(_ChipsJob(qs), lambda outs: sink.update(zip(keys, outs)))

    hosts.add(_SwapJob(p4), swapped)


class _JobGroup:
    def __init__(self, jobs):
        self.jobs = jobs
        self.ins = [a for j in jobs for a in j.ins]
        self.out_shape = [s for j in jobs for s in j.out_shape]
        self.sems = [s for j in jobs for s in j.sems]

    def _each(self, phase, ins, outs, sems):
        i = o = s = 0
        for j in self.jobs:
            ni, no, ns = len(j.ins), len(j.out_shape), len(j.sems)
            getattr(j, phase)(ins[i:i + ni], outs[o:o + no], sems[s:s + ns])
            i, o, s = i + ni, o + no, s + ns

    def first(self, ins, outs, sems):
        self._each("first", ins, outs, sems)

    def mid(self, ins, outs, sems):
        self._each("mid", ins, outs, sems)

    def last(self, ins, outs, sems):
        self._each("last", ins, outs, sems)


def _hosted(hosts, duration, body, force=False, **kw):
    entries = hosts.take(duration, force) if hosts is not None else []
    out, jouts = _call(body, job=_JobGroup([e[0] for e in entries]) if entries else None, **kw)
    o = 0
    for job, done in entries:
        done(jouts[o:o + len(job.out_shape)])
        o += len(job.out_shape)
    return out


def _mm(a, b, *, ta=False, tb=False, out_dtype, name, blocks=None, hosts=None):
    a_parts = a.shape[0] if a.ndim == 3 else 0
    b_parts = b.shape[0] if b.ndim == 3 else 0
    assert not (a_parts and ta) and not (b_parts and tb)
    if a_parts:
        M, K = a.shape[1], a_parts * a.shape[2]
    else:
        M, K = (a.shape[1], a.shape[0]) if ta else a.shape
    N = b_parts * b.shape[2] if b_parts else (b.shape[0] if tb else b.shape[1])
    assert K == (b.shape[1] if (tb or b_parts) else b.shape[0]), (a.shape, b.shape, ta, tb)
    ns = N // blocks if blocks else N
    tm = _tile(M, MM_TILE, LANE)
    tk = _tile(a.shape[2] if a_parts else K, MM_TK, LANE)
    tn = _tile(math.gcd(ns, b.shape[2]) if b_parts else ns, MM_TILE, LANE)
    nk = K // tk
    per = ns // tn

    dims = (((0,) if ta else (1,), (1,) if tb else (0,)), ((), ()))

    def body(a_ref, b_ref, o_ref, *acc):
        part = lax.dot_general(a_ref[...], b_ref[...], dims, preferred_element_type=F32)
        if nk == 1:
            o_ref[...] = part.astype(o_ref.dtype)
            return
        acc_ref, k = acc[0], pl.program_id(2)

        @pl.when(k == 0)
        def _():
            acc_ref[...] = part

        @pl.when(k > 0)
        def _():
            acc_ref[...] += part

        @pl.when(k == nk - 1)
        def _():
            o_ref[...] = acc_ref[...].astype(o_ref.dtype)

    if a_parts:
        a_per = a.shape[2] // tk
        a_spec = pl.BlockSpec((None, tm, tk), lambda i, j, k: (k // a_per, i, k % a_per))
    else:
        a_spec = pl.BlockSpec((tk, tm), lambda i, j, k: (k, i)) if ta else pl.BlockSpec((tm, tk), lambda i, j, k: (i, k))
    if b_parts:
        b_per = b.shape[2] // tn
        b_spec = pl.BlockSpec((None, tk, tn), lambda i, j, k: (j // b_per, k, j % b_per))
    else:
        b_spec = pl.BlockSpec((tn, tk), lambda i, j, k: (j, k)) if tb else pl.BlockSpec((tk, tn), lambda i, j, k: (k, j))
    if blocks:
        out_shape = jax.ShapeDtypeStruct((blocks, M, ns), out_dtype)
        o_spec = pl.BlockSpec((None, tm, tn), lambda i, j, k: (j // per, i, j % per))
    else:
        out_shape = jax.ShapeDtypeStruct((M, N), out_dtype)
        o_spec = pl.BlockSpec((tm, tn), lambda i, j, k: (i, j))
    return _hosted(
        hosts, 2.0 * M * N * K / MXU_FLOP_PER_US, body, name=name, out_shape=out_shape, grid=(M // tm, N // tn, nk),
        in_specs=[a_spec, b_spec], out_specs=o_spec, args=[a, b],
        scratch_shapes=[pltpu.VMEM((tm, tn), F32)] if nk > 1 else [], semantics=("parallel", "parallel", "arbitrary"))


def _rms_fwd(xin, g, name):
    T, D = xin.shape
    tr = _tile(T, ROW_TILE, SUBLANE)

    def body(x_ref, g_ref, o_ref):
        xv = x_ref[...]
        r = lax.rsqrt(jnp.mean(xv * xv, axis=-1, keepdims=True) + NORM_EPS)
        o_ref[...] = (xv * r * g_ref[...]).astype(o_ref.dtype)

    row = pl.BlockSpec((tr, D), lambda i: (i, 0))
    return pl.pallas_call(
        body, name=name, out_shape=jax.ShapeDtypeStruct((T, D), MXU),
        grid=(T // tr,), in_specs=[row, pl.BlockSpec((1, D), lambda i: (0, 0))],
        out_specs=row, compiler_params=_cp("parallel"),
    )(xin, g)


def _rms_bwd(xin, dout, g, name, resid=None, out_dtype=F32):
    T, D = xin.shape
    tr = _tile(T, ROW_TILE, SUBLANE)

    def body(*refs):
        x_ref, d_ref, g_ref = refs[:3]
        dx_ref, dg_ref = refs[-2:]
        xv = x_ref[...].astype(F32)
        dv = d_ref[...].astype(F32)
        r = lax.rsqrt(jnp.mean(xv * xv, axis=-1, keepdims=True) + NORM_EPS)
        xh = xv * r
        dh = dv * g_ref[...]
        dxv = r * (dh - xh * jnp.mean(dh * xh, axis=-1, keepdims=True))
        if resid is not None:
            dxv = refs[3][...] + dxv
        dx_ref[...] = dxv.astype(dx_ref.dtype)

        @pl.when(pl.program_id(0) == 0)
        def _():
            dg_ref[...] = jnp.zeros_like(dg_ref)

        dg_ref[...] += jnp.sum(dv * xh, axis=0, keepdims=True)

    row = pl.BlockSpec((tr, D), lambda i: (i, 0))
    vec = pl.BlockSpec((1, D), lambda i: (0, 0))
    ins = [xin, dout, g] + ([resid] if resid is not None else [])
    return pl.pallas_call(
        body, name=name,
        out_shape=[jax.ShapeDtypeStruct((T, D), out_dtype), jax.ShapeDtypeStruct((1, D), F32)],
        grid=(T // tr,), in_specs=[row, row, vec] + ([row] if resid is not None else []),
        out_specs=[row, vec], compiler_params=_cp("arbitrary"),
    )(*ins)


def _post_fwd(y, g, resid, name, next_gain=None, target=None):
    T, D = y.shape
    tr = _tile(T, ROW_TILE, SUBLANE)

    def body(y_ref, g_ref, r_ref, e_ref, o1_ref, o2_ref):
        yv = y_ref[...]
        r = lax.rsqrt(jnp.mean(yv * yv, axis=-1, keepdims=True) + NORM_EPS)
        xv = r_ref[...] + yv * r * g_ref[...]
        if target is None:
            o1_ref[...] = xv
            r2 = lax.rsqrt(jnp.mean(xv * xv, axis=-1, keepdims=True) + NORM_EPS)
            o2_ref[...] = (xv * r2 * e_ref[...]).astype(o2_ref.dtype)
        else:
            e = xv - e_ref[...]
            o1_ref[...] = e * (1.0 / D)

            @pl.when(pl.program_id(0) == 0)
            def _():
                o2_ref[...] = jnp.zeros_like(o2_ref)

            rows = jnp.sum(e * e, axis=-1, keepdims=True) * (1.0 / D)
            o2_ref[...] += jnp.broadcast_to(0.5 * jnp.sum(rows, axis=0, keepdims=True), o2_ref.shape)

    row = pl.BlockSpec((tr, D), lambda i: (i, 0))
    vec = pl.BlockSpec((1, D), lambda i: (0, 0))
    if target is None:
        extra, e_spec = next_gain, vec
        out2, o2_spec = jax.ShapeDtypeStruct((T, D), MXU), row
    else:
        extra, e_spec = target, row
        out2, o2_spec = jax.ShapeDtypeStruct((1, LANE), F32), pl.BlockSpec((1, LANE), lambda i: (0, 0))
    return pl.pallas_call(
        body, name=name, out_shape=[jax.ShapeDtypeStruct((T, D), F32), out2],
        grid=(T // tr,), in_specs=[row, vec, row, e_spec], out_specs=[row, o2_spec],
        compiler_params=_cp("arbitrary" if target is not None else "parallel"),
    )(y, g, resid, extra)


def _mla_norms_fwd(P, gq, gkv, qr, kvr, name):
    T = P.shape[0]
    tr = _tile(T, ROW_TILE, SUBLANE)

    def body(cq_ref, ckv_ref, gq_ref, gkv_ref, oq_ref, okv_ref):
        for x_ref, g_ref, o_ref in ((cq_ref, gq_ref, oq_ref), (ckv_ref, gkv_ref, okv_ref)):
            xv = x_ref[...].astype(F32)
            r = lax.rsqrt(jnp.mean(xv * xv, axis=-1, keepdims=True) + NORM_EPS)
            o_ref[...] = (xv * r * g_ref[...]).astype(o_ref.dtype)

    return pl.pallas_call(
        body, name=name,
        out_shape=[jax.ShapeDtypeStruct((T, qr), MXU), jax.ShapeDtypeStruct((T, kvr), MXU)],
        grid=(T // tr,),
        in_specs=[pl.BlockSpec((tr, qr), lambda i: (i, 0)), pl.BlockSpec((tr, kvr), lambda i: (i, qr // kvr)),
                  pl.BlockSpec((1, qr), lambda i: (0, 0)), pl.BlockSpec((1, kvr), lambda i: (0, 0))],
        out_specs=[pl.BlockSpec((tr, qr), lambda i: (i, 0)), pl.BlockSpec((tr, kvr), lambda i: (i, 0))],
        compiler_params=_cp("parallel"),
    )(P, P, gq, gkv)


def _mla_norms_bwd(P, dcqn, dckvn_k, dckvn_v, dkr, gq, gkv, qr, kvr, pa, name):
    T = P.shape[0]
    tr = _tile(T, ROW_TILE, SUBLANE)
    c2 = qr + kvr

    def body(cq_ref, ckv_ref, dq_ref, dk_ref, dv_ref, dkr_ref, gq_ref, gkv_ref, dp_ref, dgq_ref, dgkv_ref):
        @pl.when(pl.program_id(0) == 0)
        def _():
            dgq_ref[...] = jnp.zeros_like(dgq_ref)
            dgkv_ref[...] = jnp.zeros_like(dgkv_ref)

        def one(x_ref, dv, g_ref, dg_ref):
            xv = x_ref[...].astype(F32)
            r = lax.rsqrt(jnp.mean(xv * xv, axis=-1, keepdims=True) + NORM_EPS)
            xh = xv * r
            dh = dv * g_ref[...]
            dg_ref[...] += jnp.sum(dv * xh, axis=0, keepdims=True)
            return r * (dh - xh * jnp.mean(dh * xh, axis=-1, keepdims=True))

        dp_ref[:, 0:qr] = one(cq_ref, dq_ref[...], gq_ref, dgq_ref).astype(dp_ref.dtype)
        dp_ref[:, qr:c2] = one(ckv_ref, dk_ref[...] + dv_ref[...], gkv_ref, dgkv_ref).astype(dp_ref.dtype)
        dp_ref[:, c2:c2 + LANE] = dkr_ref[...].astype(dp_ref.dtype)
        if pa > c2 + LANE:
            dp_ref[:, c2 + LANE:pa] = jnp.zeros((tr, pa - c2 - LANE), dp_ref.dtype)

    return pl.pallas_call(
        body, name=name,
        out_shape=[jax.ShapeDtypeStruct((T, pa), MXU), jax.ShapeDtypeStruct((1, qr), F32),
                   jax.ShapeDtypeStruct((1, kvr), F32)],
        grid=(T // tr,),
        in_specs=[pl.BlockSpec((tr, qr), lambda i: (i, 0)), pl.BlockSpec((tr, kvr), lambda i: (i, qr // kvr)),
                  pl.BlockSpec((tr, qr), lambda i: (i, 0)), pl.BlockSpec((tr, kvr), lambda i: (i, 0)),
                  pl.BlockSpec((tr, kvr), lambda i: (i, 0)), pl.BlockSpec((tr, LANE), lambda i: (i, 0)),
                  pl.BlockSpec((1, qr), lambda i: (0, 0)), pl.BlockSpec((1, kvr), lambda i: (0, 0))],
        out_specs=[pl.BlockSpec((tr, pa), lambda i: (i, 0)), pl.BlockSpec((1, qr), lambda i: (0, 0)),
                   pl.BlockSpec((1, kvr), lambda i: (0, 0))],
        compiler_params=_cp("arbitrary"),
    )(P, P, dcqn, dckvn_k, dckvn_v, dkr, gq, gkv)


def _rope_tables(pos_ref, invf_ref):
    ang = pos_ref[...].astype(F32) * invf_ref[...]
    lane = lax.broadcasted_iota(jnp.int32, ang.shape, 1)
    cos, sin = jnp.cos(ang), jnp.sin(ang)
    half = ROPE // 2
    c = jnp.where(lane < ROPE, cos, 0.0)
    s1 = jnp.where(lane < half, -sin, 0.0)
    s2 = jnp.where((lane >= half) & (lane < ROPE), sin, 0.0)
    return c, s1, s2


def _rope_fwd(qpad, kpad, v, P, pos, invf, kr_blk, H, name):
    T = qpad.shape[0]
    tr = _tile(T, ROW_TILE, SUBLANE)

    def body(q_ref, k_ref, v_ref, kr_ref, pos_ref, invf_ref, qo_ref, ko_ref, vo_ref):
        c, s1, s2 = _rope_tables(pos_ref, invf_ref)

        def rot(t):
            return t * c + pltpu.roll(t, LANE - ROPE // 2, 1) * s1 + pltpu.roll(t, ROPE // 2, 1) * s2

        kr = rot(kr_ref[...].astype(F32)).astype(ko_ref.dtype)
        ones = jnp.ones((tr, QPAD - VHEAD), vo_ref.dtype)
        for h in range(H):
            lo = h * QPAD
            qo_ref[:, lo:lo + NOPE] = q_ref[:, lo:lo + NOPE]
            qo_ref[:, lo + NOPE:lo + QPAD] = rot(q_ref[:, lo + NOPE:lo + QPAD].astype(F32)).astype(qo_ref.dtype)
            ko_ref[:, lo:lo + NOPE] = k_ref[:, lo:lo + NOPE]
            ko_ref[:, lo + NOPE:lo + QPAD] = kr
            vo_ref[:, lo:lo + VHEAD] = v_ref[:, h * VHEAD:(h + 1) * VHEAD]
            vo_ref[:, lo + VHEAD:lo + QPAD] = ones

    wide = pl.BlockSpec((tr, H * QPAD), lambda i: (i, 0))
    return pl.pallas_call(
        body, name=name,
        out_shape=[jax.ShapeDtypeStruct(qpad.shape, MXU)] * 3,
        grid=(T // tr,),
        in_specs=[wide, wide, pl.BlockSpec((tr, H * VHEAD), lambda i: (i, 0)),
                  pl.BlockSpec((tr, LANE), lambda i: (i, kr_blk)),
                  pl.BlockSpec((tr, 1), lambda i: (i, 0)), pl.BlockSpec((1, LANE), lambda i: (0, 0))],
        out_specs=[wide, wide, wide], compiler_params=_cp("parallel"),
    )(qpad, kpad, v, P, pos, invf)


def _rope_bwd(dqcat, dkcat, pos, invf, H, name):
    T = dqcat.shape[0]
    tr = _tile(T, ROW_TILE, SUBLANE)

    def body(dq_ref, dk_ref, pos_ref, invf_ref, dqo_ref, dkr_ref):
        c, s1, s2 = _rope_tables(pos_ref, invf_ref)

        def rot_t(v):
            return v * c + pltpu.roll(v * s1, ROPE // 2, 1) + pltpu.roll(v * s2, LANE - ROPE // 2, 1)

        acc = jnp.zeros((tr, LANE), F32)
        for h in range(H):
            lo = h * QPAD
            dqo_ref[:, lo:lo + NOPE] = dq_ref[:, lo:lo + NOPE]
            dqo_ref[:, lo + NOPE:lo + QPAD] = rot_t(dq_ref[:, lo + NOPE:lo + QPAD].astype(F32)).astype(dqo_ref.dtype)
            acc = acc + dk_ref[:, lo + NOPE:lo + QPAD].astype(F32)
        dkr_ref[...] = rot_t(acc)

    wide = pl.BlockSpec((tr, H * QPAD), lambda i: (i, 0))
    return pl.pallas_call(
        body, name=name,
        out_shape=[jax.ShapeDtypeStruct(dqcat.shape, MXU), jax.ShapeDtypeStruct((T, LANE), F32)],
        grid=(T // tr,),
        in_specs=[wide, wide, pl.BlockSpec((tr, 1), lambda i: (i, 0)), pl.BlockSpec((1, LANE), lambda i: (0, 0))],
        out_specs=[wide, pl.BlockSpec((tr, LANE), lambda i: (i, 0))], compiler_params=_cp("parallel"),
    )(dqcat, dkcat, pos, invf)


def _causal_mask(s):
    row = lax.broadcasted_iota(jnp.int32, s.shape, 0)
    col = lax.broadcasted_iota(jnp.int32, s.shape, 1)
    return jnp.where(col <= row, s, -1e30)


def _flash_fwd(qcat, kcat, v, P, z_blk, H, name, hosts=None):
    T = qcat.shape[0]
    tq = _tile(T, ATT_TILE, LANE)
    hp = 2 if H % 2 == 0 and z_blk % 2 == 0 else 1
    c2 = ATT_SCALE * LOG2E

    def body(q_ref, k_ref, v_ref, z_ref, o_ref, a_ref, lse_ref, m_s, acc_s):
        i = pl.program_id(1)
        m_s[...] = jnp.full_like(m_s, -1e30)
        acc_s[...] = jnp.zeros_like(acc_s)

        def step(j, masked):
            r0 = pl.multiple_of(j * tq, tq)
            for h in range(hp):
                q = q_ref[:, h * QPAD:(h + 1) * QPAD]
                kb = k_ref[pl.ds(r0, tq), h * QPAD:(h + 1) * QPAD]
                vb = v_ref[pl.ds(r0, tq), h * QPAD:(h + 1) * QPAD]
                s = lax.dot_general(q, kb, NT, preferred_element_type=F32)
                if masked:
                    s = _causal_mask(s)
                m_prev = m_s[h]
                m_new = jnp.maximum(m_prev, jnp.max(s, axis=1, keepdims=True))
                p = jnp.exp2((s - m_new) * c2).astype(MXU)
                alpha = jnp.exp2((m_prev - m_new) * c2)
                acc_s[h] = alpha * acc_s[h] + lax.dot_general(p, vb, NN, preferred_element_type=F32)
                m_s[h] = m_new

        def loop(j, carry):
            step(j, False)
            return carry

        lax.fori_loop(0, i, loop, 0)
        step(i, True)
        for h in range(hp):
            l = acc_s[h, :, VHEAD:QPAD]
            o = acc_s[h, :, 0:VHEAD] / l
            cols = slice(h * VHEAD, (h + 1) * VHEAD)
            o_ref[:, cols] = o.astype(o_ref.dtype)
            a_ref[:, cols] = (o * _silu(z_ref[:, cols].astype(F32))).astype(a_ref.dtype)
            lse_ref[h] = m_s[h] * ATT_SCALE + jnp.log(l)

    return _hosted(
        hosts, ATT_FWD_US * (T / 4096.0) ** 2 * (H / 32.0), body, name=name,
        out_shape=[jax.ShapeDtypeStruct((T, H * VHEAD), MXU), jax.ShapeDtypeStruct((T, H * VHEAD), MXU),
                   jax.ShapeDtypeStruct((H, T, LANE), F32)],
        grid=(H // hp, T // tq),
        in_specs=[pl.BlockSpec((tq, hp * QPAD), lambda h, i: (i, h)), pl.BlockSpec((T, hp * QPAD), lambda h, i: (0, h)),
                  pl.BlockSpec((T, hp * QPAD), lambda h, i: (0, h)),
                  pl.BlockSpec((tq, hp * VHEAD), lambda h, i: (i, z_blk // hp + h))],
        out_specs=[pl.BlockSpec((tq, hp * VHEAD), lambda h, i: (i, h)), pl.BlockSpec((tq, hp * VHEAD), lambda h, i: (i, h)),
                   pl.BlockSpec((hp, tq, LANE), lambda h, i: (h, i, 0))],
        scratch_shapes=[pltpu.VMEM((hp, tq, 1), F32), pltpu.VMEM((hp, tq, QPAD), F32)],
        args=[qcat, kcat, v, P], semantics=("parallel", "arbitrary"))


def _flash_bwd(qcat, kcat, v, do, lse, delta, H, name, hosts=None):
    T = qcat.shape[0]
    tq = _tile(T, ATT_TILE, LANE)
    nq = T // tq
    c2 = ATT_SCALE * LOG2E

    def body(q_ref, do_ref, lse_ref, dl_ref, k_ref, v_ref, dq_ref, dk_ref, dv_ref, dq_s, dk_s, dv_s):
        j = pl.program_id(1)
        kb, vb = k_ref[...], v_ref[...]
        dk_s[...] = jnp.zeros_like(dk_s)
        dv_s[...] = jnp.zeros_like(dv_s)

        @pl.when(j == 0)
        def _():
            dq_s[...] = jnp.zeros_like(dq_s)

        def step(i, masked):
            rows = pl.ds(pl.multiple_of(i * tq, tq), tq)
            qb = q_ref[rows, :]
            dob = do_ref[rows, :]
            s = lax.dot_general(qb, kb, NT, preferred_element_type=F32)
            if masked:
                s = _causal_mask(s)
            p = jnp.exp2(s * c2 - lse_ref[rows, 0:1] * LOG2E)
            dv_s[...] += lax.dot_general(p.astype(dob.dtype), dob, TN, preferred_element_type=F32)
            dp = lax.dot_general(dob, vb, NT, preferred_element_type=F32)
            ds = (p * (dp - dl_ref[rows, 0:1]) * ATT_SCALE).astype(qb.dtype)
            dk_s[...] += lax.dot_general(ds, qb, TN, preferred_element_type=F32)
            dq_s[rows, :] += lax.dot_general(ds, kb, NN, preferred_element_type=F32)

        def loop(i, carry):
            step(i, False)
            return carry

        step(j, True)
        lax.fori_loop(j + 1, nq, loop, 0)
        dk_ref[...] = dk_s[...].astype(dk_ref.dtype)
        dv_ref[...] = dv_s[...].astype(dv_ref.dtype)

        @pl.when(j == nq - 1)
        def _():
            dq_ref[...] = dq_s[...].astype(dq_ref.dtype)

    return _hosted(
        hosts, ATT_BWD_US * (T / 4096.0) ** 2 * (H / 32.0), body, name=name,
        out_shape=[jax.ShapeDtypeStruct(qcat.shape, MXU), jax.ShapeDtypeStruct(kcat.shape, MXU),
                   jax.ShapeDtypeStruct(v.shape, MXU)],
        grid=(H, nq),
        in_specs=[pl.BlockSpec((T, QPAD), lambda h, j: (0, h)), pl.BlockSpec((T, VHEAD), lambda h, j: (0, h)),
                  pl.BlockSpec((None, T, LANE), lambda h, j: (h, 0, 0)), pl.BlockSpec((None, T, LANE), lambda h, j: (h, 0, 0)),
                  pl.BlockSpec((tq, QPAD), lambda h, j: (j, h)), pl.BlockSpec((tq, VHEAD), lambda h, j: (j, h))],
        out_specs=[pl.BlockSpec((T, QPAD), lambda h, j: (0, h)), pl.BlockSpec((tq, QPAD), lambda h, j: (j, h)),
                   pl.BlockSpec((tq, VHEAD), lambda h, j: (j, h))],
        scratch_shapes=[pltpu.VMEM((T, QPAD), F32), pltpu.VMEM((tq, QPAD), F32), pltpu.VMEM((tq, VHEAD), F32)],
        args=[qcat, do, lse, delta, kcat, v], semantics=("parallel", "arbitrary"))


def _mla_gate_bwd(da, o, P, pa, H, name):
    T, HV = da.shape
    tr = _tile(T, ROW_TILE, SUBLANE)
    cw = _tile(math.gcd(pa, HV), 512, LANE)
    hb = cw // VHEAD

    def body(da_ref, o_ref, z_ref, do_ref, dz_ref, dl_ref):
        dav, ov, zv = da_ref[...].astype(F32), o_ref[...].astype(F32), z_ref[...].astype(F32)
        dov = dav * _silu(zv)
        do_ref[...] = dov.astype(do_ref.dtype)
        dz_ref[...] = (dav * ov * _silu_grad(zv)).astype(dz_ref.dtype)
        prod = dov * ov
        for h in range(hb):
            dl_ref[h] = jnp.broadcast_to(jnp.sum(prod[:, h * VHEAD:(h + 1) * VHEAD], axis=1, keepdims=True), (tr, LANE))

    blk = pl.BlockSpec((tr, cw), lambda i, j: (i, j))
    shifted = pl.BlockSpec((tr, cw), lambda i, j: (i, pa // cw + j))
    return pl.pallas_call(
        body, name=name,
        out_shape=[jax.ShapeDtypeStruct((T, HV), MXU), jax.ShapeDtypeStruct((T, HV), MXU),
                   jax.ShapeDtypeStruct((H, T, LANE), F32)],
        grid=(T // tr, HV // cw),
        in_specs=[blk, blk, shifted],
        out_specs=[blk, blk, pl.BlockSpec((hb, tr, LANE), lambda i, j: (j, i, 0))],
        compiler_params=_cp("parallel", "parallel"),
    )(da, o, P)


def _bank_rows(width):
    return min(SUBLANE, width), -(-(width - 1) // SUBLANE) * SUBLANE


def _bank_fill(bank_ref, val, width, T, causal):
    nr, hp = _bank_rows(width)
    rows = lax.broadcasted_iota(jnp.int32, val.shape, 0)
    zero = jnp.zeros((hp,) + val.shape[1:], F32)
    for r in range(nr):
        if causal:
            bank_ref[r, 0:hp, :] = zero
            bank_ref[r, hp:hp + T, :] = val if r == 0 else jnp.where(rows >= r, pltpu.roll(val, r, 0), 0.0)
        else:
            bank_ref[r, T:T + hp, :] = zero
            bank_ref[r, 0:T, :] = val if r == 0 else jnp.where(rows < T - r, pltpu.roll(val, T - r, 0), 0.0)


def _bank_tap(bank_ref, s, t0, tc, width, causal):
    _, hp = _bank_rows(width)
    q, r = divmod(s, SUBLANE)
    off = hp - SUBLANE * q if causal else SUBLANE * q
    return bank_ref[r, pl.ds(pl.multiple_of(t0 + off, SUBLANE), tc), :]


def _conv_chunk(bank_ref, w_ref, t0, tc, width):
    acc = None
    for s in range(width):
        k = width - 1 - s
        term = w_ref[k:k + 1, :] * _bank_tap(bank_ref, s, t0, tc, width, True)
        acc = term if acc is None else acc + term
    return acc


def _conv_bwd_chunk(bank_ref, w_ref, xin, dw_ref, t0, tc, width):
    acc = None
    for s in range(width):
        k = width - 1 - s
        tap = _bank_tap(bank_ref, s, t0, tc, width, False)
        term = w_ref[k:k + 1, :] * tap
        acc = term if acc is None else acc + term
        dw_ref[k] += jnp.sum((tap * xin).reshape(tc // SUBLANE, SUBLANE, xin.shape[1]), axis=0)
    return acc


def _full_t(T, cb, col):
    return pl.BlockSpec((T, cb), lambda j, col=col: (0, col + j))


def _sc_fwd(P, w, D, name):
    T = P.shape[0]
    cb = _tile(D, CONV_CB, LANE)
    nb = D // cb
    width = w.shape[0]
    tc = _tile(T, CONV_CHUNK, SUBLANE)
    nr, hp = _bank_rows(width)

    def body(b_ref, c_ref, u_ref, z_ref, w_ref, a_ref, bank):
        _bank_fill(bank, c_ref[...].astype(F32) * u_ref[...].astype(F32), width, T, True)

        def chunk(ci, carry):
            t0 = pl.multiple_of(ci * tc, tc)
            rows = pl.ds(t0, tc)
            v = _conv_chunk(bank, w_ref, t0, tc, width)
            a_ref[rows, :] = (b_ref[rows, :].astype(F32) * v * _silu(z_ref[rows, :].astype(F32))).astype(a_ref.dtype)
            return carry

        lax.fori_loop(0, T // tc, chunk, 0)

    return pl.pallas_call(
        body, name=name, out_shape=jax.ShapeDtypeStruct((T, D), MXU), grid=(nb,),
        in_specs=[_full_t(T, cb, 0), _full_t(T, cb, nb), _full_t(T, cb, 2 * nb), _full_t(T, cb, 3 * nb),
                  pl.BlockSpec((width, cb), lambda j: (0, j))],
        out_specs=_full_t(T, cb, 0),
        scratch_shapes=[pltpu.VMEM((nr, T + hp, cb), F32)],
        compiler_params=_cp("parallel"),
    )(P, P, P, P, w)


def _sc_bwd(P, da, w, D, name):
    T = P.shape[0]
    cb = _tile(D, CONV_CB, LANE)
    nb = D // cb
    width = w.shape[0]
    tc = _tile(T, CONV_CHUNK, SUBLANE)
    nr, hp = _bank_rows(width)

    def body(b_ref, c_ref, u_ref, z_ref, da_ref, w_ref, dp_ref, dw_ref, bank, dv_s, dw_s):
        _bank_fill(bank, c_ref[...].astype(F32) * u_ref[...].astype(F32), width, T, True)

        def first(ci, carry):
            t0 = pl.multiple_of(ci * tc, tc)
            rows = pl.ds(t0, tc)
            v = _conv_chunk(bank, w_ref, t0, tc, width)
            bv, zv, dav = b_ref[rows, :].astype(F32), z_ref[rows, :].astype(F32), da_ref[rows, :].astype(F32)
            dyb = dav * _silu(zv)
            dp_ref[3, rows, :] = (dav * bv * v * _silu_grad(zv)).astype(dp_ref.dtype)
            dp_ref[0, rows, :] = (dyb * v).astype(dp_ref.dtype)
            dv_s[rows, :] = dyb * bv
            return carry

        lax.fori_loop(0, T // tc, first, 0)
        _bank_fill(bank, dv_s[...], width, T, False)
        dw_s[...] = jnp.zeros_like(dw_s)

        def second(ci, carry):
            t0 = pl.multiple_of(ci * tc, tc)
            rows = pl.ds(t0, tc)
            cv, uv = c_ref[rows, :].astype(F32), u_ref[rows, :].astype(F32)
            dcu = _conv_bwd_chunk(bank, w_ref, cv * uv, dw_s, t0, tc, width)
            dp_ref[1, rows, :] = (dcu * uv).astype(dp_ref.dtype)
            dp_ref[2, rows, :] = (dcu * cv).astype(dp_ref.dtype)
            return carry

        lax.fori_loop(0, T // tc, second, 0)
        dw_ref[...] = jnp.sum(dw_s[...], axis=1)

    return pl.pallas_call(
        body, name=name,
        out_shape=[jax.ShapeDtypeStruct((4, T, D), MXU), jax.ShapeDtypeStruct((width, D), F32)],
        grid=(nb,),
        in_specs=[_full_t(T, cb, 0), _full_t(T, cb, nb), _full_t(T, cb, 2 * nb), _full_t(T, cb, 3 * nb),
                  _full_t(T, cb, 0), pl.BlockSpec((width, cb), lambda j: (0, j))],
        out_specs=[pl.BlockSpec((4, T, cb), lambda j: (0, 0, j)), pl.BlockSpec((width, cb), lambda j: (0, j))],
        scratch_shapes=[pltpu.VMEM((nr, T + hp, cb), F32), pltpu.VMEM((T, cb), F32),
                        pltpu.VMEM((width, SUBLANE, cb), F32)],
        compiler_params=_cp("parallel"),
    )(P, P, P, P, da, w)


def _cf_conv_fwd(P, w, bias, D, name):
    T = P.shape[0]
    cb = _tile(D, CONV_CB, LANE)
    nb = D // cb
    width = w.shape[0]
    tc = _tile(T, CONV_CHUNK, SUBLANE)
    nr, hp = _bank_rows(width)

    def body(a_ref, g_ref, w_ref, b_ref, y_ref, bank):
        _bank_fill(bank, a_ref[...].astype(F32) * _sig(g_ref[...].astype(F32)), width, T, True)

        def chunk(ci, carry):
            t0 = pl.multiple_of(ci * tc, tc)
            y_ref[pl.ds(t0, tc), :] = (_conv_chunk(bank, w_ref, t0, tc, width) + b_ref[...]).astype(y_ref.dtype)
            return carry

        lax.fori_loop(0, T // tc, chunk, 0)

    return pl.pallas_call(
        body, name=name, out_shape=jax.ShapeDtypeStruct((T, D), F32), grid=(nb,),
        in_specs=[_full_t(T, cb, 0), _full_t(T, cb, nb), pl.BlockSpec((width, cb), lambda j: (0, j)),
                  pl.BlockSpec((1, cb), lambda j: (0, j))],
        out_specs=_full_t(T, cb, 0),
        scratch_shapes=[pltpu.VMEM((nr, T + hp, cb), F32)],
        compiler_params=_cp("parallel"),
    )(P, P, w, bias)


def _cf_conv_bwd(P, dyc, w, D, name):
    T = P.shape[0]
    cb = _tile(D, CONV_CB, LANE)
    nb = D // cb
    width = w.shape[0]
    tc = _tile(T, CONV_CHUNK, SUBLANE)
    nr, hp = _bank_rows(width)

    def body(a_ref, g_ref, dy_ref, w_ref, da_ref, dg_ref, dw_ref, db_ref, bank, dw_s):
        dyv = dy_ref[...]
        db_ref[...] = jnp.sum(dyv, axis=0, keepdims=True)
        _bank_fill(bank, dyv, width, T, False)
        dw_s[...] = jnp.zeros_like(dw_s)

        def chunk(ci, carry):
            t0 = pl.multiple_of(ci * tc, tc)
            rows = pl.ds(t0, tc)
            av, sg = a_ref[rows, :].astype(F32), _sig(g_ref[rows, :].astype(F32))
            dyg = _conv_bwd_chunk(bank, w_ref, av * sg, dw_s, t0, tc, width)
            da_ref[rows, :] = (dyg * sg).astype(da_ref.dtype)
            dg_ref[rows, :] = (dyg * av * sg * (1.0 - sg)).astype(dg_ref.dtype)
            return carry

        lax.fori_loop(0, T // tc, chunk, 0)
        dw_ref[...] = jnp.sum(dw_s[...], axis=1)

    return pl.pallas_call(
        body, name=name,
        out_shape=[jax.ShapeDtypeStruct((T, D), MXU), jax.ShapeDtypeStruct((T, D), MXU),
                   jax.ShapeDtypeStruct((width, D), F32), jax.ShapeDtypeStruct((1, D), F32)],
        grid=(nb,),
        in_specs=[_full_t(T, cb, 0), _full_t(T, cb, nb), _full_t(T, cb, 0),
                  pl.BlockSpec((width, cb), lambda j: (0, j))],
        out_specs=[_full_t(T, cb, 0), _full_t(T, cb, 0), pl.BlockSpec((width, cb), lambda j: (0, j)),
                   pl.BlockSpec((1, cb), lambda j: (0, j))],
        scratch_shapes=[pltpu.VMEM((nr, T + hp, cb), F32), pltpu.VMEM((width, SUBLANE, cb), F32)],
        compiler_params=_cp("parallel"),
    )(P, P, dyc, w)


def _layer_norm_stats(v):
    mu = jnp.mean(v, axis=-1, keepdims=True)
    cen = v - mu
    rstd = lax.rsqrt(jnp.mean(cen * cen, axis=-1, keepdims=True) + LN_EPS)
    return cen * rstd, rstd


def _layer_norm_bwd(dxh, xh, rstd):
    return rstd * (dxh - jnp.mean(dxh, axis=-1, keepdims=True) - xh * jnp.mean(dxh * xh, axis=-1, keepdims=True))


def _cf_act_fwd(yc, P, lg, lb, D, name):
    T = yc.shape[0]
    tr = _tile(T, ROW_TILE, SUBLANE)

    def body(y_ref, z_ref, g_ref, b_ref, o_ref):
        xh, _ = _layer_norm_stats(y_ref[...])
        yl = xh * g_ref[...] + b_ref[...]
        o_ref[...] = (_silu(yl) * _silu(z_ref[...].astype(F32))).astype(o_ref.dtype)

    row = pl.BlockSpec((tr, D), lambda i: (i, 0))
    vec = pl.BlockSpec((1, D), lambda i: (0, 0))
    return pl.pallas_call(
        body, name=name, out_shape=jax.ShapeDtypeStruct((T, D), MXU), grid=(T // tr,),
        in_specs=[row, pl.BlockSpec((tr, D), lambda i: (i, 2)), vec, vec], out_specs=row,
        compiler_params=_cp("parallel"),
    )(yc, P, lg, lb)


def _cf_act_bwd(yc, P, da, lg, lb, D, name):
    T = yc.shape[0]
    tr = _tile(T, ROW_TILE, SUBLANE)

    def body(y_ref, z_ref, da_ref, g_ref, b_ref, dy_ref, dz_ref, dg_ref, db_ref):
        @pl.when(pl.program_id(0) == 0)
        def _():
            dg_ref[...] = jnp.zeros_like(dg_ref)
            db_ref[...] = jnp.zeros_like(db_ref)

        xh, rstd = _layer_norm_stats(y_ref[...])
        yl = xh * g_ref[...] + b_ref[...]
        zv, dav = z_ref[...].astype(F32), da_ref[...].astype(F32)
        dz_ref[...] = (dav * _silu(yl) * _silu_grad(zv)).astype(dz_ref.dtype)
        dyl = dav * _silu(zv) * _silu_grad(yl)
        dg_ref[...] += jnp.sum(dyl * xh, axis=0, keepdims=True)
        db_ref[...] += jnp.sum(dyl, axis=0, keepdims=True)
        dy_ref[...] = _layer_norm_bwd(dyl * g_ref[...], xh, rstd)

    row = pl.BlockSpec((tr, D), lambda i: (i, 0))
    zcol = pl.BlockSpec((tr, D), lambda i: (i, 2))
    vec = pl.BlockSpec((1, D), lambda i: (0, 0))
    return pl.pallas_call(
        body, name=name,
        out_shape=[jax.ShapeDtypeStruct((T, D), F32), jax.ShapeDtypeStruct((T, D), MXU),
                   jax.ShapeDtypeStruct((1, D), F32), jax.ShapeDtypeStruct((1, D), F32)],
        grid=(T // tr,), in_specs=[row, zcol, row, vec, vec], out_specs=[row, row, vec, vec],
        compiler_params=_cp("arbitrary"),
    )(yc, P, da, lg, lb)


def _tril(w):
    row = lax.broadcasted_iota(jnp.int32, w.shape, 0)
    col = lax.broadcasted_iota(jnp.int32, w.shape, 1)
    return jnp.where(col <= row, w, 0.0)


def _gm_fwd(P, lg, lb, ws, bst, D, name):
    T = P.shape[0]
    G, ch, _ = ws.shape
    gw = D // G

    def body(p_ref, g_ref, b_ref, ws_ref, bs_ref, a_ref):
        uv, vv, zv = (p_ref[:, k * D:(k + 1) * D].astype(F32) for k in range(3))
        xh, _ = _layer_norm_stats(_gelu(vv))
        vn = (xh * g_ref[...] + b_ref[...]).astype(MXU)
        gate = _gelu(uv) * _silu(zv)
        for g in range(G):
            cols = slice(g * gw, (g + 1) * gw)
            s = lax.dot_general(_tril(ws_ref[g]).astype(MXU), vn[:, cols], NN, preferred_element_type=F32)
            a_ref[:, cols] = (gate[:, cols] * (s + bs_ref[:, g:g + 1])).astype(a_ref.dtype)

    vec = pl.BlockSpec((1, D), lambda i: (0, 0))
    return pl.pallas_call(
        body, name=name, out_shape=jax.ShapeDtypeStruct((T, D), MXU), grid=(T // ch,),
        in_specs=[pl.BlockSpec((ch, 3 * D), lambda i: (i, 0)), vec, vec,
                  pl.BlockSpec((G, ch, ch), lambda i: (0, 0, 0)), pl.BlockSpec((ch, G), lambda i: (0, 0))],
        out_specs=pl.BlockSpec((ch, D), lambda i: (i, 0)), compiler_params=_cp("parallel"),
    )(P, lg, lb, ws, bst)


def _gm_bwd(P, da, lg, lb, ws, bst, D, name):
    T = P.shape[0]
    G, ch, _ = ws.shape
    gw = D // G

    def body(p_ref, da_ref, g_ref, b_ref, ws_ref, bs_ref, dp_ref, dg_ref, db_ref, dws_ref, dbs_ref, dvn_s):
        @pl.when(pl.program_id(0) == 0)
        def _():
            dg_ref[...] = jnp.zeros_like(dg_ref)
            db_ref[...] = jnp.zeros_like(db_ref)
            dws_ref[...] = jnp.zeros_like(dws_ref)
            dbs_ref[...] = jnp.zeros_like(dbs_ref)

        uv, vv, zv = (p_ref[:, k * D:(k + 1) * D].astype(F32) for k in range(3))
        dav = da_ref[...].astype(F32)
        xh, rstd = _layer_norm_stats(_gelu(vv))
        vn = (xh * g_ref[...] + b_ref[...]).astype(MXU)
        ug, sz = _gelu(uv), _silu(zv)
        ds_all = dav * sz * ug
        for g in range(G):
            cols = slice(g * gw, (g + 1) * gw)
            wm = _tril(ws_ref[g]).astype(MXU)
            s = lax.dot_general(wm, vn[:, cols], NN, preferred_element_type=F32) + bs_ref[:, g:g + 1]
            dp_ref[:, g * gw:(g + 1) * gw] = (dav[:, cols] * sz[:, cols] * s * _gelu_grad(uv[:, cols])).astype(dp_ref.dtype)
            dp_ref[:, 2 * D + g * gw:2 * D + (g + 1) * gw] = (
                dav[:, cols] * ug[:, cols] * s * _silu_grad(zv[:, cols])).astype(dp_ref.dtype)
            ds = ds_all[:, cols]
            dsb = ds.astype(MXU)
            dvn_s[:, cols] = lax.dot_general(wm, dsb, TN, preferred_element_type=F32)
            dws_ref[g] += _tril(lax.dot_general(dsb, vn[:, cols], NT, preferred_element_type=F32))
            dbs_ref[g] += jnp.broadcast_to(jnp.sum(ds, axis=1, keepdims=True), (ch, LANE))
        dvn = dvn_s[...]
        dg_ref[...] += jnp.sum(dvn * xh, axis=0, keepdims=True)
        db_ref[...] += jnp.sum(dvn, axis=0, keepdims=True)
        dp_ref[:, D:2 * D] = (_layer_norm_bwd(dvn * g_ref[...], xh, rstd) * _gelu_grad(vv)).astype(dp_ref.dtype)

    vec = pl.BlockSpec((1, D), lambda i: (0, 0))
    return pl.pallas_call(
        body, name=name,
        out_shape=[jax.ShapeDtypeStruct((T, 3 * D), MXU), jax.ShapeDtypeStruct((1, D), F32), jax.ShapeDtypeStruct((1, D), F32),
                   jax.ShapeDtypeStruct((G, ch, ch), F32), jax.ShapeDtypeStruct((G, ch, LANE), F32)],
        grid=(T // ch,),
        in_specs=[pl.BlockSpec((ch, 3 * D), lambda i: (i, 0)), pl.BlockSpec((ch, D), lambda i: (i, 0)), vec, vec,
                  pl.BlockSpec((G, ch, ch), lambda i: (0, 0, 0)), pl.BlockSpec((ch, G), lambda i: (0, 0))],
        out_specs=[pl.BlockSpec((ch, 3 * D), lambda i: (i, 0)), vec, vec,
                   pl.BlockSpec((G, ch, ch), lambda i: (0, 0, 0)), pl.BlockSpec((G, ch, LANE), lambda i: (0, 0, 0))],
        scratch_shapes=[pltpu.VMEM((ch, D), F32)],
        compiler_params=_cp("arbitrary"),
    )(P, da, lg, lb, ws, bst)


def _adam_math(w, g, m, v):
    m = ADAM_B1 * m + (1.0 - ADAM_B1) * g
    v = ADAM_B2 * v + (1.0 - ADAM_B2) * (g * g)
    m_hat = m / (1.0 - ADAM_B1 ** ADAM_STEP)
    v_hat = v / (1.0 - ADAM_B2 ** ADAM_STEP)
    return -ADAM_LR * (m_hat / (jnp.sqrt(v_hat) + ADAM_EPS) + ADAM_WD * w), m, v


def _adam(w, m, v, gparts, name, hosts=None):
    R, C = w.shape
    n = gparts.shape[0]
    tr = _tile(R, max(SUBLANE, (1 << 18) // C), 16 if gparts.dtype != F32 else SUBLANE)

    def body(w_ref, m_ref, v_ref, gp_ref, g_ref, d_ref, mo_ref, vo_ref):
        g = gp_ref[0].astype(F32)
        for k in range(1, n):
            g = g + gp_ref[k].astype(F32)
        g_ref[...] = g
        d_ref[...], mo_ref[...], vo_ref[...] = _adam_math(w_ref[...], g, m_ref[...], v_ref[...])

    blk = pl.BlockSpec((tr, C), lambda i: (i, 0))
    return _hosted(
        hosts, R * C * 36.0 / HBM_BYTES_PER_US, body, force=True, name=name,
        out_shape=[jax.ShapeDtypeStruct((R, C), F32)] * 4, grid=(R // tr,),
        in_specs=[blk, blk, blk, pl.BlockSpec((n, tr, C), lambda i: (0, i, 0))], out_specs=[blk] * 4,
        args=[w, m, v, gparts], semantics=("parallel",))


def _sum_blocks(parts, name):
    n, R, C = parts.shape
    tr = _tile(R, 512, SUBLANE)

    def body(p_ref, o_ref):
        acc = p_ref[0]
        for k in range(1, n):
            acc = acc + p_ref[k]
        o_ref[...] = acc

    return pl.pallas_call(
        body, name=name, out_shape=jax.ShapeDtypeStruct((R, C), F32), grid=(R // tr,),
        in_specs=[pl.BlockSpec((n, tr, C), lambda i: (0, i, 0))], out_specs=pl.BlockSpec((tr, C), lambda i: (i, 0)),
        compiler_params=_cp("parallel"),
    )(parts)


def _pack_rows(shape):
    return -(-int(np.prod(shape)) // (SUBLANE * LANE)) * SUBLANE


def _pack(arrs):
    parts = []
    for a in arrs:
        flat = a.reshape(-1).astype(F32)
        rows = _pack_rows(a.shape)
        parts.append(jnp.pad(flat, (0, rows * LANE - flat.shape[0])).reshape(rows, LANE))
    return jnp.concatenate(parts, axis=0)


def _unpack(buf, shapes):
    out, off = [], 0
    for s in shapes:
        rows, n = _pack_rows(s), int(np.prod(s))
        out.append(buf[off:off + rows].reshape(-1)[:n].reshape(s))
        off += rows
    return out


def _cols_full(g):
    return jnp.transpose(g, (1, 0, 2)).reshape(g.shape[1], N_DEV * g.shape[2])


def _cols_blocks(w):
    R, N = w.shape
    return jnp.transpose(w.reshape(R, N_DEV, N // N_DEV), (1, 0, 2)).astype(WIRE)


def _rows_blocks(w):
    return w.reshape(N_DEV, w.shape[0] // N_DEV, w.shape[1]).astype(WIRE)


def kernel(x, positions, norm_pre, norm_post, w_in_mla, mla_q_norm, w_uq, mla_kv_norm, w_ukv, w_out_mla, w_in_sc, sc_conv, w_out_sc, w_in_gm, gm_ln_g, gm_ln_b, gm_w_s, gm_b_s, w_out_gm, w_in_cf, cf_dw, cf_dw_b, cf_ln_g, cf_ln_b, w_out_cf, loss_target, m_norm_pre, m_norm_post, m_w_in_mla, m_mla_q_norm, m_w_uq, m_mla_kv_norm, m_w_ukv, m_w_out_mla, m_w_in_sc, m_sc_conv, m_w_out_sc, m_w_in_gm, m_gm_ln_g, m_gm_ln_b, m_gm_w_s, m_gm_b_s, m_w_out_gm, m_w_in_cf, m_cf_dw, m_cf_dw_b, m_cf_ln_g, m_cf_ln_b, m_w_out_cf, v_norm_pre, v_norm_post, v_w_in_mla, v_mla_q_norm, v_w_uq, v_mla_kv_norm, v_w_ukv, v_w_out_mla, v_w_in_sc, v_sc_conv, v_w_out_sc, v_w_in_gm, v_gm_ln_g, v_gm_ln_b, v_gm_w_s, v_gm_b_s, v_w_out_gm, v_w_in_cf, v_cf_dw, v_cf_dw_b, v_cf_ln_g, v_cf_ln_b, v_w_out_cf):
    names = ['norm_pre', 'norm_post', 'w_in_mla', 'mla_q_norm', 'w_uq', 'mla_kv_norm', 'w_ukv', 'w_out_mla', 'w_in_sc',
             'sc_conv', 'w_out_sc', 'w_in_gm', 'gm_ln_g', 'gm_ln_b', 'gm_w_s', 'gm_b_s', 'w_out_gm', 'w_in_cf', 'cf_dw',
             'cf_dw_b', 'cf_ln_g', 'cf_ln_b', 'w_out_cf']
    W = dict(zip(names, (norm_pre, norm_post, w_in_mla, mla_q_norm, w_uq, mla_kv_norm, w_ukv, w_out_mla, w_in_sc, sc_conv,
                         w_out_sc, w_in_gm, gm_ln_g, gm_ln_b, gm_w_s, gm_b_s, w_out_gm, w_in_cf, cf_dw, cf_dw_b, cf_ln_g,
                         cf_ln_b, w_out_cf)))
    Mo = dict(zip(names, (m_norm_pre, m_norm_post, m_w_in_mla, m_mla_q_norm, m_w_uq, m_mla_kv_norm, m_w_ukv, m_w_out_mla,
                          m_w_in_sc, m_sc_conv, m_w_out_sc, m_w_in_gm, m_gm_ln_g, m_gm_ln_b, m_gm_w_s, m_gm_b_s, m_w_out_gm,
                          m_w_in_cf, m_cf_dw, m_cf_dw_b, m_cf_ln_g, m_cf_ln_b, m_w_out_cf)))
    Vo = dict(zip(names, (v_norm_pre, v_norm_post, v_w_in_mla, v_mla_q_norm, v_w_uq, v_mla_kv_norm, v_w_ukv, v_w_out_mla,
                          v_w_in_sc, v_sc_conv, v_w_out_sc, v_w_in_gm, v_gm_ln_g, v_gm_ln_b, v_gm_w_s, v_gm_b_s, v_w_out_gm,
                          v_w_in_cf, v_cf_dw, v_cf_dw_b, v_cf_ln_g, v_cf_ln_b, v_w_out_cf)))
    big = ['w_in_mla', 'w_uq', 'w_ukv', 'w_out_mla', 'w_in_sc', 'w_out_sc', 'w_in_gm', 'w_out_gm', 'w_in_cf', 'w_out_cf']
    row_sharded = {'w_out_mla', 'w_out_sc', 'w_out_gm', 'w_out_cf'}
    chan = ['sc_conv', 'gm_ln_g', 'gm_ln_b', 'cf_dw', 'cf_dw_b', 'cf_ln_g', 'cf_ln_b']
    repl = ['norm_pre', 'norm_post', 'mla_q_norm', 'mla_kv_norm', 'gm_w_s', 'gm_b_s']

    T, D = x.shape[1], x.shape[2]
    xs, tgt = x[0], loss_target[0]
    pos = positions.reshape(T, 1)
    qr, kvr = mla_q_norm.shape[-1], mla_kv_norm.shape[-1]
    H = w_uq.shape[-1] * N_DEV // (NOPE + ROPE)
    HV = H * VHEAD
    c3 = qr + kvr + ROPE
    pa = -(-c3 // 512) * 512
    assert qr % kvr == 0 and qr % LANE == 0 and kvr % LANE == 0 and pa >= qr + kvr + LANE
    mx, my, mc = _mesh_pos()
    me = 4 * mx + 2 * my + mc
    core = mc.astype(jnp.int32).reshape(1)
    cs = D // N_DEV

    def wire(nm):
        return W[nm][0].astype(WIRE)

    hosts = _Hosts()
    G = {}

    pieces = {}

    def gather(nm, name=None, piece_cols=None):
        shard = wire(nm)
        pc = shard.shape[1] if piece_cols is None else _tile(shard.shape[1], piece_cols, LANE)
        pieces[nm] = shard.shape[1] // pc
        for p in range(pieces[nm]):
            job = _GatherJob([shard[:, p * pc:(p + 1) * pc] if pieces[nm] > 1 else shard])
            if name is None:
                hosts.add(job, lambda outs, key=(nm, p): G.update({key: outs[0]}))
            else:
                G[(nm, p)] = _run_job(job, name)[0]

    def gathered(nm):
        if any((nm, p) not in G for p in range(pieces[nm])):
            hosts.flush("ag_late")
        return jnp.stack([G[(nm, p)] for p in range(pieces[nm])], axis=2)

    def gathered_rows(nm):
        g = gathered(nm)
        return g.reshape(g.shape[0] * g.shape[1], g.shape[2] * g.shape[3])

    def gathered_cols(nm):
        g = gathered(nm)
        return jnp.transpose(g, (1, 0, 2, 3)).reshape(g.shape[1], -1)

    gather('w_in_mla', "ag_mla")
    gather('w_uq')
    gather('w_ukv')
    gather('w_out_mla', piece_cols=D // 2)
    for nm_in, nm_out in (('w_in_sc', 'w_out_sc'), ('w_in_gm', 'w_out_gm'), ('w_in_cf', 'w_out_cf')):
        gather(nm_in, piece_cols=GATHER_PIECE)
        gather(nm_out, piece_cols=D // 2)
    chan_rows = [W[nm][0].reshape(-1, cs) for nm in chan]
    chan_cnt = [r.shape[0] for r in chan_rows]
    chan_local = jnp.concatenate(chan_rows, axis=0)
    chan_pad = -chan_local.shape[0] % SUBLANE
    chan_full = _cols_full(_run_job(_GatherJob([jnp.pad(chan_local, ((0, chan_pad), (0, 0)))]), "ag_small")[0])
    offs = np.cumsum([0] + chan_cnt)
    CH = {nm: chan_full[offs[i]:offs[i + 1]] for i, nm in enumerate(chan)}

    w_in_full = gathered_cols('w_in_mla')
    w_cat = jnp.concatenate([w_in_full[:, :c3], jnp.zeros((D, pa - c3), WIRE), w_in_full[:, c3:]], axis=1)
    in_names = ['w_in_mla', 'w_in_sc', 'w_in_gm', 'w_in_cf']
    out_names = ['w_out_mla', 'w_out_sc', 'w_out_gm', 'w_out_cf']
    WIN, WOUT = [w_cat], []

    half = ROPE // 2
    invf_np = np.zeros((1, LANE), np.float32)
    invf_np[0, :ROPE] = np.tile(np.float32(ROPE_THETA) ** (-np.arange(half, dtype=np.float32) / np.float32(half)), 2)
    invf = jnp.asarray(invf_np)
    gm_ws = gm_w_s[0]
    gm_bst = jnp.transpose(gm_b_s[0])

    xin, hs, Ps, acts, ys, keep = [xs], [], [], [], [], {}
    h = _rms_fwd(xs, norm_pre[0:1], "pre0")
    for i in range(4):
        if i > 0:
            WIN.append(gathered_cols(in_names[i]))
        P = _mm(h, WIN[i], out_dtype=MXU, name=f"in{i}", hosts=hosts)
        if i == 0:
            w_uq_pad = jnp.pad(gathered_cols('w_uq').reshape(qr, H, NOPE + ROPE),
                               ((0, 0), (0, 0), (0, QPAD - NOPE - ROPE))).reshape(qr, H * QPAD)
            w_ukv3 = gathered_cols('w_ukv').reshape(kvr, H, NOPE + VHEAD)
            w_k_pad = jnp.pad(w_ukv3[:, :, :NOPE], ((0, 0), (0, 0), (0, QPAD - NOPE))).reshape(kvr, H * QPAD)
            w_v = w_ukv3[:, :, NOPE:].reshape(kvr, HV)
            cqn, ckvn = _mla_norms_fwd(P, mla_q_norm, mla_kv_norm, qr, kvr, "mla_norms")
            qpad = _mm(cqn, w_uq_pad, out_dtype=MXU, name="mla_q")
            kpad = _mm(ckvn, w_k_pad, out_dtype=MXU, name="mla_k")
            vv = _mm(ckvn, w_v, out_dtype=MXU, name="mla_v")
            qcat, kcat, vcat = _rope_fwd(qpad, kpad, vv, P, pos, invf, (qr + kvr) // LANE, H, "rope")
            o, act, lse = _flash_fwd(qcat, kcat, vcat, P, pa // VHEAD, H, "attn", hosts=hosts)
            keep.update(cqn=cqn, ckvn=ckvn, qcat=qcat, kcat=kcat, v=vv, o=o, lse=lse)
        elif i == 1:
            act = _sc_fwd(P, CH['sc_conv'], D, "sc_mix")
        elif i == 2:
            act = _gm_fwd(P, CH['gm_ln_g'], CH['gm_ln_b'], gm_ws, gm_bst, D, "gm_mix")
        else:
            yc = _cf_conv_fwd(P, CH['cf_dw'], CH['cf_dw_b'], D, "cf_conv")
            act = _cf_act_fwd(yc, P, CH['cf_ln_g'], CH['cf_ln_b'], D, "cf_act")
            keep.update(yc=yc)
        WOUT.append(gathered_rows(out_names[i]))
        y = _mm(act, WOUT[i], out_dtype=F32, name=f"out{i}", hosts=hosts)
        hs.append(h), Ps.append(P), acts.append(act), ys.append(y)
        if i < 3:
            x_next, h = _post_fwd(y, norm_post[i:i + 1], xin[i], f"post{i}", next_gain=norm_pre[i + 1:i + 2])
            xin.append(x_next)
        else:
            dx, loss_part = _post_fwd(y, norm_post[i:i + 1], xin[i], f"post{i}", target=tgt)
    loss = lax.psum(loss_part[0, 0], ("x", "y", "c"))

    hosts.flush("ag_late")
    dnorm_pre, dnorm_post, small_g, RS = [None] * 4, [None] * 4, {}, {}
    for i in (3, 2, 1, 0):
        dy, dnorm_post[i] = _rms_bwd(ys[i], dx, norm_post[i:i + 1], f"post_bwd{i}", out_dtype=MXU)
        da = _mm(dy, WOUT[i], tb=True, out_dtype=MXU, name=f"out_bwd{i}", hosts=hosts)
        dw_out = _mm(acts[i], dy, ta=True, out_dtype=WIRE, name=f"out_dw{i}", hosts=hosts).reshape(N_DEV, -1, D)
        _queue_reduce_scatter(hosts, {out_names[i]: dw_out}, core, f"rs_out{i}", RS)
        P = Ps[i]
        if i == 3:
            dyc, d_z, small_g['cf_ln_g'], small_g['cf_ln_b'] = _cf_act_bwd(
                keep['yc'], P, da, CH['cf_ln_g'], CH['cf_ln_b'], D, "cf_act_bwd")
            d_a, d_g, small_g['cf_dw'], small_g['cf_dw_b'] = _cf_conv_bwd(P, dyc, CH['cf_dw'], D, "cf_conv_bwd")
            dP = jnp.concatenate([d_a, d_g, d_z], axis=1)
        elif i == 2:
            dP, small_g['gm_ln_g'], small_g['gm_ln_b'], small_g['gm_w_s'], dbs = _gm_bwd(
                P, da, CH['gm_ln_g'], CH['gm_ln_b'], gm_ws, gm_bst, D, "gm_mix_bwd")
            small_g['gm_b_s'] = dbs[:, :, 0]
        elif i == 1:
            dP, small_g['sc_conv'] = _sc_bwd(P, da, CH['sc_conv'], D, "sc_mix_bwd")
        else:
            do, d_z, delta = _mla_gate_bwd(da, keep['o'], P, pa, H, "gate_bwd")
            dqcat, dkcat, dv = _flash_bwd(keep['qcat'], keep['kcat'], keep['v'], do, keep['lse'], delta, H, "attn_bwd",
                                          hosts=hosts)
            dqpad, dkr = _rope_bwd(dqcat, dkcat, pos, invf, H, "rope_bwd")
            dcqn = _mm(dqpad, w_uq_pad, tb=True, out_dtype=F32, name="mla_q_bwd", hosts=hosts)
            dckvn_k = _mm(dkcat, w_k_pad, tb=True, out_dtype=F32, name="mla_k_bwd", hosts=hosts)
            dckvn_v = _mm(dv, w_v, tb=True, out_dtype=F32, name="mla_v_bwd", hosts=hosts)
            dw_uq_pad = _mm(keep['cqn'], dqpad, ta=True, out_dtype=F32, name="mla_q_dw", hosts=hosts)
            dw_k_pad = _mm(keep['ckvn'], dkcat, ta=True, out_dtype=F32, name="mla_k_dw", hosts=hosts)
            dw_v = _mm(keep['ckvn'], dv, ta=True, out_dtype=F32, name="mla_v_dw", hosts=hosts)
            dw_uq = _cols_blocks(dw_uq_pad.reshape(qr, H, QPAD)[:, :, :NOPE + ROPE].reshape(qr, H * (NOPE + ROPE)))
            dw_ukv = _cols_blocks(jnp.concatenate(
                [dw_k_pad.reshape(kvr, H, QPAD)[:, :, :NOPE], dw_v.reshape(kvr, H, VHEAD)], axis=2).reshape(kvr, H * (NOPE + VHEAD)))
            _queue_reduce_scatter(hosts, {'w_uq': dw_uq, 'w_ukv': dw_ukv}, core, "rs_lat", RS)
            d_pa, small_g['mla_q_norm'], small_g['mla_kv_norm'] = _mla_norms_bwd(
                P, dcqn, dckvn_k, dckvn_v, dkr, mla_q_norm, mla_kv_norm, qr, kvr, pa, "mla_norms_bwd")
            dP = jnp.concatenate([d_pa, d_z], axis=1)
        dh = _mm(dP, WIN[i], tb=True, out_dtype=MXU, name=f"in_bwd{i}", hosts=hosts)
        dx, dnorm_pre[i] = _rms_bwd(xin[i], dh, norm_pre[i:i + 1], f"pre_bwd{i}", resid=dx)
        if i == 0:
            dw_cat = _mm(hs[i], dP, ta=True, out_dtype=F32, name="in_dw0", hosts=hosts)
            dw_in = _cols_blocks(jnp.concatenate([dw_cat[:, :c3], dw_cat[:, pa:]], axis=1))
        else:
            dw_in = _mm(hs[i], dP, ta=True, out_dtype=WIRE, name=f"in_dw{i}", blocks=N_DEV, hosts=hosts)
        _queue_reduce_scatter(hosts, {in_names[i]: dw_in}, core, f"rs_in{i}", RS)
    grad_x = dx[None]

    small_g['norm_pre'] = jnp.concatenate(dnorm_pre, axis=0)
    small_g['norm_post'] = jnp.concatenate(dnorm_post, axis=0)
    small_names = repl + chan
    small_shapes = [(small_g[nm].shape) for nm in small_names]
    tail, big_out = {}, {}
    hosts.add(_GatherJob([_pack([small_g[nm] for nm in small_names])]), lambda outs: tail.update(grads=outs[0]))

    def adam_big(nm, carrier=None):
        if nm not in RS:
            hosts.flush("rs_late")
        big_out[nm] = [o[None] for o in _adam(W[nm][0], Mo[nm][0], Vo[nm][0], RS[nm], "adam_" + nm, hosts=carrier)]

    for nm in ('w_in_sc', 'w_in_gm', 'w_in_cf', 'w_out_sc', 'w_out_gm', 'w_out_cf'):
        adam_big(nm, hosts)
    hosts.flush("rs_late")
    for nm in ('w_in_mla', 'w_uq', 'w_ukv', 'w_out_mla'):
        adam_big(nm)
    totals = dict(zip(small_names, _unpack(_sum_blocks(tail['grads'], "sum_grads"), small_shapes)))
    local_g = []
    for nm in small_names:
        g = totals[nm]
        if nm in chan:
            g = lax.dynamic_slice_in_dim(g.reshape(-1, D), me * cs, cs, axis=1)
        local_g.append(g.reshape(W[nm].shape))
    local_shapes = [W[nm].shape for nm in small_names]
    sm = _adam(_pack([W[nm] for nm in small_names]), _pack([Mo[nm] for nm in small_names]),
               _pack([Vo[nm] for nm in small_names]), _pack(local_g)[None], "adam_small")
    small_out = [dict(zip(small_names, _unpack(buf, local_shapes))) for buf in sm]

    outs = [loss, grad_x]
    for k in range(4):
        outs += [big_out[nm][k] if nm in big_out else small_out[k][nm] for nm in names]
    return tuple(outs)
```

```python
import functools
import math

import numpy as np
import jax
import jax.numpy as jnp
from jax import lax
from jax.experimental import pallas as pl
from jax.experimental.pallas import tpu as pltpu

F32 = jnp.float32
MXU = jnp.bfloat16
WIRE = jnp.bfloat16
MESH = pl.DeviceIdType.MESH
N_DEV = 8

NORM_EPS = 1e-6
LN_EPS = 1e-5
ROPE_THETA = 10000.0
NOPE, ROPE, VHEAD = 128, 64, 128
QPAD = 256
ATT_SCALE = float((NOPE + ROPE) ** -0.5)
LOG2E = 1.4426950408889634
INV_SQRT2 = 0.7071067811865476
INV_SQRT_2PI = 0.3989422804014327
ADAM_LR, ADAM_B1, ADAM_B2, ADAM_EPS, ADAM_WD, ADAM_STEP = 0.001, 0.9, 0.999, 1e-08, 0.01, 10

LANE = 128
SUBLANE = 8
VMEM_LIMIT = 56 * 1024 * 1024
MM_TILE = 1024
MM_TK = 4096
ATT_TILE = 512
ROW_TILE = 128
CONV_CHUNK = 256
CONV_CB = 128
MXU_FLOP_PER_US = 9.2e8
HBM_BYTES_PER_US = 3.0e6
ATT_FWD_US, ATT_BWD_US = 1150.0, 1600.0
HOST_SLACK = 1.35

NT = (((1,), (1,)), ((), ()))
TN = (((0,), (0,)), ((), ()))
NN = (((1,), (0,)), ((), ()))


def _tile(dim, pref, align):
    t = min(pref, dim)
    t -= t % align
    while t >= align:
        if dim % t == 0:
            return t
        t -= align
    return dim


def _cp(*sem):
    return pltpu.CompilerParams(dimension_semantics=sem, vmem_limit_bytes=VMEM_LIMIT)


def _sig(v):
    return jax.nn.sigmoid(v)


def _silu(v):
    return v * _sig(v)


def _silu_grad(v):
    s = _sig(v)
    return s * (1.0 + v * (1.0 - s))


def _gelu(v):
    return 0.5 * v * (1.0 + lax.erf(v * INV_SQRT2))


def _gelu_grad(v):
    return 0.5 * (1.0 + lax.erf(v * INV_SQRT2)) + v * jnp.exp(-0.5 * v * v) * INV_SQRT_2PI


def _mesh_pos():
    return lax.axis_index("x"), lax.axis_index("y"), lax.axis_index("c")


def _other_chips(x, y):
    return [(1 - x, y), (x, 1 - y), (1 - x, 1 - y)]


class _GatherJob:
    US_PER_MB = 46.0

    def __init__(self, arrs, cols=False):
        n = len(arrs)
        self.ins = list(arrs)
        self.cols = cols
        if cols:
            assert all(a.ndim == 2 and a.shape[1] % LANE == 0 for a in arrs)
            self.out_shape = [jax.ShapeDtypeStruct((a.shape[0], N_DEV * a.shape[1]), a.dtype) for a in arrs]
        else:
            self.out_shape = [jax.ShapeDtypeStruct((N_DEV,) + a.shape, a.dtype) for a in arrs]
        self.sems = [pltpu.SemaphoreType.DMA((n, 7)), pltpu.SemaphoreType.DMA((n, 7)), pltpu.SemaphoreType.DMA((n,))]
        self.cost = self.US_PER_MB * sum(a.size * a.dtype.itemsize for a in arrs) / 1e6

    def _parts(self, ins, outs, sems):
        send_sems, recv_sems, loc_sems = sems
        x, y, c = _mesh_pos()
        n = len(ins)

        def block_of(a, dev):
            if not self.cols:
                return outs[a].at[dev]
            width = ins[a].shape[1]
            return outs[a].at[:, pl.ds(pl.multiple_of(dev * width, LANE), width)]

        def copy(a, k, block, to, src=None):
            dst = block_of(a, 4 * block[0] + 2 * block[1] + block[2])
            return pltpu.make_async_remote_copy(
                src_ref=dst if src is None else src, dst_ref=dst,
                send_sem=send_sems.at[a, k], recv_sem=recv_sems.at[a, k],
                device_id=to, device_id_type=MESH)

        me, sib, chips = (x, y, c), (x, y, 1 - c), _other_chips(x, y)
        locs = [pltpu.make_async_copy(ins[a], block_of(a, 4 * x + 2 * y + c), loc_sems.at[a]) for a in range(n)]
        first = []
        for a in range(n):
            first.append(copy(a, 0, me, sib, src=ins[a]))
            for j, chip in enumerate(chips):
                first.append(copy(a, 1 + j, me, (*chip, c), src=ins[a]))
        passed = [copy(a, 4 + j, (*chip, c), sib) for j, chip in enumerate(chips) for a in range(n)]
        landed = [copy(a, 1 + j, (*chip, c), me) for j, chip in enumerate(chips) for a in range(n)]
        from_sib = [copy(a, 0, sib, me) for a in range(n)]
        from_sib += [copy(a, 4 + j, (*chip, 1 - c), me) for j, chip in enumerate(chips) for a in range(n)]
        return locs, first, landed, passed, from_sib

    def first(self, ins, outs, sems):
        locs, first, _, _, _ = self._parts(ins, outs, sems)
        for cp in locs + first:
            cp.start()

    def mid(self, ins, outs, sems):
        _, _, landed, passed, _ = self._parts(ins, outs, sems)
        for got, fwd in zip(landed, passed):
            got.wait_recv()
            fwd.start()

    def last(self, ins, outs, sems):
        locs, first, _, passed, from_sib = self._parts(ins, outs, sems)
        for cp in from_sib:
            cp.wait_recv()
        for cp in first + passed:
            cp.wait_send()
        for cp in locs:
            cp.wait()


class _SwapJob:
    US_PER_MB = 0.8

    def __init__(self, parts):
        n = len(parts)
        self.ins = list(parts)
        self.out_shape = [jax.ShapeDtypeStruct((4, 1) + p.shape[2:], p.dtype) for p in parts]
        self.sems = [pltpu.SemaphoreType.DMA((n,)), pltpu.SemaphoreType.DMA((n,))]
        self.cost = 5.0 + self.US_PER_MB * sum(p.size * p.dtype.itemsize for p in parts) / 1e6

    def _copies(self, ins, outs, sems):
        send_sems, recv_sems = sems
        x, y, c = _mesh_pos()
        return [pltpu.make_async_remote_copy(
            src_ref=ins[a].at[:, pl.ds(1 - c, 1)], dst_ref=outs[a],
            send_sem=send_sems.at[a], recv_sem=recv_sems.at[a],
            device_id=(x, y, 1 - c), device_id_type=MESH) for a in range(len(ins))]

    def first(self, ins, outs, sems):
        for cp in self._copies(ins, outs, sems):
            cp.start()

    def mid(self, ins, outs, sems):
        pass

    def last(self, ins, outs, sems):
        for cp in self._copies(ins, outs, sems):
            cp.wait()


class _ChipsJob:
    US_PER_MB = 11.0

    def __init__(self, qs):
        n = len(qs)
        self.ins = list(qs)
        self.out_shape = [jax.ShapeDtypeStruct(q.shape, q.dtype) for q in qs]
        self.sems = [pltpu.SemaphoreType.DMA((n, 3)), pltpu.SemaphoreType.DMA((n, 3)), pltpu.SemaphoreType.DMA((n,))]
        self.cost = 5.0 + self.US_PER_MB * sum(q.size * q.dtype.itemsize for q in qs) / 1e6

    def _copies(self, ins, outs, sems):
        send_sems, recv_sems, loc_sems = sems
        x, y, c = _mesh_pos()
        cps = []
        for a in range(len(ins)):
            cps.append(pltpu.make_async_copy(ins[a].at[2 * x + y], outs[a].at[3], loc_sems.at[a]))
            for k, chip in enumerate(_other_chips(x, y)):
                cps.append(pltpu.make_async_remote_copy(
                    src_ref=ins[a].at[2 * chip[0] + chip[1]], dst_ref=outs[a].at[k],
                    send_sem=send_sems.at[a, k], recv_sem=recv_sems.at[a, k],
                    device_id=(*chip, c), device_id_type=MESH))
        return cps

    def first(self, ins, outs, sems):
        for cp in self._copies(ins, outs, sems):
            cp.start()

    def mid(self, ins, outs, sems):
        pass

    def last(self, ins, outs, sems):
        for cp in self._copies(ins, outs, sems):
            cp.wait()


def _call(body, *, name, out_shape, in_specs, out_specs, args, grid=(), scratch_shapes=(), semantics=(), job=None):
    if job is None:
        outs = pl.pallas_call(
            body, name=name, out_shape=out_shape, grid=grid, in_specs=in_specs, out_specs=out_specs,
            scratch_shapes=list(scratch_shapes), compiler_params=_cp(*semantics))(*args)
        return outs, None
    single = not isinstance(out_shape, (list, tuple))
    host_out = [out_shape] if single else list(out_shape)
    host_ospecs = [out_specs] if single else list(out_specs)
    n_in, n_out, n_scr = len(in_specs), len(host_out), len(scratch_shapes)
    j_in, j_out = len(job.ins), len(job.out_shape)
    total = int(np.prod(grid)) if grid else 1
    mid_step = max(total * 7 // 8, 1)

    def full(*refs):
        h_in, jin = refs[:n_in], refs[n_in:n_in + j_in]
        o = n_in + j_in
        h_out, jout = refs[o:o + n_out], refs[o + n_out:o + n_out + j_out]
        o += n_out + j_out
        h_scr, jsem = refs[o:o + n_scr], refs[o + n_scr:]
        if total == 1:
            job.first(jin, jout, jsem)
            body(*h_in, *h_out, *h_scr)
            job.mid(jin, jout, jsem)
            job.last(jin, jout, jsem)
            return
        step = pl.program_id(0)
        for d in range(1, len(grid)):
            step = step * grid[d] + pl.program_id(d)
        pl.when(step == 0)(lambda: job.first(jin, jout, jsem))
        body(*h_in, *h_out, *h_scr)
        pl.when(step == mid_step)(lambda: job.mid(jin, jout, jsem))
        pl.when(step == total - 1)(lambda: job.last(jin, jout, jsem))

    hbm = pl.BlockSpec(memory_space=pltpu.HBM)
    outs = pl.pallas_call(
        full, name=name, out_shape=host_out + list(job.out_shape), grid=grid,
        in_specs=list(in_specs) + [hbm] * j_in, out_specs=host_ospecs + [hbm] * j_out,
        scratch_shapes=list(scratch_shapes) + list(job.sems),
        compiler_params=_cp(*(("arbitrary",) * len(grid))))(*args, *job.ins)
    host = outs[0] if single else list(outs[:n_out])
    return host, list(outs[n_out:])


def _run_job(job, name):
    def body():
        pass

    return _call(body, name=name, out_shape=[], in_specs=[], out_specs=[], args=[], job=job)[1]


class _Hosts:
    def __init__(self):
        self.pending = []
        self.flushed = 0

    def add(self, job, done):
        self.pending.append((job, done))

    def take(self, duration, force=False):
        taken, room = [], HOST_SLACK * duration
        for e in list(self.pending):
            if e[0].cost <= room:
                taken.append(e)
                room -= e[0].cost
        if not taken and force and self.pending:
            taken = [min(self.pending, key=lambda e: e[0].cost)]
        for e in taken:
            self.pending.remove(e)
        return taken

    def flush(self, name):
        while self.pending:
            job, done = self.pending.pop(0)
            done(_run_job(job, f"{name}{self.flushed}"))
            self.flushed += 1


def _pair_add(part4, recv, core, name):
    _, _, R, C = part4.shape
    tr = _tile(R, max(SUBLANE, (1 << 19) // C), 16)

    def body(core_ref, p_ref, r_ref, o_ref):
        o_ref[...] = (p_ref[...].astype(F32) + r_ref[...].astype(F32)).astype(o_ref.dtype)

    return pl.pallas_call(
        body, name=name,
        out_shape=jax.ShapeDtypeStruct((4, R, C), WIRE),
        grid_spec=pltpu.PrefetchScalarGridSpec(
            num_scalar_prefetch=1, grid=(4, R // tr),
            in_specs=[pl.BlockSpec((None, None, tr, C), lambda s, i, cr: (s, cr[0], i, 0)),
                      pl.BlockSpec((None, None, tr, C), lambda s, i, cr: (s, 0, i, 0))],
            out_specs=pl.BlockSpec((None, tr, C), lambda s, i, cr: (s, i, 0))),
        compiler_params=_cp("parallel", "parallel"),
    )(core, part4, recv)


def _queue_reduce_scatter(hosts, parts, core, name, sink):
    keys = list(parts)
    p4 = [parts[k].reshape((4, 2) + parts[k].shape[1:]) for k in keys]

    def swapped(recv):
        qs = [_pair_add(p, r, core, f"{name}_add{i}") for i, (p, r) in enumerate(zip(p4, recv))]
        hosts.add(_ChipsJob(qs), lambda outs: sink.update(zip(keys, outs)))

    hosts.add(_SwapJob(p4), swapped)


class _JobGroup:
    def __init__(self, jobs):
        self.jobs = jobs
        self.ins = [a for j in jobs for a in j.ins]
        self.out_shape = [s for j in jobs for s in j.out_shape]
        self.sems = [s for j in jobs for s in j.sems]

    def _each(self, phase, ins, outs, sems):
        i = o = s = 0
        for j in self.jobs:
            ni, no, ns = len(j.ins), len(j.out_shape), len(j.sems)
            getattr(j, phase)(ins[i:i + ni], outs[o:o + no], sems[s:s + ns])
            i, o, s = i + ni, o + no, s + ns

    def first(self, ins, outs, sems):
        self._each("first", ins, outs, sems)

    def mid(self, ins, outs, sems):
        self._each("mid", ins, outs, sems)

    def last(self, ins, outs, sems):
        self._each("last", ins, outs, sems)


def _hosted(hosts, duration, body, force=False, **kw):
    entries = hosts.take(duration, force) if hosts is not None else []
    out, jouts = _call(body, job=_JobGroup([e[0] for e in entries]) if entries else None, **kw)
    o = 0
    for job, done in entries:
        done(jouts[o:o + len(job.out_shape)])
        o += len(job.out_shape)
    return out


def _mm(a, b, *, ta=False, tb=False, out_dtype, name, blocks=None, hosts=None):
    a_parts = a.shape[0] if a.ndim == 3 else 0
    b_parts = b.shape[0] if b.ndim == 3 else 0
    assert not (a_parts and ta) and not (b_parts and tb)
    if a_parts:
        M, K = a.shape[1], a_parts * a.shape[2]
    else:
        M, K = (a.shape[1], a.shape[0]) if ta else a.shape
    N = b_parts * b.shape[2] if b_parts else (b.shape[0] if tb else b.shape[1])
    assert K == (b.shape[1] if (tb or b_parts) else b.shape[0]), (a.shape, b.shape, ta, tb)
    ns = N // blocks if blocks else N
    tm = _tile(M, MM_TILE, LANE)
    tk = _tile(a.shape[2] if a_parts else K, MM_TK, LANE)
    tn = _tile(math.gcd(ns, b.shape[2]) if b_parts else ns, MM_TILE, LANE)
    nk = K // tk
    per = ns // tn

    dims = (((0,) if ta else (1,), (1,) if tb else (0,)), ((), ()))

    def body(a_ref, b_ref, o_ref, *acc):
        part = lax.dot_general(a_ref[...], b_ref[...], dims, preferred_element_type=F32)
        if nk == 1:
            o_ref[...] = part.astype(o_ref.dtype)
            return
        acc_ref, k = acc[0], pl.program_id(2)

        @pl.when(k == 0)
        def _():
            acc_ref[...] = part

        @pl.when(k > 0)
        def _():
            acc_ref[...] += part

        @pl.when(k == nk - 1)
        def _():
            o_ref[...] = acc_ref[...].astype(o_ref.dtype)

    if a_parts:
        a_per = a.shape[2] // tk
        a_spec = pl.BlockSpec((None, tm, tk), lambda i, j, k: (k // a_per, i, k % a_per))
    else:
        a_spec = pl.BlockSpec((tk, tm), lambda i, j, k: (k, i)) if ta else pl.BlockSpec((tm, tk), lambda i, j, k: (i, k))
    if b_parts:
        b_per = b.shape[2] // tn
        b_spec = pl.BlockSpec((None, tk, tn), lambda i, j, k: (j // b_per, k, j % b_per))
    else:
        b_spec = pl.BlockSpec((tn, tk), lambda i, j, k: (j, k)) if tb else pl.BlockSpec((tk, tn), lambda i, j, k: (k, j))
    if blocks:
        out_shape = jax.ShapeDtypeStruct((blocks, M, ns), out_dtype)
        o_spec = pl.BlockSpec((None, tm, tn), lambda i, j, k: (j // per, i, j % per))
    else:
        out_shape = jax.ShapeDtypeStruct((M, N), out_dtype)
        o_spec = pl.BlockSpec((tm, tn), lambda i, j, k: (i, j))
    return _hosted(
        hosts, 2.0 * M * N * K / MXU_FLOP_PER_US, body, name=name, out_shape=out_shape, grid=(M // tm, N // tn, nk),
        in_specs=[a_spec, b_spec], out_specs=o_spec, args=[a, b],
        scratch_shapes=[pltpu.VMEM((tm, tn), F32)] if nk > 1 else [], semantics=("parallel", "parallel", "arbitrary"))


def _rms_fwd(xin, g, name):
    T, D = xin.shape
    tr = _tile(T, ROW_TILE, SUBLANE)

    def body(x_ref, g_ref, o_ref):
        xv = x_ref[...]
        r = lax.rsqrt(jnp.mean(xv * xv, axis=-1, keepdims=True) + NORM_EPS)
        o_ref[...] = (xv * r * g_ref[...]).astype(o_ref.dtype)

    row = pl.BlockSpec((tr, D), lambda i: (i, 0))
    return pl.pallas_call(
        body, name=name, out_shape=jax.ShapeDtypeStruct((T, D), MXU),
        grid=(T // tr,), in_specs=[row, pl.BlockSpec((1, D), lambda i: (0, 0))],
        out_specs=row, compiler_params=_cp("parallel"),
    )(xin, g)


def _rms_bwd(xin, dout, g, name, resid=None, out_dtype=F32):
    T, D = xin.shape
    tr = _tile(T, ROW_TILE, SUBLANE)

    def body(*refs):
        x_ref, d_ref, g_ref = refs[:3]
        dx_ref, dg_ref = refs[-2:]
        xv = x_ref[...].astype(F32)
        dv = d_ref[...].astype(F32)
        r = lax.rsqrt(jnp.mean(xv * xv, axis=-1, keepdims=True) + NORM_EPS)
        xh = xv * r
        dh = dv * g_ref[...]
        dxv = r * (dh - xh * jnp.mean(dh * xh, axis=-1, keepdims=True))
        if resid is not None:
            dxv = refs[3][...] + dxv
        dx_ref[...] = dxv.astype(dx_ref.dtype)

        @pl.when(pl.program_id(0) == 0)
        def _():
            dg_ref[...] = jnp.zeros_like(dg_ref)

        dg_ref[...] += jnp.sum(dv * xh, axis=0, keepdims=True)

    row = pl.BlockSpec((tr, D), lambda i: (i, 0))
    vec = pl.BlockSpec((1, D), lambda i: (0, 0))
    ins = [xin, dout, g] + ([resid] if resid is not None else [])
    return pl.pallas_call(
        body, name=name,
        out_shape=[jax.ShapeDtypeStruct((T, D), out_dtype), jax.ShapeDtypeStruct((1, D), F32)],
        grid=(T // tr,), in_specs=[row, row, vec] + ([row] if resid is not None else []),
        out_specs=[row, vec], compiler_params=_cp("arbitrary"),
    )(*ins)


def _post_fwd(y, g, resid, name, next_gain=None, target=None):
    T, D = y.shape
    tr = _tile(T, ROW_TILE, SUBLANE)

    def body(y_ref, g_ref, r_ref, e_ref, o1_ref, o2_ref):
        yv = y_ref[...]
        r = lax.rsqrt(jnp.mean(yv * yv, axis=-1, keepdims=True) + NORM_EPS)
        xv = r_ref[...] + yv * r * g_ref[...]
        if target is None:
            o1_ref[...] = xv
            r2 = lax.rsqrt(jnp.mean(xv * xv, axis=-1, keepdims=True) + NORM_EPS)
            o2_ref[...] = (xv * r2 * e_ref[...]).astype(o2_ref.dtype)
        else:
            e = xv - e_ref[...]
            o1_ref[...] = e * (1.0 / D)

            @pl.when(pl.program_id(0) == 0)
            def _():
                o2_ref[...] = jnp.zeros_like(o2_ref)

            rows = jnp.sum(e * e, axis=-1, keepdims=True) * (1.0 / D)
            o2_ref[...] += jnp.broadcast_to(0.5 * jnp.sum(rows, axis=0, keepdims=True), o2_ref.shape)

    row = pl.BlockSpec((tr, D), lambda i: (i, 0))
    vec = pl.BlockSpec((1, D), lambda i: (0, 0))
    if target is None:
        extra, e_spec = next_gain, vec
        out2, o2_spec = jax.ShapeDtypeStruct((T, D), MXU), row
    else:
        extra, e_spec = target, row
        out2, o2_spec = jax.ShapeDtypeStruct((1, LANE), F32), pl.BlockSpec((1, LANE), lambda i: (0, 0))
    return pl.pallas_call(
        body, name=name, out_shape=[jax.ShapeDtypeStruct((T, D), F32), out2],
        grid=(T // tr,), in_specs=[row, vec, row, e_spec], out_specs=[row, o2_spec],
        compiler_params=_cp("arbitrary" if target is not None else "parallel"),
    )(y, g, resid, extra)


def _mla_norms_fwd(P, gq, gkv, qr, kvr, name):
    T = P.shape[0]
    tr = _tile(T, ROW_TILE, SUBLANE)

    def body(cq_ref, ckv_ref, gq_ref, gkv_ref, oq_ref, okv_ref):
        for x_ref, g_ref, o_ref in ((cq_ref, gq_ref, oq_ref), (ckv_ref, gkv_ref, okv_ref)):
            xv = x_ref[...].astype(F32)
            r = lax.rsqrt(jnp.mean(xv * xv, axis=-1, keepdims=True) + NORM_EPS)
            o_ref[...] = (xv * r * g_ref[...]).astype(o_ref.dtype)

    return pl.pallas_call(
        body, name=name,
        out_shape=[jax.ShapeDtypeStruct((T, qr), MXU), jax.ShapeDtypeStruct((T, kvr), MXU)],
        grid=(T // tr,),
        in_specs=[pl.BlockSpec((tr, qr), lambda i: (i, 0)), pl.BlockSpec((tr, kvr), lambda i: (i, qr // kvr)),
                  pl.BlockSpec((1, qr), lambda i: (0, 0)), pl.BlockSpec((1, kvr), lambda i: (0, 0))],
        out_specs=[pl.BlockSpec((tr, qr), lambda i: (i, 0)), pl.BlockSpec((tr, kvr), lambda i: (i, 0))],
        compiler_params=_cp("parallel"),
    )(P, P, gq, gkv)


def _mla_norms_bwd(P, dcqn, dckvn_k, dckvn_v, dkr, gq, gkv, qr, kvr, pa, name):
    T = P.shape[0]
    tr = _tile(T, ROW_TILE, SUBLANE)
    c2 = qr + kvr

    def body(cq_ref, ckv_ref, dq_ref, dk_ref, dv_ref, dkr_ref, gq_ref, gkv_ref, dp_ref, dgq_ref, dgkv_ref):
        @pl.when(pl.program_id(0) == 0)
        def _():
            dgq_ref[...] = jnp.zeros_like(dgq_ref)
            dgkv_ref[...] = jnp.zeros_like(dgkv_ref)

        def one(x_ref, dv, g_ref, dg_ref):
            xv = x_ref[...].astype(F32)
            r = lax.rsqrt(jnp.mean(xv * xv, axis=-1, keepdims=True) + NORM_EPS)
            xh = xv * r
            dh = dv * g_ref[...]
            dg_ref[...] += jnp.sum(dv * xh, axis=0, keepdims=True)
            return r * (dh - xh * jnp.mean(dh * xh, axis=-1, keepdims=True))

        dp_ref[:, 0:qr] = one(cq_ref, dq_ref[...], gq_ref, dgq_ref).astype(dp_ref.dtype)
        dp_ref[:, qr:c2] = one(ckv_ref, dk_ref[...] + dv_ref[...], gkv_ref, dgkv_ref).astype(dp_ref.dtype)
        dp_ref[:, c2:c2 + LANE] = dkr_ref[...].astype(dp_ref.dtype)
        if pa > c2 + LANE:
            dp_ref[:, c2 + LANE:pa] = jnp.zeros((tr, pa - c2 - LANE), dp_ref.dtype)

    return pl.pallas_call(
        body, name=name,
        out_shape=[jax.ShapeDtypeStruct((T, pa), MXU), jax.ShapeDtypeStruct((1, qr), F32),
                   jax.ShapeDtypeStruct((1, kvr), F32)],
        grid=(T // tr,),
        in_specs=[pl.BlockSpec((tr, qr), lambda i: (i, 0)), pl.BlockSpec((tr, kvr), lambda i: (i, qr // kvr)),
                  pl.BlockSpec((tr, qr), lambda i: (i, 0)), pl.BlockSpec((tr, kvr), lambda i: (i, 0)),
                  pl.BlockSpec((tr, kvr), lambda i: (i, 0)), pl.BlockSpec((tr, LANE), lambda i: (i, 0)),
                  pl.BlockSpec((1, qr), lambda i: (0, 0)), pl.BlockSpec((1, kvr), lambda i: (0, 0))],
        out_specs=[pl.BlockSpec((tr, pa), lambda i: (i, 0)), pl.BlockSpec((1, qr), lambda i: (0, 0)),
                   pl.BlockSpec((1, kvr), lambda i: (0, 0))],
        compiler_params=_cp("arbitrary"),
    )(P, P, dcqn, dckvn_k, dckvn_v, dkr, gq, gkv)


def _rope_tables(pos_ref, invf_ref):
    ang = pos_ref[...].astype(F32) * invf_ref[...]
    lane = lax.broadcasted_iota(jnp.int32, ang.shape, 1)
    cos, sin = jnp.cos(ang), jnp.sin(ang)
    half = ROPE // 2
    c = jnp.where(lane < ROPE, cos, 0.0)
    s1 = jnp.where(lane < half, -sin, 0.0)
    s2 = jnp.where((lane >= half) & (lane < ROPE), sin, 0.0)
    return c, s1, s2


def _rope_fwd(qpad, kpad, v, P, pos, invf, kr_blk, H, name):
    T = qpad.shape[0]
    tr = _tile(T, ROW_TILE, SUBLANE)

    def body(q_ref, k_ref, v_ref, kr_ref, pos_ref, invf_ref, qo_ref, ko_ref, vo_ref):
        c, s1, s2 = _rope_tables(pos_ref, invf_ref)

        def rot(t):
            return t * c + pltpu.roll(t, LANE - ROPE // 2, 1) * s1 + pltpu.roll(t, ROPE // 2, 1) * s2

        kr = rot(kr_ref[...].astype(F32)).astype(ko_ref.dtype)
        ones = jnp.ones((tr, QPAD - VHEAD), vo_ref.dtype)
        for h in range(H):
            lo = h * QPAD
            qo_ref[:, lo:lo + NOPE] = q_ref[:, lo:lo + NOPE]
            qo_ref[:, lo + NOPE:lo + QPAD] = rot(q_ref[:, lo + NOPE:lo + QPAD].astype(F32)).astype(qo_ref.dtype)
            ko_ref[:, lo:lo + NOPE] = k_ref[:, lo:lo + NOPE]
            ko_ref[:, lo + NOPE:lo + QPAD] = kr
            vo_ref[:, lo:lo + VHEAD] = v_ref[:, h * VHEAD:(h + 1) * VHEAD]
            vo_ref[:, lo + VHEAD:lo + QPAD] = ones

    wide = pl.BlockSpec((tr, H * QPAD), lambda i: (i, 0))
    return pl.pallas_call(
        body, name=name,
        out_shape=[jax.ShapeDtypeStruct(qpad.shape, MXU)] * 3,
        grid=(T // tr,),
        in_specs=[wide, wide, pl.BlockSpec((tr, H * VHEAD), lambda i: (i, 0)),
                  pl.BlockSpec((tr, LANE), lambda i: (i, kr_blk)),
                  pl.BlockSpec((tr, 1), lambda i: (i, 0)), pl.BlockSpec((1, LANE), lambda i: (0, 0))],
        out_specs=[wide, wide, wide], compiler_params=_cp("parallel"),
    )(qpad, kpad, v, P, pos, invf)


def _rope_bwd(dqcat, dkcat, pos, invf, H, name):
    T = dqcat.shape[0]
    tr = _tile(T, ROW_TILE, SUBLANE)

    def body(dq_ref, dk_ref, pos_ref, invf_ref, dqo_ref, dkr_ref):
        c, s1, s2 = _rope_tables(pos_ref, invf_ref)

        def rot_t(v):
            return v * c + pltpu.roll(v * s1, ROPE // 2, 1) + pltpu.roll(v * s2, LANE - ROPE // 2, 1)

        acc = jnp.zeros((tr, LANE), F32)
        for h in range(H):
            lo = h * QPAD
            dqo_ref[:, lo:lo + NOPE] = dq_ref[:, lo:lo + NOPE]
            dqo_ref[:, lo + NOPE:lo + QPAD] = rot_t(dq_ref[:, lo + NOPE:lo + QPAD].astype(F32)).astype(dqo_ref.dtype)
            acc = acc + dk_ref[:, lo + NOPE:lo + QPAD].astype(F32)
        dkr_ref[...] = rot_t(acc)

    wide = pl.BlockSpec((tr, H * QPAD), lambda i: (i, 0))
    return pl.pallas_call(
        body, name=name,
        out_shape=[jax.ShapeDtypeStruct(dqcat.shape, MXU), jax.ShapeDtypeStruct((T, LANE), F32)],
        grid=(T // tr,),
        in_specs=[wide, wide, pl.BlockSpec((tr, 1), lambda i: (i, 0)), pl.BlockSpec((1, LANE), lambda i: (0, 0))],
        out_specs=[wide, pl.BlockSpec((tr, LANE), lambda i: (i, 0))], compiler_params=_cp("parallel"),
    )(dqcat, dkcat, pos, invf)


def _causal_mask(s):
    row = lax.broadcasted_iota(jnp.int32, s.shape, 0)
    col = lax.broadcasted_iota(jnp.int32, s.shape, 1)
    return jnp.where(col <= row, s, -1e30)


def _flash_fwd(qcat, kcat, v, P, z_blk, H, name, hosts=None):
    T = qcat.shape[0]
    tq = _tile(T, ATT_TILE, LANE)
    hp = 2 if H % 2 == 0 and z_blk % 2 == 0 else 1
    c2 = ATT_SCALE * LOG2E

    def body(q_ref, k_ref, v_ref, z_ref, o_ref, a_ref, lse_ref, m_s, acc_s):
        i = pl.program_id(1)
        m_s[...] = jnp.full_like(m_s, -1e30)
        acc_s[...] = jnp.zeros_like(acc_s)

        def step(j, masked):
            r0 = pl.multiple_of(j * tq, tq)
            for h in range(hp):
                q = q_ref[:, h * QPAD:(h + 1) * QPAD]
                kb = k_ref[pl.ds(r0, tq), h * QPAD:(h + 1) * QPAD]
                vb = v_ref[pl.ds(r0, tq), h * QPAD:(h + 1) * QPAD]
                s = lax.dot_general(q, kb, NT, preferred_element_type=F32)
                if masked:
                    s = _causal_mask(s)
                m_prev = m_s[h]
                m_new = jnp.maximum(m_prev, jnp.max(s, axis=1, keepdims=True))
                p = jnp.exp2((s - m_new) * c2).astype(MXU)
                alpha = jnp.exp2((m_prev - m_new) * c2)
                acc_s[h] = alpha * acc_s[h] + lax.dot_general(p, vb, NN, preferred_element_type=F32)
                m_s[h] = m_new

        def loop(j, carry):
            step(j, False)
            return carry

        lax.fori_loop(0, i, loop, 0)
        step(i, True)
        for h in range(hp):
            l = acc_s[h, :, VHEAD:QPAD]
            o = acc_s[h, :, 0:VHEAD] / l
            cols = slice(h * VHEAD, (h + 1) * VHEAD)
            o_ref[:, cols] = o.astype(o_ref.dtype)
            a_ref[:, cols] = (o * _silu(z_ref[:, cols].astype(F32))).astype(a_ref.dtype)
            lse_ref[h] = m_s[h] * ATT_SCALE + jnp.log(l)

    return _hosted(
        hosts, ATT_FWD_US * (T / 4096.0) ** 2 * (H / 32.0), body, name=name,
        out_shape=[jax.ShapeDtypeStruct((T, H * VHEAD), MXU), jax.ShapeDtypeStruct((T, H * VHEAD), MXU),
                   jax.ShapeDtypeStruct((H, T, LANE), F32)],
        grid=(H // hp, T // tq),
        in_specs=[pl.BlockSpec((tq, hp * QPAD), lambda h, i: (i, h)), pl.BlockSpec((T, hp * QPAD), lambda h, i: (0, h)),
                  pl.BlockSpec((T, hp * QPAD), lambda h, i: (0, h)),
                  pl.BlockSpec((tq, hp * VHEAD), lambda h, i: (i, z_blk // hp + h))],
        out_specs=[pl.BlockSpec((tq, hp * VHEAD), lambda h, i: (i, h)), pl.BlockSpec((tq, hp * VHEAD), lambda h, i: (i, h)),
                   pl.BlockSpec((hp, tq, LANE), lambda h, i: (h, i, 0))],
        scratch_shapes=[pltpu.VMEM((hp, tq, 1), F32), pltpu.VMEM((hp, tq, QPAD), F32)],
        args=[qcat, kcat, v, P], semantics=("parallel", "arbitrary"))


def _flash_bwd(qcat, kcat, v, do, lse, delta, H, name, hosts=None):
    T = qcat.shape[0]
    tq = _tile(T, ATT_TILE, LANE)
    nq = T // tq
    c2 = ATT_SCALE * LOG2E

    def body(q_ref, do_ref, lse_ref, dl_ref, k_ref, v_ref, dq_ref, dk_ref, dv_ref, dq_s, dk_s, dv_s):
        j = pl.program_id(1)
        kb, vb = k_ref[...], v_ref[...]
        dk_s[...] = jnp.zeros_like(dk_s)
        dv_s[...] = jnp.zeros_like(dv_s)

        @pl.when(j == 0)
        def _():
            dq_s[...] = jnp.zeros_like(dq_s)

        def step(i, masked):
            rows = pl.ds(pl.multiple_of(i * tq, tq), tq)
            qb = q_ref[rows, :]
            dob = do_ref[rows, :]
            s = lax.dot_general(qb, kb, NT, preferred_element_type=F32)
            if masked:
                s = _causal_mask(s)
            p = jnp.exp2(s * c2 - lse_ref[rows, 0:1] * LOG2E)
            dv_s[...] += lax.dot_general(p.astype(dob.dtype), dob, TN, preferred_element_type=F32)
            dp = lax.dot_general(dob, vb, NT, preferred_element_type=F32)
            ds = (p * (dp - dl_ref[rows, 0:1]) * ATT_SCALE).astype(qb.dtype)
            dk_s[...] += lax.dot_general(ds, qb, TN, preferred_element_type=F32)
            dq_s[rows, :] += lax.dot_general(ds, kb, NN, preferred_element_type=F32)

        def loop(i, carry):
            step(i, False)
            return carry

        step(j, True)
        lax.fori_loop(j + 1, nq, loop, 0)
        dk_ref[...] = dk_s[...].astype(dk_ref.dtype)
        dv_ref[...] = dv_s[...].astype(dv_ref.dtype)

        @pl.when(j == nq - 1)
        def _():
            dq_ref[...] = dq_s[...].astype(dq_ref.dtype)

    return _hosted(
        hosts, ATT_BWD_US * (T / 4096.0) ** 2 * (H / 32.0), body, name=name,
        out_shape=[jax.ShapeDtypeStruct(qcat.shape, MXU), jax.ShapeDtypeStruct(kcat.shape, MXU),
                   jax.ShapeDtypeStruct(v.shape, MXU)],
        grid=(H, nq),
        in_specs=[pl.BlockSpec((T, QPAD), lambda h, j: (0, h)), pl.BlockSpec((T, VHEAD), lambda h, j: (0, h)),
                  pl.BlockSpec((None, T, LANE), lambda h, j: (h, 0, 0)), pl.BlockSpec((None, T, LANE), lambda h, j: (h, 0, 0)),
                  pl.BlockSpec((tq, QPAD), lambda h, j: (j, h)), pl.BlockSpec((tq, VHEAD), lambda h, j: (j, h))],
        out_specs=[pl.BlockSpec((T, QPAD), lambda h, j: (0, h)), pl.BlockSpec((tq, QPAD), lambda h, j: (j, h)),
                   pl.BlockSpec((tq, VHEAD), lambda h, j: (j, h))],
        scratch_shapes=[pltpu.VMEM((T, QPAD), F32), pltpu.VMEM((tq, QPAD), F32), pltpu.VMEM((tq, VHEAD), F32)],
        args=[qcat, do, lse, delta, kcat, v], semantics=("parallel", "arbitrary"))


def _mla_gate_bwd(da, o, P, pa, H, name):
    T, HV = da.shape
    tr = _tile(T, ROW_TILE, SUBLANE)
    cw = _tile(math.gcd(pa, HV), 512, LANE)
    hb = cw // VHEAD

    def body(da_ref, o_ref, z_ref, do_ref, dz_ref, dl_ref):
        dav, ov, zv = da_ref[...].astype(F32), o_ref[...].astype(F32), z_ref[...].astype(F32)
        dov = dav * _silu(zv)
        do_ref[...] = dov.astype(do_ref.dtype)
        dz_ref[...] = (dav * ov * _silu_grad(zv)).astype(dz_ref.dtype)
        prod = dov * ov
        for h in range(hb):
            dl_ref[h] = jnp.broadcast_to(jnp.sum(prod[:, h * VHEAD:(h + 1) * VHEAD], axis=1, keepdims=True), (tr, LANE))

    blk = pl.BlockSpec((tr, cw), lambda i, j: (i, j))
    shifted = pl.BlockSpec((tr, cw), lambda i, j: (i, pa // cw + j))
    return pl.pallas_call(
        body, name=name,
        out_shape=[jax.ShapeDtypeStruct((T, HV), MXU), jax.ShapeDtypeStruct((T, HV), MXU),
                   jax.ShapeDtypeStruct((H, T, LANE), F32)],
        grid=(T // tr, HV // cw),
        in_specs=[blk, blk, shifted],
        out_specs=[blk, blk, pl.BlockSpec((hb, tr, LANE), lambda i, j: (j, i, 0))],
        compiler_params=_cp("parallel", "parallel"),
    )(da, o, P)


def _bank_rows(width):
    return min(SUBLANE, width), -(-(width - 1) // SUBLANE) * SUBLANE


def _bank_fill(bank_ref, val, width, T, causal):
    nr, hp = _bank_rows(width)
    rows = lax.broadcasted_iota(jnp.int32, val.shape, 0)
    zero = jnp.zeros((hp,) + val.shape[1:], F32)
    for r in range(nr):
        if causal:
            bank_ref[r, 0:hp, :] = zero
            bank_ref[r, hp:hp + T, :] = val if r == 0 else jnp.where(rows >= r, pltpu.roll(val, r, 0), 0.0)
        else:
            bank_ref[r, T:T + hp, :] = zero
            bank_ref[r, 0:T, :] = val if r == 0 else jnp.where(rows < T - r, pltpu.roll(val, T - r, 0), 0.0)


def _bank_tap(bank_ref, s, t0, tc, width, causal):
    _, hp = _bank_rows(width)
    q, r = divmod(s, SUBLANE)
    off = hp - SUBLANE * q if causal else SUBLANE * q
    return bank_ref[r, pl.ds(pl.multiple_of(t0 + off, SUBLANE), tc), :]


def _conv_chunk(bank_ref, w_ref, t0, tc, width):
    acc = None
    for s in range(width):
        k = width - 1 - s
        term = w_ref[k:k + 1, :] * _bank_tap(bank_ref, s, t0, tc, width, True)
        acc = term if acc is None else acc + term
    return acc


def _conv_bwd_chunk(bank_ref, w_ref, xin, dw_ref, t0, tc, width):
    acc = None
    for s in range(width):
        k = width - 1 - s
        tap = _bank_tap(bank_ref, s, t0, tc, width, False)
        term = w_ref[k:k + 1, :] * tap
        acc = term if acc is None else acc + term
        dw_ref[k] += jnp.sum((tap * xin).reshape(tc // SUBLANE, SUBLANE, xin.shape[1]), axis=0)
    return acc


def _full_t(T, cb, col):
    return pl.BlockSpec((T, cb), lambda j, col=col: (0, col + j))


def _sc_fwd(P, w, D, name):
    T = P.shape[0]
    cb = _tile(D, CONV_CB, LANE)
    nb = D // cb
    width = w.shape[0]
    tc = _tile(T, CONV_CHUNK, SUBLANE)
    nr, hp = _bank_rows(width)

    def body(b_ref, c_ref, u_ref, z_ref, w_ref, a_ref, bank):
        _bank_fill(bank, c_ref[...].astype(F32) * u_ref[...].astype(F32), width, T, True)

        def chunk(ci, carry):
            t0 = pl.multiple_of(ci * tc, tc)
            rows = pl.ds(t0, tc)
            v = _conv_chunk(bank, w_ref, t0, tc, width)
            a_ref[rows, :] = (b_ref[rows, :].astype(F32) * v * _silu(z_ref[rows, :].astype(F32))).astype(a_ref.dtype)
            return carry

        lax.fori_loop(0, T // tc, chunk, 0)

    return pl.pallas_call(
        body, name=name, out_shape=jax.ShapeDtypeStruct((T, D), MXU), grid=(nb,),
        in_specs=[_full_t(T, cb, 0), _full_t(T, cb, nb), _full_t(T, cb, 2 * nb), _full_t(T, cb, 3 * nb),
                  pl.BlockSpec((width, cb), lambda j: (0, j))],
        out_specs=_full_t(T, cb, 0),
        scratch_shapes=[pltpu.VMEM((nr, T + hp, cb), F32)],
        compiler_params=_cp("parallel"),
    )(P, P, P, P, w)


def _sc_bwd(P, da, w, D, name):
    T = P.shape[0]
    cb = _tile(D, CONV_CB, LANE)
    nb = D // cb
    width = w.shape[0]
    tc = _tile(T, CONV_CHUNK, SUBLANE)
    nr, hp = _bank_rows(width)

    def body(b_ref, c_ref, u_ref, z_ref, da_ref, w_ref, dp_ref, dw_ref, bank, dv_s, dw_s):
        _bank_fill(bank, c_ref[...].astype(F32) * u_ref[...].astype(F32), width, T, True)

        def first(ci, carry):
            t0 = pl.multiple_of(ci * tc, tc)
            rows = pl.ds(t0, tc)
            v = _conv_chunk(bank, w_ref, t0, tc, width)
            bv, zv, dav = b_ref[rows, :].astype(F32), z_ref[rows, :].astype(F32), da_ref[rows, :].astype(F32)
            dyb = dav * _silu(zv)
            dp_ref[3, rows, :] = (dav * bv * v * _silu_grad(zv)).astype(dp_ref.dtype)
            dp_ref[0, rows, :] = (dyb * v).astype(dp_ref.dtype)
            dv_s[rows, :] = dyb * bv
            return carry

        lax.fori_loop(0, T // tc, first, 0)
        _bank_fill(bank, dv_s[...], width, T, False)
        dw_s[...] = jnp.zeros_like(dw_s)

        def second(ci, carry):
            t0 = pl.multiple_of(ci * tc, tc)
            rows = pl.ds(t0, tc)
            cv, uv = c_ref[rows, :].astype(F32), u_ref[rows, :].astype(F32)
            dcu = _conv_bwd_chunk(bank, w_ref, cv * uv, dw_s, t0, tc, width)
            dp_ref[1, rows, :] = (dcu * uv).astype(dp_ref.dtype)
            dp_ref[2, rows, :] = (dcu * cv).astype(dp_ref.dtype)
            return carry

        lax.fori_loop(0, T // tc, second, 0)
        dw_ref[...] = jnp.sum(dw_s[...], axis=1)

    return pl.pallas_call(
        body, name=name,
        out_shape=[jax.ShapeDtypeStruct((4, T, D), MXU), jax.ShapeDtypeStruct((width, D), F32)],
        grid=(nb,),
        in_specs=[_full_t(T, cb, 0), _full_t(T, cb, nb), _full_t(T, cb, 2 * nb), _full_t(T, cb, 3 * nb),
                  _full_t(T, cb, 0), pl.BlockSpec((width, cb), lambda j: (0, j))],
        out_specs=[pl.BlockSpec((4, T, cb), lambda j: (0, 0, j)), pl.BlockSpec((width, cb), lambda j: (0, j))],
        scratch_shapes=[pltpu.VMEM((nr, T + hp, cb), F32), pltpu.VMEM((T, cb), F32),
                        pltpu.VMEM((width, SUBLANE, cb), F32)],
        compiler_params=_cp("parallel"),
    )(P, P, P, P, da, w)


def _cf_conv_fwd(P, w, bias, D, name):
    T = P.shape[0]
    cb = _tile(D, CONV_CB, LANE)
    nb = D // cb
    width = w.shape[0]
    tc = _tile(T, CONV_CHUNK, SUBLANE)
    nr, hp = _bank_rows(width)

    def body(a_ref, g_ref, w_ref, b_ref, y_ref, bank):
        _bank_fill(bank, a_ref[...].astype(F32) * _sig(g_ref[...].astype(F32)), width, T, True)

        def chunk(ci, carry):
            t0 = pl.multiple_of(ci * tc, tc)
            y_ref[pl.ds(t0, tc), :] = (_conv_chunk(bank, w_ref, t0, tc, width) + b_ref[...]).astype(y_ref.dtype)
            return carry

        lax.fori_loop(0, T // tc, chunk, 0)

    return pl.pallas_call(
        body, name=name, out_shape=jax.ShapeDtypeStruct((T, D), F32), grid=(nb,),
        in_specs=[_full_t(T, cb, 0), _full_t(T, cb, nb), pl.BlockSpec((width, cb), lambda j: (0, j)),
                  pl.BlockSpec((1, cb), lambda j: (0, j))],
        out_specs=_full_t(T, cb, 0),
        scratch_shapes=[pltpu.VMEM((nr, T + hp, cb), F32)],
        compiler_params=_cp("parallel"),
    )(P, P, w, bias)


def _cf_conv_bwd(P, dyc, w, D, name):
    T = P.shape[0]
    cb = _tile(D, CONV_CB, LANE)
    nb = D // cb
    width = w.shape[0]
    tc = _tile(T, CONV_CHUNK, SUBLANE)
    nr, hp = _bank_rows(width)

    def body(a_ref, g_ref, dy_ref, w_ref, da_ref, dg_ref, dw_ref, db_ref, bank, dw_s):
        dyv = dy_ref[...]
        db_ref[...] = jnp.sum(dyv, axis=0, keepdims=True)
        _bank_fill(bank, dyv, width, T, False)
        dw_s[...] = jnp.zeros_like(dw_s)

        def chunk(ci, carry):
            t0 = pl.multiple_of(ci * tc, tc)
            rows = pl.ds(t0, tc)
            av, sg = a_ref[rows, :].astype(F32), _sig(g_ref[rows, :].astype(F32))
            dyg = _conv_bwd_chunk(bank, w_ref, av * sg, dw_s, t0, tc, width)
            da_ref[rows, :] = (dyg * sg).astype(da_ref.dtype)
            dg_ref[rows, :] = (dyg * av * sg * (1.0 - sg)).astype(dg_ref.dtype)
            return carry

        lax.fori_loop(0, T // tc, chunk, 0)
        dw_ref[...] = jnp.sum(dw_s[...], axis=1)

    return pl.pallas_call(
        body, name=name,
        out_shape=[jax.ShapeDtypeStruct((T, D), MXU), jax.ShapeDtypeStruct((T, D), MXU),
                   jax.ShapeDtypeStruct((width, D), F32), jax.ShapeDtypeStruct((1, D), F32)],
        grid=(nb,),
        in_specs=[_full_t(T, cb, 0), _full_t(T, cb, nb), _full_t(T, cb, 0),
                  pl.BlockSpec((width, cb), lambda j: (0, j))],
        out_specs=[_full_t(T, cb, 0), _full_t(T, cb, 0), pl.BlockSpec((width, cb), lambda j: (0, j)),
                   pl.BlockSpec((1, cb), lambda j: (0, j))],
        scratch_shapes=[pltpu.VMEM((nr, T + hp, cb), F32), pltpu.VMEM((width, SUBLANE, cb), F32)],
        compiler_params=_cp("parallel"),
    )(P, P, dyc, w)


def _layer_norm_stats(v):
    mu = jnp.mean(v, axis=-1, keepdims=True)
    cen = v - mu
    rstd = lax.rsqrt(jnp.mean(cen * cen, axis=-1, keepdims=True) + LN_EPS)
    return cen * rstd, rstd


def _layer_norm_bwd(dxh, xh, rstd):
    return rstd * (dxh - jnp.mean(dxh, axis=-1, keepdims=True) - xh * jnp.mean(dxh * xh, axis=-1, keepdims=True))


def _cf_act_fwd(yc, P, lg, lb, D, name):
    T = yc.shape[0]
    tr = _tile(T, ROW_TILE, SUBLANE)

    def body(y_ref, z_ref, g_ref, b_ref, o_ref):
        xh, _ = _layer_norm_stats(y_ref[...])
        yl = xh * g_ref[...] + b_ref[...]
        o_ref[...] = (_silu(yl) * _silu(z_ref[...].astype(F32))).astype(o_ref.dtype)

    row = pl.BlockSpec((tr, D), lambda i: (i, 0))
    vec = pl.BlockSpec((1, D), lambda i: (0, 0))
    return pl.pallas_call(
        body, name=name, out_shape=jax.ShapeDtypeStruct((T, D), MXU), grid=(T // tr,),
        in_specs=[row, pl.BlockSpec((tr, D), lambda i: (i, 2)), vec, vec], out_specs=row,
        compiler_params=_cp("parallel"),
    )(yc, P, lg, lb)


def _cf_act_bwd(yc, P, da, lg, lb, D, name):
    T = yc.shape[0]
    tr = _tile(T, ROW_TILE, SUBLANE)

    def body(y_ref, z_ref, da_ref, g_ref, b_ref, dy_ref, dz_ref, dg_ref, db_ref):
        @pl.when(pl.program_id(0) == 0)
        def _():
            dg_ref[...] = jnp.zeros_like(dg_ref)
            db_ref[...] = jnp.zeros_like(db_ref)

        xh, rstd = _layer_norm_stats(y_ref[...])
        yl = xh * g_ref[...] + b_ref[...]
        zv, dav = z_ref[...].astype(F32), da_ref[...].astype(F32)
        dz_ref[...] = (dav * _silu(yl) * _silu_grad(zv)).astype(dz_ref.dtype)
        dyl = dav * _silu(zv) * _silu_grad(yl)
        dg_ref[...] += jnp.sum(dyl * xh, axis=0, keepdims=True)
        db_ref[...] += jnp.sum(dyl, axis=0, keepdims=True)
        dy_ref[...] = _layer_norm_bwd(dyl * g_ref[...], xh, rstd)

    row = pl.BlockSpec((tr, D), lambda i: (i, 0))
    zcol = pl.BlockSpec((tr, D), lambda i: (i, 2))
    vec = pl.BlockSpec((1, D), lambda i: (0, 0))
    return pl.pallas_call(
        body, name=name,
        out_shape=[jax.ShapeDtypeStruct((T, D), F32), jax.ShapeDtypeStruct((T, D), MXU),
                   jax.ShapeDtypeStruct((1, D), F32), jax.ShapeDtypeStruct((1, D), F32)],
        grid=(T // tr,), in_specs=[row, zcol, row, vec, vec], out_specs=[row, row, vec, vec],
        compiler_params=_cp("arbitrary"),
    )(yc, P, da, lg, lb)


def _tril(w):
    row = lax.broadcasted_iota(jnp.int32, w.shape, 0)
    col = lax.broadcasted_iota(jnp.int32, w.shape, 1)
    return jnp.where(col <= row, w, 0.0)


def _gm_fwd(P, lg, lb, ws, bst, D, name):
    T = P.shape[0]
    G, ch, _ = ws.shape
    gw = D // G

    def body(p_ref, g_ref, b_ref, ws_ref, bs_ref, a_ref):
        uv, vv, zv = (p_ref[:, k * D:(k + 1) * D].astype(F32) for k in range(3))
        xh, _ = _layer_norm_stats(_gelu(vv))
        vn = (xh * g_ref[...] + b_ref[...]).astype(MXU)
        gate = _gelu(uv) * _silu(zv)
        for g in range(G):
            cols = slice(g * gw, (g + 1) * gw)
            s = lax.dot_general(_tril(ws_ref[g]).astype(MXU), vn[:, cols], NN, preferred_element_type=F32)
            a_ref[:, cols] = (gate[:, cols] * (s + bs_ref[:, g:g + 1])).astype(a_ref.dtype)

    vec = pl.BlockSpec((1, D), lambda i: (0, 0))
    return pl.pallas_call(
        body, name=name, out_shape=jax.ShapeDtypeStruct((T, D), MXU), grid=(T // ch,),
        in_specs=[pl.BlockSpec((ch, 3 * D), lambda i: (i, 0)), vec, vec,
                  pl.BlockSpec((G, ch, ch), lambda i: (0, 0, 0)), pl.BlockSpec((ch, G), lambda i: (0, 0))],
        out_specs=pl.BlockSpec((ch, D), lambda i: (i, 0)), compiler_params=_cp("parallel"),
    )(P, lg, lb, ws, bst)


def _gm_bwd(P, da, lg, lb, ws, bst, D, name):
    T = P.shape[0]
    G, ch, _ = ws.shape
    gw = D // G

    def body(p_ref, da_ref, g_ref, b_ref, ws_ref, bs_ref, dp_ref, dg_ref, db_ref, dws_ref, dbs_ref, dvn_s):
        @pl.when(pl.program_id(0) == 0)
        def _():
            dg_ref[...] = jnp.zeros_like(dg_ref)
            db_ref[...] = jnp.zeros_like(db_ref)
            dws_ref[...] = jnp.zeros_like(dws_ref)
            dbs_ref[...] = jnp.zeros_like(dbs_ref)

        uv, vv, zv = (p_ref[:, k * D:(k + 1) * D].astype(F32) for k in range(3))
        dav = da_ref[...].astype(F32)
        xh, rstd = _layer_norm_stats(_gelu(vv))
        vn = (xh * g_ref[...] + b_ref[...]).astype(MXU)
        ug, sz = _gelu(uv), _silu(zv)
        ds_all = dav * sz * ug
        for g in range(G):
            cols = slice(g * gw, (g + 1) * gw)
            wm = _tril(ws_ref[g]).astype(MXU)
            s = lax.dot_general(wm, vn[:, cols], NN, preferred_element_type=F32) + bs_ref[:, g:g + 1]
            dp_ref[:, g * gw:(g + 1) * gw] = (dav[:, cols] * sz[:, cols] * s * _gelu_grad(uv[:, cols])).astype(dp_ref.dtype)
            dp_ref[:, 2 * D + g * gw:2 * D + (g + 1) * gw] = (
                dav[:, cols] * ug[:, cols] * s * _silu_grad(zv[:, cols])).astype(dp_ref.dtype)
            ds = ds_all[:, cols]
            dsb = ds.astype(MXU)
            dvn_s[:, cols] = lax.dot_general(wm, dsb, TN, preferred_element_type=F32)
            dws_ref[g] += _tril(lax.dot_general(dsb, vn[:, cols], NT, preferred_element_type=F32))
            dbs_ref[g] += jnp.broadcast_to(jnp.sum(ds, axis=1, keepdims=True), (ch, LANE))
        dvn = dvn_s[...]
        dg_ref[...] += jnp.sum(dvn * xh, axis=0, keepdims=True)
        db_ref[...] += jnp.sum(dvn, axis=0, keepdims=True)
        dp_ref[:, D:2 * D] = (_layer_norm_bwd(dvn * g_ref[...], xh, rstd) * _gelu_grad(vv)).astype(dp_ref.dtype)

    vec = pl.BlockSpec((1, D), lambda i: (0, 0))
    return pl.pallas_call(
        body, name=name,
        out_shape=[jax.ShapeDtypeStruct((T, 3 * D), MXU), jax.ShapeDtypeStruct((1, D), F32), jax.ShapeDtypeStruct((1, D), F32),
                   jax.ShapeDtypeStruct((G, ch, ch), F32), jax.ShapeDtypeStruct((G, ch, LANE), F32)],
        grid=(T // ch,),
        in_specs=[pl.BlockSpec((ch, 3 * D), lambda i: (i, 0)), pl.BlockSpec((ch, D), lambda i: (i, 0)), vec, vec,
                  pl.BlockSpec((G, ch, ch), lambda i: (0, 0, 0)), pl.BlockSpec((ch, G), lambda i: (0, 0))],
        out_specs=[pl.BlockSpec((ch, 3 * D), lambda i: (i, 0)), vec, vec,
                   pl.BlockSpec((G, ch, ch), lambda i: (0, 0, 0)), pl.BlockSpec((G, ch, LANE), lambda i: (0, 0, 0))],
        scratch_shapes=[pltpu.VMEM((ch, D), F32)],
        compiler_params=_cp("arbitrary"),
    )(P, da, lg, lb, ws, bst)


def _adam_math(w, g, m, v):
    m = ADAM_B1 * m + (1.0 - ADAM_B1) * g
    v = ADAM_B2 * v + (1.0 - ADAM_B2) * (g * g)
    m_hat = m / (1.0 - ADAM_B1 ** ADAM_STEP)
    v_hat = v / (1.0 - ADAM_B2 ** ADAM_STEP)
    return -ADAM_LR * (m_hat / (jnp.sqrt(v_hat) + ADAM_EPS) + ADAM_WD * w), m, v


def _adam(w, m, v, gparts, name, hosts=None):
    R, C = w.shape
    n = gparts.shape[0]
    tr = _tile(R, max(SUBLANE, (1 << 18) // C), 16 if gparts.dtype != F32 else SUBLANE)

    def body(w_ref, m_ref, v_ref, gp_ref, g_ref, d_ref, mo_ref, vo_ref):
        g = gp_ref[0].astype(F32)
        for k in range(1, n):
            g = g + gp_ref[k].astype(F32)
        g_ref[...] = g
        d_ref[...], mo_ref[...], vo_ref[...] = _adam_math(w_ref[...], g, m_ref[...], v_ref[...])

    blk = pl.BlockSpec((tr, C), lambda i: (i, 0))
    return _hosted(
        hosts, R * C * 36.0 / HBM_BYTES_PER_US, body, force=True, name=name,
        out_shape=[jax.ShapeDtypeStruct((R, C), F32)] * 4, grid=(R // tr,),
        in_specs=[blk, blk, blk, pl.BlockSpec((n, tr, C), lambda i: (0, i, 0))], out_specs=[blk] * 4,
        args=[w, m, v, gparts], semantics=("parallel",))


def _sum_blocks(parts, name):
    n, R, C = parts.shape
    tr = _tile(R, 512, SUBLANE)

    def body(p_ref, o_ref):
        acc = p_ref[0]
        for k in range(1, n):
            acc = acc + p_ref[k]
        o_ref[...] = acc

    return pl.pallas_call(
        body, name=name, out_shape=jax.ShapeDtypeStruct((R, C), F32), grid=(R // tr,),
        in_specs=[pl.BlockSpec((n, tr, C), lambda i: (0, i, 0))], out_specs=pl.BlockSpec((tr, C), lambda i: (i, 0)),
        compiler_params=_cp("parallel"),
    )(parts)


def _pack_rows(shape):
    return -(-int(np.prod(shape)) // (SUBLANE * LANE)) * SUBLANE


def _pack(arrs):
    parts = []
    for a in arrs:
        flat = a.reshape(-1).astype(F32)
        rows = _pack_rows(a.shape)
        parts.append(jnp.pad(flat, (0, rows * LANE - flat.shape[0])).reshape(rows, LANE))
    return jnp.concatenate(parts, axis=0)


def _unpack(buf, shapes):
    out, off = [], 0
    for s in shapes:
        rows, n = _pack_rows(s), int(np.prod(s))
        out.append(buf[off:off + rows].reshape(-1)[:n].reshape(s))
        off += rows
    return out


def _cols_full(g):
    return jnp.transpose(g, (1, 0, 2)).reshape(g.shape[1], N_DEV * g.shape[2])


def _cols_blocks(w):
    R, N = w.shape
    return jnp.transpose(w.reshape(R, N_DEV, N // N_DEV), (1, 0, 2)).astype(WIRE)


def _rows_blocks(w):
    return w.reshape(N_DEV, w.shape[0] // N_DEV, w.shape[1]).astype(WIRE)


def kernel(x, positions, norm_pre, norm_post, w_in_mla, mla_q_norm, w_uq, mla_kv_norm, w_ukv, w_out_mla, w_in_sc, sc_conv, w_out_sc, w_in_gm, gm_ln_g, gm_ln_b, gm_w_s, gm_b_s, w_out_gm, w_in_cf, cf_dw, cf_dw_b, cf_ln_g, cf_ln_b, w_out_cf, loss_target, m_norm_pre, m_norm_post, m_w_in_mla, m_mla_q_norm, m_w_uq, m_mla_kv_norm, m_w_ukv, m_w_out_mla, m_w_in_sc, m_sc_conv, m_w_out_sc, m_w_in_gm, m_gm_ln_g, m_gm_ln_b, m_gm_w_s, m_gm_b_s, m_w_out_gm, m_w_in_cf, m_cf_dw, m_cf_dw_b, m_cf_ln_g, m_cf_ln_b, m_w_out_cf, v_norm_pre, v_norm_post, v_w_in_mla, v_mla_q_norm, v_w_uq, v_mla_kv_norm, v_w_ukv, v_w_out_mla, v_w_in_sc, v_sc_conv, v_w_out_sc, v_w_in_gm, v_gm_ln_g, v_gm_ln_b, v_gm_w_s, v_gm_b_s, v_w_out_gm, v_w_in_cf, v_cf_dw, v_cf_dw_b, v_cf_ln_g, v_cf_ln_b, v_w_out_cf):
    names = ['norm_pre', 'norm_post', 'w_in_mla', 'mla_q_norm', 'w_uq', 'mla_kv_norm', 'w_ukv', 'w_out_mla', 'w_in_sc',
             'sc_conv', 'w_out_sc', 'w_in_gm', 'gm_ln_g', 'gm_ln_b', 'gm_w_s', 'gm_b_s', 'w_out_gm', 'w_in_cf', 'cf_dw',
             'cf_dw_b', 'cf_ln_g', 'cf_ln_b', 'w_out_cf']
    W = dict(zip(names, (norm_pre, norm_post, w_in_mla, mla_q_norm, w_uq, mla_kv_norm, w_ukv, w_out_mla, w_in_sc, sc_conv,
                         w_out_sc, w_in_gm, gm_ln_g, gm_ln_b, gm_w_s, gm_b_s, w_out_gm, w_in_cf, cf_dw, cf_dw_b, cf_ln_g,
                         cf_ln_b, w_out_cf)))
    Mo = dict(zip(names, (m_norm_pre, m_norm_post, m_w_in_mla, m_mla_q_norm, m_w_uq, m_mla_kv_norm, m_w_ukv, m_w_out_mla,
                          m_w_in_sc, m_sc_conv, m_w_out_sc, m_w_in_gm, m_gm_ln_g, m_gm_ln_b, m_gm_w_s, m_gm_b_s, m_w_out_gm,
                          m_w_in_cf, m_cf_dw, m_cf_dw_b, m_cf_ln_g, m_cf_ln_b, m_w_out_cf)))
    Vo = dict(zip(names, (v_norm_pre, v_norm_post, v_w_in_mla, v_mla_q_norm, v_w_uq, v_mla_kv_norm, v_w_ukv, v_w_out_mla,
                          v_w_in_sc, v_sc_conv, v_w_out_sc, v_w_in_gm, v_gm_ln_g, v_gm_ln_b, v_gm_w_s, v_gm_b_s, v_w_out_gm,
                          v_w_in_cf, v_cf_dw, v_cf_dw_b, v_cf_ln_g, v_cf_ln_b, v_w_out_cf)))
    big = ['w_in_mla', 'w_uq', 'w_ukv', 'w_out_mla', 'w_in_sc', 'w_out_sc', 'w_in_gm', 'w_out_gm', 'w_in_cf', 'w_out_cf']
    row_sharded = {'w_out_mla', 'w_out_sc', 'w_out_gm', 'w_out_cf'}
    chan = ['sc_conv', 'gm_ln_g', 'gm_ln_b', 'cf_dw', 'cf_dw_b', 'cf_ln_g', 'cf_ln_b']
    repl = ['norm_pre', 'norm_post', 'mla_q_norm', 'mla_kv_norm', 'gm_w_s', 'gm_b_s']

    T, D = x.shape[1], x.shape[2]
    xs, tgt = x[0], loss_target[0]
    pos = positions.reshape(T, 1)
    qr, kvr = mla_q_norm.shape[-1], mla_kv_norm.shape[-1]
    H = w_uq.shape[-1] * N_DEV // (NOPE + ROPE)
    HV = H * VHEAD
    c3 = qr + kvr + ROPE
    pa = -(-c3 // 512) * 512
    assert qr % kvr == 0 and qr % LANE == 0 and kvr % LANE == 0 and pa >= qr + kvr + LANE
    mx, my, mc = _mesh_pos()
    me = 4 * mx + 2 * my + mc
    core = mc.astype(jnp.int32).reshape(1)
    cs = D // N_DEV

    def wire(nm):
        return W[nm][0].astype(WIRE)

    hosts = _Hosts()
    G = {}

    def gather(nm, name=None):
        shard = wire(nm)
        in_place = nm not in row_sharded and shard.shape[1] % LANE == 0
        job = _GatherJob([shard], cols=in_place)
        if name is None:
            hosts.add(job, lambda outs: G.update({nm: outs[0]}))
        else:
            G[nm] = _run_job(job, name)[0]

    def gathered_rows(nm):
        if nm not in G:
            hosts.flush("ag_late")
        return G[nm].reshape(G[nm].shape[0] * G[nm].shape[1], G[nm].shape[2])

    def gathered_cols(nm):
        if nm not in G:
            hosts.flush("ag_late")
        return G[nm] if G[nm].ndim == 2 else _cols_full(G[nm])

    gather('w_in_mla', "ag_mla")
    chan_rows = [W[nm][0].reshape(-1, cs) for nm in chan]
    chan_cnt = [r.shape[0] for r in chan_rows]
    chan_local = jnp.concatenate(chan_rows, axis=0)
    chan_pad = -chan_local.shape[0] % SUBLANE
    chan_full = _cols_full(_run_job(_GatherJob([jnp.pad(chan_local, ((0, chan_pad), (0, 0)))]), "ag_small")[0])
    offs = np.cumsum([0] + chan_cnt)
    CH = {nm: chan_full[offs[i]:offs[i + 1]] for i, nm in enumerate(chan)}

    w_in_full = gathered_cols('w_in_mla')
    w_cat = jnp.concatenate([w_in_full[:, :c3], jnp.zeros((D, pa - c3), WIRE), w_in_full[:, c3:]], axis=1)
    in_names = ['w_in_mla', 'w_in_sc', 'w_in_gm', 'w_in_cf']
    out_names = ['w_out_mla', 'w_out_sc', 'w_out_gm', 'w_out_cf']
    WIN, WOUT = [w_cat], []

    half = ROPE // 2
    invf_np = np.zeros((1, LANE), np.float32)
    invf_np[0, :ROPE] = np.tile(np.float32(ROPE_THETA) ** (-np.arange(half, dtype=np.float32) / np.float32(half)), 2)
    invf = jnp.asarray(invf_np)
    gm_ws = gm_w_s[0]
    gm_bst = jnp.transpose(gm_b_s[0])

    xin, hs, Ps, acts, ys, keep = [xs], [], [], [], [], {}
    h = _rms_fwd(xs, norm_pre[0:1], "pre0")
    for i in range(4):
        if i == 0:
            gather('w_uq')
            gather('w_ukv')
        else:
            WIN.append(gathered_cols(in_names[i]))
            gather(in_names[i + 1] if i < 3 else out_names[3])
        P = _mm(h, WIN[i], out_dtype=MXU, name=f"in{i}", hosts=hosts)
        if i == 0:
            w_uq_pad = jnp.pad(gathered_cols('w_uq').reshape(qr, H, NOPE + ROPE),
                               ((0, 0), (0, 0), (0, QPAD - NOPE - ROPE))).reshape(qr, H * QPAD)
            w_ukv3 = gathered_cols('w_ukv').reshape(kvr, H, NOPE + VHEAD)
            w_k_pad = jnp.pad(w_ukv3[:, :, :NOPE], ((0, 0), (0, 0), (0, QPAD - NOPE))).reshape(kvr, H * QPAD)
            w_v = w_ukv3[:, :, NOPE:].reshape(kvr, HV)
            cqn, ckvn = _mla_norms_fwd(P, mla_q_norm, mla_kv_norm, qr, kvr, "mla_norms")
            qpad = _mm(cqn, w_uq_pad, out_dtype=MXU, name="mla_q")
            kpad = _mm(ckvn, w_k_pad, out_dtype=MXU, name="mla_k")
            vv = _mm(ckvn, w_v, out_dtype=MXU, name="mla_v")
            qcat, kcat, vcat = _rope_fwd(qpad, kpad, vv, P, pos, invf, (qr + kvr) // LANE, H, "rope")
            for nm in (out_names[0], in_names[1], out_names[1]):
                gather(nm)
            o, act, lse = _flash_fwd(qcat, kcat, vcat, P, pa // VHEAD, H, "attn", hosts=hosts)
            keep.update(cqn=cqn, ckvn=ckvn, qcat=qcat, kcat=kcat, v=vv, o=o, lse=lse)
        elif i == 1:
            act = _sc_fwd(P, CH['sc_conv'], D, "sc_mix")
        elif i == 2:
            act = _gm_fwd(P, CH['gm_ln_g'], CH['gm_ln_b'], gm_ws, gm_bst, D, "gm_mix")
        else:
            yc = _cf_conv_fwd(P, CH['cf_dw'], CH['cf_dw_b'], D, "cf_conv")
            act = _cf_act_fwd(yc, P, CH['cf_ln_g'], CH['cf_ln_b'], D, "cf_act")
            keep.update(yc=yc)
        WOUT.append(gathered_rows(out_names[i]))
        if i == 1:
            gather(out_names[2])
        y = _mm(act, WOUT[i], out_dtype=F32, name=f"out{i}", hosts=hosts)
        hs.append(h), Ps.append(P), acts.append(act), ys.append(y)
        if i < 3:
            x_next, h = _post_fwd(y, norm_post[i:i + 1], xin[i], f"post{i}", next_gain=norm_pre[i + 1:i + 2])
            xin.append(x_next)
        else:
            dx, loss_part = _post_fwd(y, norm_post[i:i + 1], xin[i], f"post{i}", target=tgt)
    loss = lax.psum(loss_part[0, 0], ("x", "y", "c"))

    hosts.flush("ag_late")
    dnorm_pre, dnorm_post, small_g, RS = [None] * 4, [None] * 4, {}, {}
    for i in (3, 2, 1, 0):
        dy, dnorm_post[i] = _rms_bwd(ys[i], dx, norm_post[i:i + 1], f"post_bwd{i}", out_dtype=MXU)
        da = _mm(dy, WOUT[i], tb=True, out_dtype=MXU, name=f"out_bwd{i}", hosts=hosts)
        dw_out = _mm(acts[i], dy, ta=True, out_dtype=WIRE, name=f"out_dw{i}", hosts=hosts).reshape(N_DEV, -1, D)
        _queue_reduce_scatter(hosts, {out_names[i]: dw_out}, core, f"rs_out{i}", RS)
        P = Ps[i]
        if i == 3:
            dyc, d_z, small_g['cf_ln_g'], small_g['cf_ln_b'] = _cf_act_bwd(
                keep['yc'], P, da, CH['cf_ln_g'], CH['cf_ln_b'], D, "cf_act_bwd")
            d_a, d_g, small_g['cf_dw'], small_g['cf_dw_b'] = _cf_conv_bwd(P, dyc, CH['cf_dw'], D, "cf_conv_bwd")
            dP = jnp.concatenate([d_a, d_g, d_z], axis=1)
        elif i == 2:
            dP, small_g['gm_ln_g'], small_g['gm_ln_b'], small_g['gm_w_s'], dbs = _gm_bwd(
                P, da, CH['gm_ln_g'], CH['gm_ln_b'], gm_ws, gm_bst, D, "gm_mix_bwd")
            small_g['gm_b_s'] = dbs[:, :, 0]
        elif i == 1:
            dP, small_g['sc_conv'] = _sc_bwd(P, da, CH['sc_conv'], D, "sc_mix_bwd")
        else:
            do, d_z, delta = _mla_gate_bwd(da, keep['o'], P, pa, H, "gate_bwd")
            dqcat, dkcat, dv = _flash_bwd(keep['qcat'], keep['kcat'], keep['v'], do, keep['lse'], delta, H, "attn_bwd",
                                          hosts=hosts)
            dqpad, dkr = _rope_bwd(dqcat, dkcat, pos, invf, H, "rope_bwd")
            dcqn = _mm(dqpad, w_uq_pad, tb=True, out_dtype=F32, name="mla_q_bwd", hosts=hosts)
            dckvn_k = _mm(dkcat, w_k_pad, tb=True, out_dtype=F32, name="mla_k_bwd", hosts=hosts)
            dckvn_v = _mm(dv, w_v, tb=True, out_dtype=F32, name="mla_v_bwd", hosts=hosts)
            dw_uq_pad = _mm(keep['cqn'], dqpad, ta=True, out_dtype=F32, name="mla_q_dw", hosts=hosts)
            dw_k_pad = _mm(keep['ckvn'], dkcat, ta=True, out_dtype=F32, name="mla_k_dw", hosts=hosts)
            dw_v = _mm(keep['ckvn'], dv, ta=True, out_dtype=F32, name="mla_v_dw", hosts=hosts)
            dw_uq = _cols_blocks(dw_uq_pad.reshape(qr, H, QPAD)[:, :, :NOPE + ROPE].reshape(qr, H * (NOPE + ROPE)))
            dw_ukv = _cols_blocks(jnp.concatenate(
                [dw_k_pad.reshape(kvr, H, QPAD)[:, :, :NOPE], dw_v.reshape(kvr, H, VHEAD)], axis=2).reshape(kvr, H * (NOPE + VHEAD)))
            _queue_reduce_scatter(hosts, {'w_uq': dw_uq, 'w_ukv': dw_ukv}, core, "rs_lat", RS)
            d_pa, small_g['mla_q_norm'], small_g['mla_kv_norm'] = _mla_norms_bwd(
                P, dcqn, dckvn_k, dckvn_v, dkr, mla_q_norm, mla_kv_norm, qr, kvr, pa, "mla_norms_bwd")
            dP = jnp.concatenate([d_pa, d_z], axis=1)
        dh = _mm(dP, WIN[i], tb=True, out_dtype=MXU, name=f"in_bwd{i}", hosts=hosts)
        dx, dnorm_pre[i] = _rms_bwd(xin[i], dh, norm_pre[i:i + 1], f"pre_bwd{i}", resid=dx)
        if i == 0:
            dw_cat = _mm(hs[i], dP, ta=True, out_dtype=F32, name="in_dw0", hosts=hosts)
            dw_in = _cols_blocks(jnp.concatenate([dw_cat[:, :c3], dw_cat[:, pa:]], axis=1))
        else:
            dw_in = _mm(hs[i], dP, ta=True, out_dtype=WIRE, name=f"in_dw{i}", blocks=N_DEV, hosts=hosts)
        _queue_reduce_scatter(hosts, {in_names[i]: dw_in}, core, f"rs_in{i}", RS)
    grad_x = dx[None]

    small_g['norm_pre'] = jnp.concatenate(dnorm_pre, axis=0)
    small_g['norm_post'] = jnp.concatenate(dnorm_post, axis=0)
    small_names = repl + chan
    small_shapes = [(small_g[nm].shape) for nm in small_names]
    tail, big_out = {}, {}
    hosts.add(_GatherJob([_pack([small_g[nm] for nm in small_names])]), lambda outs: tail.update(grads=outs[0]))

    def adam_big(nm, carrier=None):
        if nm not in RS:
            hosts.flush("rs_late")
        big_out[nm] = [o[None] for o in _adam(W[nm][0], Mo[nm][0], Vo[nm][0], RS[nm], "adam_" + nm, hosts=carrier)]

    for nm in ('w_in_sc', 'w_in_gm', 'w_in_cf', 'w_out_sc', 'w_out_gm', 'w_out_cf'):
        adam_big(nm, hosts)
    hosts.flush("rs_late")
    for nm in ('w_in_mla', 'w_uq', 'w_ukv', 'w_out_mla'):
        adam_big(nm)
    totals = dict(zip(small_names, _unpack(_sum_blocks(tail['grads'], "sum_grads"), small_shapes)))
    local_g = []
    for nm in small_names:
        g = totals[nm]
        if nm in chan:
            g = lax.dynamic_slice_in_dim(g.reshape(-1, D), me * cs, cs, axis=1)
        local_g.append(g.reshape(W[nm].shape))
    local_shapes = [W[nm].shape for nm in small_names]
    sm = _adam(_pack([W[nm] for nm in small_names]), _pack([Mo[nm] for nm in small_names]),
               _pack([Vo[nm] for nm in small_names]), _pack(local_g)[None], "adam_small")
    small_out = [dict(zip(small_names, _unpack(buf, local_shapes))) for buf in sm]

    outs = [loss, grad_x]
    for k in range(4):
        outs += [big_out[nm][k] if nm in big_out else small_out[k][nm] for nm in names]
    return tuple(outs)
```

```python
import functools
import math

import numpy as np
import jax
import jax.numpy as jnp
from jax import lax
from jax.experimental import pallas as pl
from jax.experimental.pallas import tpu as pltpu

F32 = jnp.float32
MXU = jnp.bfloat16
WIRE = jnp.bfloat16
MESH = pl.DeviceIdType.MESH
N_DEV = 8

NORM_EPS = 1e-6
LN_EPS = 1e-5
ROPE_THETA = 10000.0
NOPE, ROPE, VHEAD = 128, 64, 128
QPAD = 256
ATT_SCALE = float((NOPE + ROPE) ** -0.5)
LOG2E = 1.4426950408889634
INV_SQRT2 = 0.7071067811865476
INV_SQRT_2PI = 0.3989422804014327
ADAM_LR, ADAM_B1, ADAM_B2, ADAM_EPS, ADAM_WD, ADAM_STEP = 0.001, 0.9, 0.999, 1e-08, 0.01, 10

LANE = 128
SUBLANE = 8
VMEM_LIMIT = 56 * 1024 * 1024
MM_TILE = 1024
MM_TK = 4096
ATT_TILE = 512
ROW_TILE = 128
CONV_CHUNK = 256
CONV_CB = 128
MXU_FLOP_PER_US = 9.2e8
HBM_BYTES_PER_US = 3.0e6
ATT_FWD_US, ATT_BWD_US = 1150.0, 1600.0
HOST_SLACK = 1.35

NT = (((1,), (1,)), ((), ()))
TN = (((0,), (0,)), ((), ()))
NN = (((1,), (0,)), ((), ()))


def _tile(dim, pref, align):
    t = min(pref, dim)
    t -= t % align
    while t >= align:
        if dim % t == 0:
            return t
        t -= align
    return dim


def _cp(*sem):
    return pltpu.CompilerParams(dimension_semantics=sem, vmem_limit_bytes=VMEM_LIMIT)


def _sig(v):
    return jax.nn.sigmoid(v)


def _silu(v):
    return v * _sig(v)


def _silu_grad(v):
    s = _sig(v)
    return s * (1.0 + v * (1.0 - s))


def _gelu(v):
    return 0.5 * v * (1.0 + lax.erf(v * INV_SQRT2))


def _gelu_grad(v):
    return 0.5 * (1.0 + lax.erf(v * INV_SQRT2)) + v * jnp.exp(-0.5 * v * v) * INV_SQRT_2PI


def _mesh_pos():
    return lax.axis_index("x"), lax.axis_index("y"), lax.axis_index("c")


def _other_chips(x, y):
    return [(1 - x, y), (x, 1 - y), (1 - x, 1 - y)]


class _GatherJob:
    US_PER_MB = 38.0

    def __init__(self, arrs, cols=False):
        n = len(arrs)
        self.ins = list(arrs)
        self.cols = cols
        if cols:
            assert all(a.ndim == 2 and a.shape[1] % LANE == 0 for a in arrs)
            self.out_shape = [jax.ShapeDtypeStruct((a.shape[0], N_DEV * a.shape[1]), a.dtype) for a in arrs]
        else:
            self.out_shape = [jax.ShapeDtypeStruct((N_DEV,) + a.shape, a.dtype) for a in arrs]
        self.sems = [pltpu.SemaphoreType.DMA((n, 7)), pltpu.SemaphoreType.DMA((n, 7)), pltpu.SemaphoreType.DMA((n,))]
        self.cost = self.US_PER_MB * sum(a.size * a.dtype.itemsize for a in arrs) / 1e6

    def _parts(self, ins, outs, sems):
        send_sems, recv_sems, loc_sems = sems
        x, y, c = _mesh_pos()
        n = len(ins)

        def block_of(a, dev):
            if not self.cols:
                return outs[a].at[dev]
            width = ins[a].shape[1]
            return outs[a].at[:, pl.ds(pl.multiple_of(dev * width, LANE), width)]

        def copy(a, k, block, to, src=None):
            dst = block_of(a, 4 * block[0] + 2 * block[1] + block[2])
            return pltpu.make_async_remote_copy(
                src_ref=dst if src is None else src, dst_ref=dst,
                send_sem=send_sems.at[a, k], recv_sem=recv_sems.at[a, k],
                device_id=to, device_id_type=MESH)

        me, sib = (x, y, c), (x, y, 1 - c)
        xn, yn, dg = (1 - x, y, c), (x, 1 - y, c), (1 - x, 1 - y, c)
        on_src = (x + (1 - c) * (1 - 2 * x), y + c * (1 - 2 * y), c)
        on_dst = (x + c * (1 - 2 * x), y + (1 - c) * (1 - 2 * y), c)
        locs = [pltpu.make_async_copy(ins[a], block_of(a, 4 * x + 2 * y + c), loc_sems.at[a]) for a in range(n)]
        first, landed, onward, to_sib, diag, diag_sib, from_sib = [], [], [], [], [], [], []
        for a in range(n):
            first += [copy(a, 0, me, sib, src=ins[a]), copy(a, 1, me, xn, src=ins[a]), copy(a, 2, me, yn, src=ins[a])]
            landed += [copy(a, 1, xn, me), copy(a, 2, yn, me)]
            onward.append(copy(a, 3, on_src, on_dst))
            to_sib += [copy(a, 4, xn, sib), copy(a, 5, yn, sib)]
            diag.append(copy(a, 3, dg, me))
            diag_sib.append(copy(a, 6, dg, sib))
            from_sib += [copy(a, 0, sib, me)] + [copy(a, 4 + j, (*blk[:2], 1 - c), me) for j, blk in enumerate((xn, yn, dg))]
        return dict(locs=locs, first=first, landed=landed, onward=onward, to_sib=to_sib, diag=diag,
                    diag_sib=diag_sib, from_sib=from_sib)

    def first(self, ins, outs, sems):
        p = self._parts(ins, outs, sems)
        for cp in p["locs"] + p["first"]:
            cp.start()

    def mid(self, ins, outs, sems):
        p = self._parts(ins, outs, sems)
        for cp in p["landed"]:
            cp.wait_recv()
        for cp in p["onward"] + p["to_sib"]:
            cp.start()

    def last(self, ins, outs, sems):
        p = self._parts(ins, outs, sems)
        for got, fwd in zip(p["diag"], p["diag_sib"]):
            got.wait_recv()
            fwd.start()
        for cp in p["from_sib"]:
            cp.wait_recv()
        for cp in p["first"] + p["onward"] + p["to_sib"] + p["diag_sib"]:
            cp.wait_send()
        for cp in p["locs"]:
            cp.wait()


class _SwapJob:
    US_PER_MB = 0.8

    def __init__(self, parts):
        n = len(parts)
        self.ins = list(parts)
        self.out_shape = [jax.ShapeDtypeStruct((4, 1) + p.shape[2:], p.dtype) for p in parts]
        self.sems = [pltpu.SemaphoreType.DMA((n,)), pltpu.SemaphoreType.DMA((n,))]
        self.cost = 5.0 + self.US_PER_MB * sum(p.size * p.dtype.itemsize for p in parts) / 1e6

    def _copies(self, ins, outs, sems):
        send_sems, recv_sems = sems
        x, y, c = _mesh_pos()
        return [pltpu.make_async_remote_copy(
            src_ref=ins[a].at[:, pl.ds(1 - c, 1)], dst_ref=outs[a],
            send_sem=send_sems.at[a], recv_sem=recv_sems.at[a],
            device_id=(x, y, 1 - c), device_id_type=MESH) for a in range(len(ins))]

    def first(self, ins, outs, sems):
        for cp in self._copies(ins, outs, sems):
            cp.start()

    def mid(self, ins, outs, sems):
        pass

    def last(self, ins, outs, sems):
        for cp in self._copies(ins, outs, sems):
            cp.wait()


class _ChipsJob:
    US_PER_MB = 11.0

    def __init__(self, qs):
        n = len(qs)
        self.ins = list(qs)
        self.out_shape = [jax.ShapeDtypeStruct(q.shape, q.dtype) for q in qs]
        self.sems = [pltpu.SemaphoreType.DMA((n, 3)), pltpu.SemaphoreType.DMA((n, 3)), pltpu.SemaphoreType.DMA((n,))]
        self.cost = 5.0 + self.US_PER_MB * sum(q.size * q.dtype.itemsize for q in qs) / 1e6

    def _copies(self, ins, outs, sems):
        send_sems, recv_sems, loc_sems = sems
        x, y, c = _mesh_pos()
        cps = []
        for a in range(len(ins)):
            cps.append(pltpu.make_async_copy(ins[a].at[2 * x + y], outs[a].at[3], loc_sems.at[a]))
            for k, chip in enumerate(_other_chips(x, y)):
                cps.append(pltpu.make_async_remote_copy(
                    src_ref=ins[a].at[2 * chip[0] + chip[1]], dst_ref=outs[a].at[k],
                    send_sem=send_sems.at[a, k], recv_sem=recv_sems.at[a, k],
                    device_id=(*chip, c), device_id_type=MESH))
        return cps

    def first(self, ins, outs, sems):
        for cp in self._copies(ins, outs, sems):
            cp.start()

    def mid(self, ins, outs, sems):
        pass

    def last(self, ins, outs, sems):
        for cp in self._copies(ins, outs, sems):
            cp.wait()


def _call(body, *, name, out_shape, in_specs, out_specs, args, grid=(), scratch_shapes=(), semantics=(), job=None):
    if job is None:
        outs = pl.pallas_call(
            body, name=name, out_shape=out_shape, grid=grid, in_specs=in_specs, out_specs=out_specs,
            scratch_shapes=list(scratch_shapes), compiler_params=_cp(*semantics))(*args)
        return outs, None
    single = not isinstance(out_shape, (list, tuple))
    host_out = [out_shape] if single else list(out_shape)
    host_ospecs = [out_specs] if single else list(out_specs)
    n_in, n_out, n_scr = len(in_specs), len(host_out), len(scratch_shapes)
    j_in, j_out = len(job.ins), len(job.out_shape)
    total = int(np.prod(grid)) if grid else 1
    mid_step = max(total // 2, 1)

    def full(*refs):
        h_in, jin = refs[:n_in], refs[n_in:n_in + j_in]
        o = n_in + j_in
        h_out, jout = refs[o:o + n_out], refs[o + n_out:o + n_out + j_out]
        o += n_out + j_out
        h_scr, jsem = refs[o:o + n_scr], refs[o + n_scr:]
        if total == 1:
            job.first(jin, jout, jsem)
            body(*h_in, *h_out, *h_scr)
            job.mid(jin, jout, jsem)
            job.last(jin, jout, jsem)
            return
        step = pl.program_id(0)
        for d in range(1, len(grid)):
            step = step * grid[d] + pl.program_id(d)
        pl.when(step == 0)(lambda: job.first(jin, jout, jsem))
        body(*h_in, *h_out, *h_scr)
        pl.when(step == mid_step)(lambda: job.mid(jin, jout, jsem))
        pl.when(step == total - 1)(lambda: job.last(jin, jout, jsem))

    hbm = pl.BlockSpec(memory_space=pltpu.HBM)
    outs = pl.pallas_call(
        full, name=name, out_shape=host_out + list(job.out_shape), grid=grid,
        in_specs=list(in_specs) + [hbm] * j_in, out_specs=host_ospecs + [hbm] * j_out,
        scratch_shapes=list(scratch_shapes) + list(job.sems),
        compiler_params=_cp(*(("arbitrary",) * len(grid))))(*args, *job.ins)
    host = outs[0] if single else list(outs[:n_out])
    return host, list(outs[n_out:])


def _run_job(job, name):
    def body():
        pass

    return _call(body, name=name, out_shape=[], in_specs=[], out_specs=[], args=[], job=job)[1]


class _Hosts:
    def __init__(self):
        self.pending = []
        self.flushed = 0

    def add(self, job, done):
        self.pending.append((job, done))

    def take(self, duration, force=False):
        taken, room = [], HOST_SLACK * duration
        for e in list(self.pending):
            if e[0].cost <= room:
                taken.append(e)
                room -= e[0].cost
        if not taken and force and self.pending:
            taken = [min(self.pending, key=lambda e: e[0].cost)]
        for e in taken:
            self.pending.remove(e)
        return taken

    def flush(self, name):
        while self.pending:
            job, done = self.pending.pop(0)
            done(_run_job(job, f"{name}{self.flushed}"))
            self.flushed += 1


def _pair_add(part4, recv, core, name):
    _, _, R, C = part4.shape
    tr = _tile(R, max(SUBLANE, (1 << 19) // C), 16)

    def body(core_ref, p_ref, r_ref, o_ref):
        o_ref[...] = (p_ref[...].astype(F32) + r_ref[...].astype(F32)).astype(o_ref.dtype)

    return pl.pallas_call(
        body, name=name,
        out_shape=jax.ShapeDtypeStruct((4, R, C), WIRE),
        grid_spec=pltpu.PrefetchScalarGridSpec(
            num_scalar_prefetch=1, grid=(4, R // tr),
            in_specs=[pl.BlockSpec((None, None, tr, C), lambda s, i, cr: (s, cr[0], i, 0)),
                      pl.BlockSpec((None, None, tr, C), lambda s, i, cr: (s, 0, i, 0))],
            out_specs=pl.BlockSpec((None, tr, C), lambda s, i, cr: (s, i, 0))),
        compiler_params=_cp("parallel", "parallel"),
    )(core, part4, recv)


def _queue_reduce_scatter(hosts, parts, core, name, sink):
    keys = list(parts)
    p4 = [parts[k].reshape((4, 2) + parts[k].shape[1:]) for k in keys]

    def swapped(recv):
        qs = [_pair_add(p, r, core, f"{name}_add{i}") for i, (p, r) in enumerate(zip(p4, recv))]
        hosts.add(_ChipsJob(qs), lambda outs: sink.update(zip(keys, outs)))

    hosts.add(_SwapJob(p4), swapped)


class _JobGroup:
    def __init__(self, jobs):
        self.jobs = jobs
        self.ins = [a for j in jobs for a in j.ins]
        self.out_shape = [s for j in jobs for s in j.out_shape]
        self.sems = [s for j in jobs for s in j.sems]

    def _each(self, phase, ins, outs, sems):
        i = o = s = 0
        for j in self.jobs:
            ni, no, ns = len(j.ins), len(j.out_shape), len(j.sems)
            getattr(j, phase)(ins[i:i + ni], outs[o:o + no], sems[s:s + ns])
            i, o, s = i + ni, o + no, s + ns

    def first(self, ins, outs, sems):
        self._each("first", ins, outs, sems)

    def mid(self, ins, outs, sems):
        self._each("mid", ins, outs, sems)

    def last(self, ins, outs, sems):
        self._each("last", ins, outs, sems)


def _hosted(hosts, duration, body, force=False, **kw):
    entries = hosts.take(duration, force) if hosts is not None else []
    out, jouts = _call(body, job=_JobGroup([e[0] for e in entries]) if entries else None, **kw)
    o = 0
    for job, done in entries:
        done(jouts[o:o + len(job.out_shape)])
        o += len(job.out_shape)
    return out


def _mm(a, b, *, ta=False, tb=False, out_dtype, name, blocks=None, hosts=None):
    a_parts = a.shape[0] if a.ndim == 3 else 0
    b_parts = b.shape[0] if b.ndim == 3 else 0
    assert not (a_parts and ta) and not (b_parts and tb)
    if a_parts:
        M, K = a.shape[1], a_parts * a.shape[2]
    else:
        M, K = (a.shape[1], a.shape[0]) if ta else a.shape
    N = b_parts * b.shape[2] if b_parts else (b.shape[0] if tb else b.shape[1])
    assert K == (b.shape[1] if (tb or b_parts) else b.shape[0]), (a.shape, b.shape, ta, tb)
    ns = N // blocks if blocks else N
    tm = _tile(M, MM_TILE, LANE)
    tk = _tile(a.shape[2] if a_parts else K, MM_TK, LANE)
    tn = _tile(math.gcd(ns, b.shape[2]) if b_parts else ns, MM_TILE, LANE)
    nk = K // tk
    per = ns // tn

    dims = (((0,) if ta else (1,), (1,) if tb else (0,)), ((), ()))

    def body(a_ref, b_ref, o_ref, *acc):
        part = lax.dot_general(a_ref[...], b_ref[...], dims, preferred_element_type=F32)
        if nk == 1:
            o_ref[...] = part.astype(o_ref.dtype)
            return
        acc_ref, k = acc[0], pl.program_id(2)

        @pl.when(k == 0)
        def _():
            acc_ref[...] = part

        @pl.when(k > 0)
        def _():
            acc_ref[...] += part

        @pl.when(k == nk - 1)
        def _():
            o_ref[...] = acc_ref[...].astype(o_ref.dtype)

    if a_parts:
        a_per = a.shape[2] // tk
        a_spec = pl.BlockSpec((None, tm, tk), lambda i, j, k: (k // a_per, i, k % a_per))
    else:
        a_spec = pl.BlockSpec((tk, tm), lambda i, j, k: (k, i)) if ta else pl.BlockSpec((tm, tk), lambda i, j, k: (i, k))
    if b_parts:
        b_per = b.shape[2] // tn
        b_spec = pl.BlockSpec((None, tk, tn), lambda i, j, k: (j // b_per, k, j % b_per))
    else:
        b_spec = pl.BlockSpec((tn, tk), lambda i, j, k: (j, k)) if tb else pl.BlockSpec((tk, tn), lambda i, j, k: (k, j))
    if blocks:
        out_shape = jax.ShapeDtypeStruct((blocks, M, ns), out_dtype)
        o_spec = pl.BlockSpec((None, tm, tn), lambda i, j, k: (j // per, i, j % per))
    else:
        out_shape = jax.ShapeDtypeStruct((M, N), out_dtype)
        o_spec = pl.BlockSpec((tm, tn), lambda i, j, k: (i, j))
    return _hosted(
        hosts, 2.0 * M * N * K / MXU_FLOP_PER_US, body, name=name, out_shape=out_shape, grid=(M // tm, N // tn, nk),
        in_specs=[a_spec, b_spec], out_specs=o_spec, args=[a, b],
        scratch_shapes=[pltpu.VMEM((tm, tn), F32)] if nk > 1 else [], semantics=("parallel", "parallel", "arbitrary"))


def _rms_fwd(xin, g, name, hosts=None):
    T, D = xin.shape
    tr = _tile(T, ROW_TILE, SUBLANE)

    def body(x_ref, g_ref, o_ref):
        xv = x_ref[...]
        r = lax.rsqrt(jnp.mean(xv * xv, axis=-1, keepdims=True) + NORM_EPS)
        o_ref[...] = (xv * r * g_ref[...]).astype(o_ref.dtype)

    row = pl.BlockSpec((tr, D), lambda i: (i, 0))
    return _hosted(
        hosts, T * D * 6.0 / HBM_BYTES_PER_US, body, force=True, name=name,
        out_shape=jax.ShapeDtypeStruct((T, D), MXU), grid=(T // tr,),
        in_specs=[row, pl.BlockSpec((1, D), lambda i: (0, 0))], out_specs=row, args=[xin, g], semantics=("parallel",))


def _rms_bwd(xin, dout, g, name, resid=None, out_dtype=F32):
    T, D = xin.shape
    tr = _tile(T, ROW_TILE, SUBLANE)

    def body(*refs):
        x_ref, d_ref, g_ref = refs[:3]
        dx_ref, dg_ref = refs[-2:]
        xv = x_ref[...].astype(F32)
        dv = d_ref[...].astype(F32)
        r = lax.rsqrt(jnp.mean(xv * xv, axis=-1, keepdims=True) + NORM_EPS)
        xh = xv * r
        dh = dv * g_ref[...]
        dxv = r * (dh - xh * jnp.mean(dh * xh, axis=-1, keepdims=True))
        if resid is not None:
            dxv = refs[3][...] + dxv
        dx_ref[...] = dxv.astype(dx_ref.dtype)

        @pl.when(pl.program_id(0) == 0)
        def _():
            dg_ref[...] = jnp.zeros_like(dg_ref)

        dg_ref[...] += jnp.sum(dv * xh, axis=0, keepdims=True)

    row = pl.BlockSpec((tr, D), lambda i: (i, 0))
    vec = pl.BlockSpec((1, D), lambda i: (0, 0))
    ins = [xin, dout, g] + ([resid] if resid is not None else [])
    return pl.pallas_call(
        body, name=name,
        out_shape=[jax.ShapeDtypeStruct((T, D), out_dtype), jax.ShapeDtypeStruct((1, D), F32)],
        grid=(T // tr,), in_specs=[row, row, vec] + ([row] if resid is not None else []),
        out_specs=[row, vec], compiler_params=_cp("arbitrary"),
    )(*ins)


def _post_fwd(y, g, resid, name, next_gain=None, target=None):
    T, D = y.shape
    tr = _tile(T, ROW_TILE, SUBLANE)

    def body(y_ref, g_ref, r_ref, e_ref, o1_ref, o2_ref):
        yv = y_ref[...]
        r = lax.rsqrt(jnp.mean(yv * yv, axis=-1, keepdims=True) + NORM_EPS)
        xv = r_ref[...] + yv * r * g_ref[...]
        if target is None:
            o1_ref[...] = xv
            r2 = lax.rsqrt(jnp.mean(xv * xv, axis=-1, keepdims=True) + NORM_EPS)
            o2_ref[...] = (xv * r2 * e_ref[...]).astype(o2_ref.dtype)
        else:
            e = xv - e_ref[...]
            o1_ref[...] = e * (1.0 / D)

            @pl.when(pl.program_id(0) == 0)
            def _():
                o2_ref[...] = jnp.zeros_like(o2_ref)

            rows = jnp.sum(e * e, axis=-1, keepdims=True) * (1.0 / D)
            o2_ref[...] += jnp.broadcast_to(0.5 * jnp.sum(rows, axis=0, keepdims=True), o2_ref.shape)

    row = pl.BlockSpec((tr, D), lambda i: (i, 0))
    vec = pl.BlockSpec((1, D), lambda i: (0, 0))
    if target is None:
        extra, e_spec = next_gain, vec
        out2, o2_spec = jax.ShapeDtypeStruct((T, D), MXU), row
    else:
        extra, e_spec = target, row
        out2, o2_spec = jax.ShapeDtypeStruct((1, LANE), F32), pl.BlockSpec((1, LANE), lambda i: (0, 0))
    return pl.pallas_call(
        body, name=name, out_shape=[jax.ShapeDtypeStruct((T, D), F32), out2],
        grid=(T // tr,), in_specs=[row, vec, row, e_spec], out_specs=[row, o2_spec],
        compiler_params=_cp("arbitrary" if target is not None else "parallel"),
    )(y, g, resid, extra)


def _mla_norms_fwd(P, gq, gkv, qr, kvr, name):
    T = P.shape[0]
    tr = _tile(T, ROW_TILE, SUBLANE)

    def body(cq_ref, ckv_ref, gq_ref, gkv_ref, oq_ref, okv_ref):
        for x_ref, g_ref, o_ref in ((cq_ref, gq_ref, oq_ref), (ckv_ref, gkv_ref, okv_ref)):
            xv = x_ref[...].astype(F32)
            r = lax.rsqrt(jnp.mean(xv * xv, axis=-1, keepdims=True) + NORM_EPS)
            o_ref[...] = (xv * r * g_ref[...]).astype(o_ref.dtype)

    return pl.pallas_call(
        body, name=name,
        out_shape=[jax.ShapeDtypeStruct((T, qr), MXU), jax.ShapeDtypeStruct((T, kvr), MXU)],
        grid=(T // tr,),
        in_specs=[pl.BlockSpec((tr, qr), lambda i: (i, 0)), pl.BlockSpec((tr, kvr), lambda i: (i, qr // kvr)),
                  pl.BlockSpec((1, qr), lambda i: (0, 0)), pl.BlockSpec((1, kvr), lambda i: (0, 0))],
        out_specs=[pl.BlockSpec((tr, qr), lambda i: (i, 0)), pl.BlockSpec((tr, kvr), lambda i: (i, 0))],
        compiler_params=_cp("parallel"),
    )(P, P, gq, gkv)


def _mla_norms_bwd(P, dcqn, dckvn_k, dckvn_v, dkr, gq, gkv, qr, kvr, pa, name):
    T = P.shape[0]
    tr = _tile(T, ROW_TILE, SUBLANE)
    c2 = qr + kvr

    def body(cq_ref, ckv_ref, dq_ref, dk_ref, dv_ref, dkr_ref, gq_ref, gkv_ref, dp_ref, dgq_ref, dgkv_ref):
        @pl.when(pl.program_id(0) == 0)
        def _():
            dgq_ref[...] = jnp.zeros_like(dgq_ref)
            dgkv_ref[...] = jnp.zeros_like(dgkv_ref)

        def one(x_ref, dv, g_ref, dg_ref):
            xv = x_ref[...].astype(F32)
            r = lax.rsqrt(jnp.mean(xv * xv, axis=-1, keepdims=True) + NORM_EPS)
            xh = xv * r
            dh = dv * g_ref[...]
            dg_ref[...] += jnp.sum(dv * xh, axis=0, keepdims=True)
            return r * (dh - xh * jnp.mean(dh * xh, axis=-1, keepdims=True))

        dp_ref[:, 0:qr] = one(cq_ref, dq_ref[...], gq_ref, dgq_ref).astype(dp_ref.dtype)
        dp_ref[:, qr:c2] = one(ckv_ref, dk_ref[...] + dv_ref[...], gkv_ref, dgkv_ref).astype(dp_ref.dtype)
        dp_ref[:, c2:c2 + LANE] = dkr_ref[...].astype(dp_ref.dtype)
        if pa > c2 + LANE:
            dp_ref[:, c2 + LANE:pa] = jnp.zeros((tr, pa - c2 - LANE), dp_ref.dtype)

    return pl.pallas_call(
        body, name=name,
        out_shape=[jax.ShapeDtypeStruct((T, pa), MXU), jax.ShapeDtypeStruct((1, qr), F32),
                   jax.ShapeDtypeStruct((1, kvr), F32)],
        grid=(T // tr,),
        in_specs=[pl.BlockSpec((tr, qr), lambda i: (i, 0)), pl.BlockSpec((tr, kvr), lambda i: (i, qr // kvr)),
                  pl.BlockSpec((tr, qr), lambda i: (i, 0)), pl.BlockSpec((tr, kvr), lambda i: (i, 0)),
                  pl.BlockSpec((tr, kvr), lambda i: (i, 0)), pl.BlockSpec((tr, LANE), lambda i: (i, 0)),
                  pl.BlockSpec((1, qr), lambda i: (0, 0)), pl.BlockSpec((1, kvr), lambda i: (0, 0))],
        out_specs=[pl.BlockSpec((tr, pa), lambda i: (i, 0)), pl.BlockSpec((1, qr), lambda i: (0, 0)),
                   pl.BlockSpec((1, kvr), lambda i: (0, 0))],
        compiler_params=_cp("arbitrary"),
    )(P, P, dcqn, dckvn_k, dckvn_v, dkr, gq, gkv)


def _rope_tables(pos_ref, invf_ref):
    ang = pos_ref[...].astype(F32) * invf_ref[...]
    lane = lax.broadcasted_iota(jnp.int32, ang.shape, 1)
    cos, sin = jnp.cos(ang), jnp.sin(ang)
    half = ROPE // 2
    c = jnp.where(lane < ROPE, cos, 0.0)
    s1 = jnp.where(lane < half, -sin, 0.0)
    s2 = jnp.where((lane >= half) & (lane < ROPE), sin, 0.0)
    return c, s1, s2


def _rope_fwd(qpad, kpad, v, P, pos, invf, kr_blk, H, name):
    T = qpad.shape[0]
    tr = _tile(T, ROW_TILE, SUBLANE)

    def body(q_ref, k_ref, v_ref, kr_ref, pos_ref, invf_ref, qo_ref, ko_ref, vo_ref):
        c, s1, s2 = _rope_tables(pos_ref, invf_ref)

        def rot(t):
            return t * c + pltpu.roll(t, LANE - ROPE // 2, 1) * s1 + pltpu.roll(t, ROPE // 2, 1) * s2

        kr = rot(kr_ref[...].astype(F32)).astype(ko_ref.dtype)
        ones = jnp.ones((tr, QPAD - VHEAD), vo_ref.dtype)
        for h in range(H):
            lo = h * QPAD
            qo_ref[:, lo:lo + NOPE] = q_ref[:, lo:lo + NOPE]
            qo_ref[:, lo + NOPE:lo + QPAD] = rot(q_ref[:, lo + NOPE:lo + QPAD].astype(F32)).astype(qo_ref.dtype)
            ko_ref[:, lo:lo + NOPE] = k_ref[:, lo:lo + NOPE]
            ko_ref[:, lo + NOPE:lo + QPAD] = kr
            vo_ref[:, lo:lo + VHEAD] = v_ref[:, h * VHEAD:(h + 1) * VHEAD]
            vo_ref[:, lo + VHEAD:lo + QPAD] = ones

    wide = pl.BlockSpec((tr, H * QPAD), lambda i: (i, 0))
    return pl.pallas_call(
        body, name=name,
        out_shape=[jax.ShapeDtypeStruct(qpad.shape, MXU)] * 3,
        grid=(T // tr,),
        in_specs=[wide, wide, pl.BlockSpec((tr, H * VHEAD), lambda i: (i, 0)),
                  pl.BlockSpec((tr, LANE), lambda i: (i, kr_blk)),
                  pl.BlockSpec((tr, 1), lambda i: (i, 0)), pl.BlockSpec((1, LANE), lambda i: (0, 0))],
        out_specs=[wide, wide, wide], compiler_params=_cp("parallel"),
    )(qpad, kpad, v, P, pos, invf)


def _rope_bwd(dqcat, dkcat, pos, invf, H, name):
    T = dqcat.shape[0]
    tr = _tile(T, ROW_TILE, SUBLANE)

    def body(dq_ref, dk_ref, pos_ref, invf_ref, dqo_ref, dkr_ref):
        c, s1, s2 = _rope_tables(pos_ref, invf_ref)

        def rot_t(v):
            return v * c + pltpu.roll(v * s1, ROPE // 2, 1) + pltpu.roll(v * s2, LANE - ROPE // 2, 1)

        acc = jnp.zeros((tr, LANE), F32)
        for h in range(H):
            lo = h * QPAD
            dqo_ref[:, lo:lo + NOPE] = dq_ref[:, lo:lo + NOPE]
            dqo_ref[:, lo + NOPE:lo + QPAD] = rot_t(dq_ref[:, lo + NOPE:lo + QPAD].astype(F32)).astype(dqo_ref.dtype)
            acc = acc + dk_ref[:, lo + NOPE:lo + QPAD].astype(F32)
        dkr_ref[...] = rot_t(acc)

    wide = pl.BlockSpec((tr, H * QPAD), lambda i: (i, 0))
    return pl.pallas_call(
        body, name=name,
        out_shape=[jax.ShapeDtypeStruct(dqcat.shape, MXU), jax.ShapeDtypeStruct((T, LANE), F32)],
        grid=(T // tr,),
        in_specs=[wide, wide, pl.BlockSpec((tr, 1), lambda i: (i, 0)), pl.BlockSpec((1, LANE), lambda i: (0, 0))],
        out_specs=[wide, pl.BlockSpec((tr, LANE), lambda i: (i, 0))], compiler_params=_cp("parallel"),
    )(dqcat, dkcat, pos, invf)


def _causal_mask(s):
    row = lax.broadcasted_iota(jnp.int32, s.shape, 0)
    col = lax.broadcasted_iota(jnp.int32, s.shape, 1)
    return jnp.where(col <= row, s, -1e30)


def _flash_fwd(qcat, kcat, v, P, z_blk, H, name, hosts=None):
    T = qcat.shape[0]
    tq = _tile(T, ATT_TILE, LANE)
    hp = 2 if H % 2 == 0 and z_blk % 2 == 0 else 1
    c2 = ATT_SCALE * LOG2E

    def body(q_ref, k_ref, v_ref, z_ref, o_ref, a_ref, lse_ref, m_s, acc_s):
        i = pl.program_id(1)
        m_s[...] = jnp.full_like(m_s, -1e30)
        acc_s[...] = jnp.zeros_like(acc_s)

        def step(j, masked):
            r0 = pl.multiple_of(j * tq, tq)
            for h in range(hp):
                q = q_ref[:, h * QPAD:(h + 1) * QPAD]
                kb = k_ref[pl.ds(r0, tq), h * QPAD:(h + 1) * QPAD]
                vb = v_ref[pl.ds(r0, tq), h * QPAD:(h + 1) * QPAD]
                s = lax.dot_general(q, kb, NT, preferred_element_type=F32)
                if masked:
                    s = _causal_mask(s)
                m_prev = m_s[h]
                m_new = jnp.maximum(m_prev, jnp.max(s, axis=1, keepdims=True))
                p = jnp.exp2((s - m_new) * c2).astype(MXU)
                alpha = jnp.exp2((m_prev - m_new) * c2)
                acc_s[h] = alpha * acc_s[h] + lax.dot_general(p, vb, NN, preferred_element_type=F32)
                m_s[h] = m_new

        def loop(j, carry):
            step(j, False)
            return carry

        lax.fori_loop(0, i, loop, 0)
        step(i, True)
        for h in range(hp):
            l = acc_s[h, :, VHEAD:QPAD]
            o = acc_s[h, :, 0:VHEAD] / l
            cols = slice(h * VHEAD, (h + 1) * VHEAD)
            o_ref[:, cols] = o.astype(o_ref.dtype)
            a_ref[:, cols] = (o * _silu(z_ref[:, cols].astype(F32))).astype(a_ref.dtype)
            lse_ref[h] = m_s[h] * ATT_SCALE + jnp.log(l)

    return _hosted(
        hosts, ATT_FWD_US * (T / 4096.0) ** 2 * (H / 32.0), body, name=name,
        out_shape=[jax.ShapeDtypeStruct((T, H * VHEAD), MXU), jax.ShapeDtypeStruct((T, H * VHEAD), MXU),
                   jax.ShapeDtypeStruct((H, T, LANE), F32)],
        grid=(H // hp, T // tq),
        in_specs=[pl.BlockSpec((tq, hp * QPAD), lambda h, i: (i, h)), pl.BlockSpec((T, hp * QPAD), lambda h, i: (0, h)),
                  pl.BlockSpec((T, hp * QPAD), lambda h, i: (0, h)),
                  pl.BlockSpec((tq, hp * VHEAD), lambda h, i: (i, z_blk // hp + h))],
        out_specs=[pl.BlockSpec((tq, hp * VHEAD), lambda h, i: (i, h)), pl.BlockSpec((tq, hp * VHEAD), lambda h, i: (i, h)),
                   pl.BlockSpec((hp, tq, LANE), lambda h, i: (h, i, 0))],
        scratch_shapes=[pltpu.VMEM((hp, tq, 1), F32), pltpu.VMEM((hp, tq, QPAD), F32)],
        args=[qcat, kcat, v, P], semantics=("parallel", "arbitrary"))


def _flash_bwd(qcat, kcat, v, do, lse, delta, H, name, hosts=None):
    T = qcat.shape[0]
    tq = _tile(T, ATT_TILE, LANE)
    nq = T // tq
    c2 = ATT_SCALE * LOG2E

    def body(q_ref, do_ref, lse_ref, dl_ref, k_ref, v_ref, dq_ref, dk_ref, dv_ref, dq_s, dk_s, dv_s):
        j = pl.program_id(1)
        kb, vb = k_ref[...], v_ref[...]
        dk_s[...] = jnp.zeros_like(dk_s)
        dv_s[...] = jnp.zeros_like(dv_s)

        @pl.when(j == 0)
        def _():
            dq_s[...] = jnp.zeros_like(dq_s)

        def step(i, masked):
            rows = pl.ds(pl.multiple_of(i * tq, tq), tq)
            qb = q_ref[rows, :]
            dob = do_ref[rows, :]
            s = lax.dot_general(qb, kb, NT, preferred_element_type=F32)
            if masked:
                s = _causal_mask(s)
            p = jnp.exp2(s * c2 - lse_ref[rows, 0:1] * LOG2E)
            dv_s[...] += lax.dot_general(p.astype(dob.dtype), dob, TN, preferred_element_type=F32)
            dp = lax.dot_general(dob, vb, NT, preferred_element_type=F32)
            ds = (p * (dp - dl_ref[rows, 0:1]) * ATT_SCALE).astype(qb.dtype)
            dk_s[...] += lax.dot_general(ds, qb, TN, preferred_element_type=F32)
            dq_s[rows, :] += lax.dot_general(ds, kb, NN, preferred_element_type=F32)

        def loop(i, carry):
            step(i, False)
            return carry

        step(j, True)
        lax.fori_loop(j + 1, nq, loop, 0)
        dk_ref[...] = dk_s[...].astype(dk_ref.dtype)
        dv_ref[...] = dv_s[...].astype(dv_ref.dtype)

        @pl.when(j == nq - 1)
        def _():
            dq_ref[...] = dq_s[...].astype(dq_ref.dtype)

    return _hosted(
        hosts, ATT_BWD_US * (T / 4096.0) ** 2 * (H / 32.0), body, name=name,
        out_shape=[jax.ShapeDtypeStruct(qcat.shape, MXU), jax.ShapeDtypeStruct(kcat.shape, MXU),
                   jax.ShapeDtypeStruct(v.shape, MXU)],
        grid=(H, nq),
        in_specs=[pl.BlockSpec((T, QPAD), lambda h, j: (0, h)), pl.BlockSpec((T, VHEAD), lambda h, j: (0, h)),
                  pl.BlockSpec((None, T, LANE), lambda h, j: (h, 0, 0)), pl.BlockSpec((None, T, LANE), lambda h, j: (h, 0, 0)),
                  pl.BlockSpec((tq, QPAD), lambda h, j: (j, h)), pl.BlockSpec((tq, VHEAD), lambda h, j: (j, h))],
        out_specs=[pl.BlockSpec((T, QPAD), lambda h, j: (0, h)), pl.BlockSpec((tq, QPAD), lambda h, j: (j, h)),
                   pl.BlockSpec((tq, VHEAD), lambda h, j: (j, h))],
        scratch_shapes=[pltpu.VMEM((T, QPAD), F32), pltpu.VMEM((tq, QPAD), F32), pltpu.VMEM((tq, VHEAD), F32)],
        args=[qcat, do, lse, delta, kcat, v], semantics=("parallel", "arbitrary"))


def _mla_gate_bwd(da, o, P, pa, H, name):
    T, HV = da.shape
    tr = _tile(T, ROW_TILE, SUBLANE)
    cw = _tile(math.gcd(pa, HV), 512, LANE)
    hb = cw // VHEAD

    def body(da_ref, o_ref, z_ref, do_ref, dz_ref, dl_ref):
        dav, ov, zv = da_ref[...].astype(F32), o_ref[...].astype(F32), z_ref[...].astype(F32)
        dov = dav * _silu(zv)
        do_ref[...] = dov.astype(do_ref.dtype)
        dz_ref[...] = (dav * ov * _silu_grad(zv)).astype(dz_ref.dtype)
        prod = dov * ov
        for h in range(hb):
            dl_ref[h] = jnp.broadcast_to(jnp.sum(prod[:, h * VHEAD:(h + 1) * VHEAD], axis=1, keepdims=True), (tr, LANE))

    blk = pl.BlockSpec((tr, cw), lambda i, j: (i, j))
    shifted = pl.BlockSpec((tr, cw), lambda i, j: (i, pa // cw + j))
    return pl.pallas_call(
        body, name=name,
        out_shape=[jax.ShapeDtypeStruct((T, HV), MXU), jax.ShapeDtypeStruct((T, HV), MXU),
                   jax.ShapeDtypeStruct((H, T, LANE), F32)],
        grid=(T // tr, HV // cw),
        in_specs=[blk, blk, shifted],
        out_specs=[blk, blk, pl.BlockSpec((hb, tr, LANE), lambda i, j: (j, i, 0))],
        compiler_params=_cp("parallel", "parallel"),
    )(da, o, P)


def _bank_rows(width):
    return min(SUBLANE, width), -(-(width - 1) // SUBLANE) * SUBLANE


def _bank_fill(bank_ref, val, width, T, causal):
    nr, hp = _bank_rows(width)
    rows = lax.broadcasted_iota(jnp.int32, val.shape, 0)
    zero = jnp.zeros((hp,) + val.shape[1:], F32)
    for r in range(nr):
        if causal:
            bank_ref[r, 0:hp, :] = zero
            bank_ref[r, hp:hp + T, :] = val if r == 0 else jnp.where(rows >= r, pltpu.roll(val, r, 0), 0.0)
        else:
            bank_ref[r, T:T + hp, :] = zero
            bank_ref[r, 0:T, :] = val if r == 0 else jnp.where(rows < T - r, pltpu.roll(val, T - r, 0), 0.0)


def _bank_tap(bank_ref, s, t0, tc, width, causal):
    _, hp = _bank_rows(width)
    q, r = divmod(s, SUBLANE)
    off = hp - SUBLANE * q if causal else SUBLANE * q
    return bank_ref[r, pl.ds(pl.multiple_of(t0 + off, SUBLANE), tc), :]


def _conv_chunk(bank_ref, w_ref, t0, tc, width):
    acc = None
    for s in range(width):
        k = width - 1 - s
        term = w_ref[k:k + 1, :] * _bank_tap(bank_ref, s, t0, tc, width, True)
        acc = term if acc is None else acc + term
    return acc


def _conv_bwd_chunk(bank_ref, w_ref, xin, dw_ref, t0, tc, width):
    acc = None
    for s in range(width):
        k = width - 1 - s
        tap = _bank_tap(bank_ref, s, t0, tc, width, False)
        term = w_ref[k:k + 1, :] * tap
        acc = term if acc is None else acc + term
        dw_ref[k] += jnp.sum((tap * xin).reshape(tc // SUBLANE, SUBLANE, xin.shape[1]), axis=0)
    return acc


def _full_t(T, cb, col):
    return pl.BlockSpec((T, cb), lambda j, col=col: (0, col + j))


def _sc_fwd(P, w, D, name):
    T = P.shape[0]
    cb = _tile(D, CONV_CB, LANE)
    nb = D // cb
    width = w.shape[0]
    tc = _tile(T, CONV_CHUNK, SUBLANE)
    nr, hp = _bank_rows(width)

    def body(b_ref, c_ref, u_ref, z_ref, w_ref, a_ref, bank):
        _bank_fill(bank, c_ref[...].astype(F32) * u_ref[...].astype(F32), width, T, True)

        def chunk(ci, carry):
            t0 = pl.multiple_of(ci * tc, tc)
            rows = pl.ds(t0, tc)
            v = _conv_chunk(bank, w_ref, t0, tc, width)
            a_ref[rows, :] = (b_ref[rows, :].astype(F32) * v * _silu(z_ref[rows, :].astype(F32))).astype(a_ref.dtype)
            return carry

        lax.fori_loop(0, T // tc, chunk, 0)

    return pl.pallas_call(
        body, name=name, out_shape=jax.ShapeDtypeStruct((T, D), MXU), grid=(nb,),
        in_specs=[_full_t(T, cb, 0), _full_t(T, cb, nb), _full_t(T, cb, 2 * nb), _full_t(T, cb, 3 * nb),
                  pl.BlockSpec((width, cb), lambda j: (0, j))],
        out_specs=_full_t(T, cb, 0),
        scratch_shapes=[pltpu.VMEM((nr, T + hp, cb), F32)],
        compiler_params=_cp("parallel"),
    )(P, P, P, P, w)


def _sc_bwd(P, da, w, D, name):
    T = P.shape[0]
    cb = _tile(D, CONV_CB, LANE)
    nb = D // cb
    width = w.shape[0]
    tc = _tile(T, CONV_CHUNK, SUBLANE)
    nr, hp = _bank_rows(width)

    def body(b_ref, c_ref, u_ref, z_ref, da_ref, w_ref, dp_ref, dw_ref, bank, dv_s, dw_s):
        _bank_fill(bank, c_ref[...].astype(F32) * u_ref[...].astype(F32), width, T, True)

        def first(ci, carry):
            t0 = pl.multiple_of(ci * tc, tc)
            rows = pl.ds(t0, tc)
            v = _conv_chunk(bank, w_ref, t0, tc, width)
            bv, zv, dav = b_ref[rows, :].astype(F32), z_ref[rows, :].astype(F32), da_ref[rows, :].astype(F32)
            dyb = dav * _silu(zv)
            dp_ref[3, rows, :] = (dav * bv * v * _silu_grad(zv)).astype(dp_ref.dtype)
            dp_ref[0, rows, :] = (dyb * v).astype(dp_ref.dtype)
            dv_s[rows, :] = dyb * bv
            return carry

        lax.fori_loop(0, T // tc, first, 0)
        _bank_fill(bank, dv_s[...], width, T, False)
        dw_s[...] = jnp.zeros_like(dw_s)

        def second(ci, carry):
            t0 = pl.multiple_of(ci * tc, tc)
            rows = pl.ds(t0, tc)
            cv, uv = c_ref[rows, :].astype(F32), u_ref[rows, :].astype(F32)
            dcu = _conv_bwd_chunk(bank, w_ref, cv * uv, dw_s, t0, tc, width)
            dp_ref[1, rows, :] = (dcu * uv).astype(dp_ref.dtype)
            dp_ref[2, rows, :] = (dcu * cv).astype(dp_ref.dtype)
            return carry

        lax.fori_loop(0, T // tc, second, 0)
        dw_ref[...] = jnp.sum(dw_s[...], axis=1)

    return pl.pallas_call(
        body, name=name,
        out_shape=[jax.ShapeDtypeStruct((4, T, D), MXU), jax.ShapeDtypeStruct((width, D), F32)],
        grid=(nb,),
        in_specs=[_full_t(T, cb, 0), _full_t(T, cb, nb), _full_t(T, cb, 2 * nb), _full_t(T, cb, 3 * nb),
                  _full_t(T, cb, 0), pl.BlockSpec((width, cb), lambda j: (0, j))],
        out_specs=[pl.BlockSpec((4, T, cb), lambda j: (0, 0, j)), pl.BlockSpec((width, cb), lambda j: (0, j))],
        scratch_shapes=[pltpu.VMEM((nr, T + hp, cb), F32), pltpu.VMEM((T, cb), F32),
                        pltpu.VMEM((width, SUBLANE, cb), F32)],
        compiler_params=_cp("parallel"),
    )(P, P, P, P, da, w)


def _cf_conv_fwd(P, w, bias, D, name):
    T = P.shape[0]
    cb = _tile(D, CONV_CB, LANE)
    nb = D // cb
    width = w.shape[0]
    tc = _tile(T, CONV_CHUNK, SUBLANE)
    nr, hp = _bank_rows(width)

    def body(a_ref, g_ref, w_ref, b_ref, y_ref, bank):
        _bank_fill(bank, a_ref[...].astype(F32) * _sig(g_ref[...].astype(F32)), width, T, True)

        def chunk(ci, carry):
            t0 = pl.multiple_of(ci * tc, tc)
            y_ref[pl.ds(t0, tc), :] = (_conv_chunk(bank, w_ref, t0, tc, width) + b_ref[...]).astype(y_ref.dtype)
            return carry

        lax.fori_loop(0, T // tc, chunk, 0)

    return pl.pallas_call(
        body, name=name, out_shape=jax.ShapeDtypeStruct((T, D), F32), grid=(nb,),
        in_specs=[_full_t(T, cb, 0), _full_t(T, cb, nb), pl.BlockSpec((width, cb), lambda j: (0, j)),
                  pl.BlockSpec((1, cb), lambda j: (0, j))],
        out_specs=_full_t(T, cb, 0),
        scratch_shapes=[pltpu.VMEM((nr, T + hp, cb), F32)],
        compiler_params=_cp("parallel"),
    )(P, P, w, bias)


def _cf_conv_bwd(P, dyc, w, D, name):
    T = P.shape[0]
    cb = _tile(D, CONV_CB, LANE)
    nb = D // cb
    width = w.shape[0]
    tc = _tile(T, CONV_CHUNK, SUBLANE)
    nr, hp = _bank_rows(width)

    def body(a_ref, g_ref, dy_ref, w_ref, da_ref, dg_ref, dw_ref, db_ref, bank, dw_s):
        dyv = dy_ref[...]
        db_ref[...] = jnp.sum(dyv, axis=0, keepdims=True)
        _bank_fill(bank, dyv, width, T, False)
        dw_s[...] = jnp.zeros_like(dw_s)

        def chunk(ci, carry):
            t0 = pl.multiple_of(ci * tc, tc)
            rows = pl.ds(t0, tc)
            av, sg = a_ref[rows, :].astype(F32), _sig(g_ref[rows, :].astype(F32))
            dyg = _conv_bwd_chunk(bank, w_ref, av * sg, dw_s, t0, tc, width)
            da_ref[rows, :] = (dyg * sg).astype(da_ref.dtype)
            dg_ref[rows, :] = (dyg * av * sg * (1.0 - sg)).astype(dg_ref.dtype)
            return carry

        lax.fori_loop(0, T // tc, chunk, 0)
        dw_ref[...] = jnp.sum(dw_s[...], axis=1)

    return pl.pallas_call(
        body, name=name,
        out_shape=[jax.ShapeDtypeStruct((T, D), MXU), jax.ShapeDtypeStruct((T, D), MXU),
                   jax.ShapeDtypeStruct((width, D), F32), jax.ShapeDtypeStruct((1, D), F32)],
        grid=(nb,),
        in_specs=[_full_t(T, cb, 0), _full_t(T, cb, nb), _full_t(T, cb, 0),
                  pl.BlockSpec((width, cb), lambda j: (0, j))],
        out_specs=[_full_t(T, cb, 0), _full_t(T, cb, 0), pl.BlockSpec((width, cb), lambda j: (0, j)),
                   pl.BlockSpec((1, cb), lambda j: (0, j))],
        scratch_shapes=[pltpu.VMEM((nr, T + hp, cb), F32), pltpu.VMEM((width, SUBLANE, cb), F32)],
        compiler_params=_cp("parallel"),
    )(P, P, dyc, w)


def _layer_norm_stats(v):
    mu = jnp.mean(v, axis=-1, keepdims=True)
    cen = v - mu
    rstd = lax.rsqrt(jnp.mean(cen * cen, axis=-1, keepdims=True) + LN_EPS)
    return cen * rstd, rstd


def _layer_norm_bwd(dxh, xh, rstd):
    return rstd * (dxh - jnp.mean(dxh, axis=-1, keepdims=True) - xh * jnp.mean(dxh * xh, axis=-1, keepdims=True))


def _cf_act_fwd(yc, P, lg, lb, D, name):
    T = yc.shape[0]
    tr = _tile(T, ROW_TILE, SUBLANE)

    def body(y_ref, z_ref, g_ref, b_ref, o_ref):
        xh, _ = _layer_norm_stats(y_ref[...])
        yl = xh * g_ref[...] + b_ref[...]
        o_ref[...] = (_silu(yl) * _silu(z_ref[...].astype(F32))).astype(o_ref.dtype)

    row = pl.BlockSpec((tr, D), lambda i: (i, 0))
    vec = pl.BlockSpec((1, D), lambda i: (0, 0))
    return pl.pallas_call(
        body, name=name, out_shape=jax.ShapeDtypeStruct((T, D), MXU), grid=(T // tr,),
        in_specs=[row, pl.BlockSpec((tr, D), lambda i: (i, 2)), vec, vec], out_specs=row,
        compiler_params=_cp("parallel"),
    )(yc, P, lg, lb)


def _cf_act_bwd(yc, P, da, lg, lb, D, name):
    T = yc.shape[0]
    tr = _tile(T, ROW_TILE, SUBLANE)

    def body(y_ref, z_ref, da_ref, g_ref, b_ref, dy_ref, dz_ref, dg_ref, db_ref):
        @pl.when(pl.program_id(0) == 0)
        def _():
            dg_ref[...] = jnp.zeros_like(dg_ref)
            db_ref[...] = jnp.zeros_like(db_ref)

        xh, rstd = _layer_norm_stats(y_ref[...])
        yl = xh * g_ref[...] + b_ref[...]
        zv, dav = z_ref[...].astype(F32), da_ref[...].astype(F32)
        dz_ref[...] = (dav * _silu(yl) * _silu_grad(zv)).astype(dz_ref.dtype)
        dyl = dav * _silu(zv) * _silu_grad(yl)
        dg_ref[...] += jnp.sum(dyl * xh, axis=0, keepdims=True)
        db_ref[...] += jnp.sum(dyl, axis=0, keepdims=True)
        dy_ref[...] = _layer_norm_bwd(dyl * g_ref[...], xh, rstd)

    row = pl.BlockSpec((tr, D), lambda i: (i, 0))
    zcol = pl.BlockSpec((tr, D), lambda i: (i, 2))
    vec = pl.BlockSpec((1, D), lambda i: (0, 0))
    return pl.pallas_call(
        body, name=name,
        out_shape=[jax.ShapeDtypeStruct((T, D), F32), jax.ShapeDtypeStruct((T, D), MXU),
                   jax.ShapeDtypeStruct((1, D), F32), jax.ShapeDtypeStruct((1, D), F32)],
        grid=(T // tr,), in_specs=[row, zcol, row, vec, vec], out_specs=[row, row, vec, vec],
        compiler_params=_cp("arbitrary"),
    )(yc, P, da, lg, lb)


def _tril(w):
    row = lax.broadcasted_iota(jnp.int32, w.shape, 0)
    col = lax.broadcasted_iota(jnp.int32, w.shape, 1)
    return jnp.where(col <= row, w, 0.0)


def _gm_fwd(P, lg, lb, ws, bst, D, name):
    T = P.shape[0]
    G, ch, _ = ws.shape
    gw = D // G

    def body(p_ref, g_ref, b_ref, ws_ref, bs_ref, a_ref):
        uv, vv, zv = (p_ref[:, k * D:(k + 1) * D].astype(F32) for k in range(3))
        xh, _ = _layer_norm_stats(_gelu(vv))
        vn = (xh * g_ref[...] + b_ref[...]).astype(MXU)
        gate = _gelu(uv) * _silu(zv)
        for g in range(G):
            cols = slice(g * gw, (g + 1) * gw)
            s = lax.dot_general(_tril(ws_ref[g]).astype(MXU), vn[:, cols], NN, preferred_element_type=F32)
            a_ref[:, cols] = (gate[:, cols] * (s + bs_ref[:, g:g + 1])).astype(a_ref.dtype)

    vec = pl.BlockSpec((1, D), lambda i: (0, 0))
    return pl.pallas_call(
        body, name=name, out_shape=jax.ShapeDtypeStruct((T, D), MXU), grid=(T // ch,),
        in_specs=[pl.BlockSpec((ch, 3 * D), lambda i: (i, 0)), vec, vec,
                  pl.BlockSpec((G, ch, ch), lambda i: (0, 0, 0)), pl.BlockSpec((ch, G), lambda i: (0, 0))],
        out_specs=pl.BlockSpec((ch, D), lambda i: (i, 0)), compiler_params=_cp("parallel"),
    )(P, lg, lb, ws, bst)


def _gm_bwd(P, da, lg, lb, ws, bst, D, name):
    T = P.shape[0]
    G, ch, _ = ws.shape
    gw = D // G

    def body(p_ref, da_ref, g_ref, b_ref, ws_ref, bs_ref, dp_ref, dg_ref, db_ref, dws_ref, dbs_ref, dvn_s):
        @pl.when(pl.program_id(0) == 0)
        def _():
            dg_ref[...] = jnp.zeros_like(dg_ref)
            db_ref[...] = jnp.zeros_like(db_ref)
            dws_ref[...] = jnp.zeros_like(dws_ref)
            dbs_ref[...] = jnp.zeros_like(dbs_ref)

        uv, vv, zv = (p_ref[:, k * D:(k + 1) * D].astype(F32) for k in range(3))
        dav = da_ref[...].astype(F32)
        xh, rstd = _layer_norm_stats(_gelu(vv))
        vn = (xh * g_ref[...] + b_ref[...]).astype(MXU)
        ug, sz = _gelu(uv), _silu(zv)
        ds_all = dav * sz * ug
        for g in range(G):
            cols = slice(g * gw, (g + 1) * gw)
            wm = _tril(ws_ref[g]).astype(MXU)
            s = lax.dot_general(wm, vn[:, cols], NN, preferred_element_type=F32) + bs_ref[:, g:g + 1]
            dp_ref[:, g * gw:(g + 1) * gw] = (dav[:, cols] * sz[:, cols] * s * _gelu_grad(uv[:, cols])).astype(dp_ref.dtype)
            dp_ref[:, 2 * D + g * gw:2 * D + (g + 1) * gw] = (
                dav[:, cols] * ug[:, cols] * s * _silu_grad(zv[:, cols])).astype(dp_ref.dtype)
            ds = ds_all[:, cols]
            dsb = ds.astype(MXU)
            dvn_s[:, cols] = lax.dot_general(wm, dsb, TN, preferred_element_type=F32)
            dws_ref[g] += _tril(lax.dot_general(dsb, vn[:, cols], NT, preferred_element_type=F32))
            dbs_ref[g] += jnp.broadcast_to(jnp.sum(ds, axis=1, keepdims=True), (ch, LANE))
        dvn = dvn_s[...]
        dg_ref[...] += jnp.sum(dvn * xh, axis=0, keepdims=True)
        db_ref[...] += jnp.sum(dvn, axis=0, keepdims=True)
        dp_ref[:, D:2 * D] = (_layer_norm_bwd(dvn * g_ref[...], xh, rstd) * _gelu_grad(vv)).astype(dp_ref.dtype)

    vec = pl.BlockSpec((1, D), lambda i: (0, 0))
    return pl.pallas_call(
        body, name=name,
        out_shape=[jax.ShapeDtypeStruct((T, 3 * D), MXU), jax.ShapeDtypeStruct((1, D), F32), jax.ShapeDtypeStruct((1, D), F32),
                   jax.ShapeDtypeStruct((G, ch, ch), F32), jax.ShapeDtypeStruct((G, ch, LANE), F32)],
        grid=(T // ch,),
        in_specs=[pl.BlockSpec((ch, 3 * D), lambda i: (i, 0)), pl.BlockSpec((ch, D), lambda i: (i, 0)), vec, vec,
                  pl.BlockSpec((G, ch, ch), lambda i: (0, 0, 0)), pl.BlockSpec((ch, G), lambda i: (0, 0))],
        out_specs=[pl.BlockSpec((ch, 3 * D), lambda i: (i, 0)), vec, vec,
                   pl.BlockSpec((G, ch, ch), lambda i: (0, 0, 0)), pl.BlockSpec((G, ch, LANE), lambda i: (0, 0, 0))],
        scratch_shapes=[pltpu.VMEM((ch, D), F32)],
        compiler_params=_cp("arbitrary"),
    )(P, da, lg, lb, ws, bst)


def _adam_math(w, g, m, v):
    m = ADAM_B1 * m + (1.0 - ADAM_B1) * g
    v = ADAM_B2 * v + (1.0 - ADAM_B2) * (g * g)
    m_hat = m / (1.0 - ADAM_B1 ** ADAM_STEP)
    v_hat = v / (1.0 - ADAM_B2 ** ADAM_STEP)
    return -ADAM_LR * (m_hat / (jnp.sqrt(v_hat) + ADAM_EPS) + ADAM_WD * w), m, v


def _adam(w, m, v, gparts, name, hosts=None):
    R, C = w.shape
    n = gparts.shape[0]
    tr = _tile(R, max(SUBLANE, (1 << 18) // C), 16 if gparts.dtype != F32 else SUBLANE)

    def body(w_ref, m_ref, v_ref, gp_ref, g_ref, d_ref, mo_ref, vo_ref):
        g = gp_ref[0].astype(F32)
        for k in range(1, n):
            g = g + gp_ref[k].astype(F32)
        g_ref[...] = g
        d_ref[...], mo_ref[...], vo_ref[...] = _adam_math(w_ref[...], g, m_ref[...], v_ref[...])

    blk = pl.BlockSpec((tr, C), lambda i: (i, 0))
    return _hosted(
        hosts, R * C * 36.0 / HBM_BYTES_PER_US, body, force=True, name=name,
        out_shape=[jax.ShapeDtypeStruct((R, C), F32)] * 4, grid=(R // tr,),
        in_specs=[blk, blk, blk, pl.BlockSpec((n, tr, C), lambda i: (0, i, 0))], out_specs=[blk] * 4,
        args=[w, m, v, gparts], semantics=("parallel",))


def _sum_blocks(parts, name):
    n, R, C = parts.shape
    tr = _tile(R, 512, SUBLANE)

    def body(p_ref, o_ref):
        acc = p_ref[0]
        for k in range(1, n):
            acc = acc + p_ref[k]
        o_ref[...] = acc

    return pl.pallas_call(
        body, name=name, out_shape=jax.ShapeDtypeStruct((R, C), F32), grid=(R // tr,),
        in_specs=[pl.BlockSpec((n, tr, C), lambda i: (0, i, 0))], out_specs=pl.BlockSpec((tr, C), lambda i: (i, 0)),
        compiler_params=_cp("parallel"),
    )(parts)


def _pack_rows(shape):
    return -(-int(np.prod(shape)) // (SUBLANE * LANE)) * SUBLANE


def _pack(arrs):
    parts = []
    for a in arrs:
        flat = a.reshape(-1).astype(F32)
        rows = _pack_rows(a.shape)
        parts.append(jnp.pad(flat, (0, rows * LANE - flat.shape[0])).reshape(rows, LANE))
    return jnp.concatenate(parts, axis=0)


def _unpack(buf, shapes):
    out, off = [], 0
    for s in shapes:
        rows, n = _pack_rows(s), int(np.prod(s))
        out.append(buf[off:off + rows].reshape(-1)[:n].reshape(s))
        off += rows
    return out


def _cols_full(g):
    return jnp.transpose(g, (1, 0, 2)).reshape(g.shape[1], N_DEV * g.shape[2])


def _cols_blocks(w):
    R, N = w.shape
    return jnp.transpose(w.reshape(R, N_DEV, N // N_DEV), (1, 0, 2)).astype(WIRE)


def _rows_blocks(w):
    return w.reshape(N_DEV, w.shape[0] // N_DEV, w.shape[1]).astype(WIRE)


def kernel(x, positions, norm_pre, norm_post, w_in_mla, mla_q_norm, w_uq, mla_kv_norm, w_ukv, w_out_mla, w_in_sc, sc_conv, w_out_sc, w_in_gm, gm_ln_g, gm_ln_b, gm_w_s, gm_b_s, w_out_gm, w_in_cf, cf_dw, cf_dw_b, cf_ln_g, cf_ln_b, w_out_cf, loss_target, m_norm_pre, m_norm_post, m_w_in_mla, m_mla_q_norm, m_w_uq, m_mla_kv_norm, m_w_ukv, m_w_out_mla, m_w_in_sc, m_sc_conv, m_w_out_sc, m_w_in_gm, m_gm_ln_g, m_gm_ln_b, m_gm_w_s, m_gm_b_s, m_w_out_gm, m_w_in_cf, m_cf_dw, m_cf_dw_b, m_cf_ln_g, m_cf_ln_b, m_w_out_cf, v_norm_pre, v_norm_post, v_w_in_mla, v_mla_q_norm, v_w_uq, v_mla_kv_norm, v_w_ukv, v_w_out_mla, v_w_in_sc, v_sc_conv, v_w_out_sc, v_w_in_gm, v_gm_ln_g, v_gm_ln_b, v_gm_w_s, v_gm_b_s, v_w_out_gm, v_w_in_cf, v_cf_dw, v_cf_dw_b, v_cf_ln_g, v_cf_ln_b, v_w_out_cf):
    names = ['norm_pre', 'norm_post', 'w_in_mla', 'mla_q_norm', 'w_uq', 'mla_kv_norm', 'w_ukv', 'w_out_mla', 'w_in_sc',
             'sc_conv', 'w_out_sc', 'w_in_gm', 'gm_ln_g', 'gm_ln_b', 'gm_w_s', 'gm_b_s', 'w_out_gm', 'w_in_cf', 'cf_dw',
             'cf_dw_b', 'cf_ln_g', 'cf_ln_b', 'w_out_cf']
    W = dict(zip(names, (norm_pre, norm_post, w_in_mla, mla_q_norm, w_uq, mla_kv_norm, w_ukv, w_out_mla, w_in_sc, sc_conv,
                         w_out_sc, w_in_gm, gm_ln_g, gm_ln_b, gm_w_s, gm_b_s, w_out_gm, w_in_cf, cf_dw, cf_dw_b, cf_ln_g,
                         cf_ln_b, w_out_cf)))
    Mo = dict(zip(names, (m_norm_pre, m_norm_post, m_w_in_mla, m_mla_q_norm, m_w_uq, m_mla_kv_norm, m_w_ukv, m_w_out_mla,
                          m_w_in_sc, m_sc_conv, m_w_out_sc, m_w_in_gm, m_gm_ln_g, m_gm_ln_b, m_gm_w_s, m_gm_b_s, m_w_out_gm,
                          m_w_in_cf, m_cf_dw, m_cf_dw_b, m_cf_ln_g, m_cf_ln_b, m_w_out_cf)))
    Vo = dict(zip(names, (v_norm_pre, v_norm_post, v_w_in_mla, v_mla_q_norm, v_w_uq, v_mla_kv_norm, v_w_ukv, v_w_out_mla,
                          v_w_in_sc, v_sc_conv, v_w_out_sc, v_w_in_gm, v_gm_ln_g, v_gm_ln_b, v_gm_w_s, v_gm_b_s, v_w_out_gm,
                          v_w_in_cf, v_cf_dw, v_cf_dw_b, v_cf_ln_g, v_cf_ln_b, v_w_out_cf)))
    big = ['w_in_mla', 'w_uq', 'w_ukv', 'w_out_mla', 'w_in_sc', 'w_out_sc', 'w_in_gm', 'w_out_gm', 'w_in_cf', 'w_out_cf']
    row_sharded = {'w_out_mla', 'w_out_sc', 'w_out_gm', 'w_out_cf'}
    chan = ['sc_conv', 'gm_ln_g', 'gm_ln_b', 'cf_dw', 'cf_dw_b', 'cf_ln_g', 'cf_ln_b']
    repl = ['norm_pre', 'norm_post', 'mla_q_norm', 'mla_kv_norm', 'gm_w_s', 'gm_b_s']

    T, D = x.shape[1], x.shape[2]
    xs, tgt = x[0], loss_target[0]
    pos = positions.reshape(T, 1)
    qr, kvr = mla_q_norm.shape[-1], mla_kv_norm.shape[-1]
    H = w_uq.shape[-1] * N_DEV // (NOPE + ROPE)
    HV = H * VHEAD
    c3 = qr + kvr + ROPE
    pa = -(-c3 // 512) * 512
    assert qr % kvr == 0 and qr % LANE == 0 and kvr % LANE == 0 and pa >= qr + kvr + LANE
    mx, my, mc = _mesh_pos()
    me = 4 * mx + 2 * my + mc
    core = mc.astype(jnp.int32).reshape(1)
    cs = D // N_DEV

    def wire(nm):
        return W[nm][0].astype(WIRE)

    hosts = _Hosts()
    G = {}

    def gather(nm, name=None):
        shard = wire(nm)
        in_place = nm not in row_sharded and shard.shape[1] % LANE == 0
        job = _GatherJob([shard], cols=in_place)
        if name is None:
            hosts.add(job, lambda outs: G.update({nm: outs[0]}))
        else:
            G[nm] = _run_job(job, name)[0]

    def gathered_rows(nm):
        if nm not in G:
            hosts.flush("ag_late")
        return G[nm].reshape(G[nm].shape[0] * G[nm].shape[1], G[nm].shape[2])

    def gathered_cols(nm):
        if nm not in G:
            hosts.flush("ag_late")
        return G[nm] if G[nm].ndim == 2 else _cols_full(G[nm])

    gather('w_in_mla')
    h = _rms_fwd(xs, norm_pre[0:1], "pre0", hosts=hosts)
    chan_rows = [W[nm][0].reshape(-1, cs) for nm in chan]
    chan_cnt = [r.shape[0] for r in chan_rows]
    chan_local = jnp.concatenate(chan_rows, axis=0)
    chan_pad = -chan_local.shape[0] % SUBLANE
    chan_full = _cols_full(_run_job(_GatherJob([jnp.pad(chan_local, ((0, chan_pad), (0, 0)))]), "ag_small")[0])
    offs = np.cumsum([0] + chan_cnt)
    CH = {nm: chan_full[offs[i]:offs[i + 1]] for i, nm in enumerate(chan)}

    w_in_full = gathered_cols('w_in_mla')
    w_cat = jnp.concatenate([w_in_full[:, :c3], jnp.zeros((D, pa - c3), WIRE), w_in_full[:, c3:]], axis=1)
    in_names = ['w_in_mla', 'w_in_sc', 'w_in_gm', 'w_in_cf']
    out_names = ['w_out_mla', 'w_out_sc', 'w_out_gm', 'w_out_cf']
    WIN, WOUT = [w_cat], []

    half = ROPE // 2
    invf_np = np.zeros((1, LANE), np.float32)
    invf_np[0, :ROPE] = np.tile(np.float32(ROPE_THETA) ** (-np.arange(half, dtype=np.float32) / np.float32(half)), 2)
    invf = jnp.asarray(invf_np)
    gm_ws = gm_w_s[0]
    gm_bst = jnp.transpose(gm_b_s[0])

    xin, hs, Ps, acts, ys, keep = [xs], [], [], [], [], {}
    for i in range(4):
        if i == 0:
            gather('w_uq')
            gather('w_ukv')
        else:
            WIN.append(gathered_cols(in_names[i]))
            gather(in_names[i + 1] if i < 3 else out_names[3])
        P = _mm(h, WIN[i], out_dtype=MXU, name=f"in{i}", hosts=hosts)
        if i == 0:
            w_uq_pad = jnp.pad(gathered_cols('w_uq').reshape(qr, H, NOPE + ROPE),
                               ((0, 0), (0, 0), (0, QPAD - NOPE - ROPE))).reshape(qr, H * QPAD)
            w_ukv3 = gathered_cols('w_ukv').reshape(kvr, H, NOPE + VHEAD)
            w_k_pad = jnp.pad(w_ukv3[:, :, :NOPE], ((0, 0), (0, 0), (0, QPAD - NOPE))).reshape(kvr, H * QPAD)
            w_v = w_ukv3[:, :, NOPE:].reshape(kvr, HV)
            cqn, ckvn = _mla_norms_fwd(P, mla_q_norm, mla_kv_norm, qr, kvr, "mla_norms")
            qpad = _mm(cqn, w_uq_pad, out_dtype=MXU, name="mla_q")
            kpad = _mm(ckvn, w_k_pad, out_dtype=MXU, name="mla_k")
            vv = _mm(ckvn, w_v, out_dtype=MXU, name="mla_v")
            qcat, kcat, vcat = _rope_fwd(qpad, kpad, vv, P, pos, invf, (qr + kvr) // LANE, H, "rope")
            for nm in (out_names[0], in_names[1], out_names[1]):
                gather(nm)
            o, act, lse = _flash_fwd(qcat, kcat, vcat, P, pa // VHEAD, H, "attn", hosts=hosts)
            keep.update(cqn=cqn, ckvn=ckvn, qcat=qcat, kcat=kcat, v=vv, o=o, lse=lse)
        elif i == 1:
            act = _sc_fwd(P, CH['sc_conv'], D, "sc_mix")
        elif i == 2:
            act = _gm_fwd(P, CH['gm_ln_g'], CH['gm_ln_b'], gm_ws, gm_bst, D, "gm_mix")
        else:
            yc = _cf_conv_fwd(P, CH['cf_dw'], CH['cf_dw_b'], D, "cf_conv")
            act = _cf_act_fwd(yc, P, CH['cf_ln_g'], CH['cf_ln_b'], D, "cf_act")
            keep.update(yc=yc)
        WOUT.append(gathered_rows(out_names[i]))
        if i == 1:
            gather(out_names[2])
        y = _mm(act, WOUT[i], out_dtype=F32, name=f"out{i}", hosts=hosts)
        hs.append(h), Ps.append(P), acts.append(act), ys.append(y)
        if i < 3:
            x_next, h = _post_fwd(y, norm_post[i:i + 1], xin[i], f"post{i}", next_gain=norm_pre[i + 1:i + 2])
            xin.append(x_next)
        else:
            dx, loss_part = _post_fwd(y, norm_post[i:i + 1], xin[i], f"post{i}", target=tgt)
    loss = lax.psum(loss_part[0, 0], ("x", "y", "c"))

    hosts.flush("ag_late")
    dnorm_pre, dnorm_post, small_g, RS = [None] * 4, [None] * 4, {}, {}
    for i in (3, 2, 1, 0):
        dy, dnorm_post[i] = _rms_bwd(ys[i], dx, norm_post[i:i + 1], f"post_bwd{i}", out_dtype=MXU)
        da = _mm(dy, WOUT[i], tb=True, out_dtype=MXU, name=f"out_bwd{i}", hosts=hosts)
        dw_out = _mm(acts[i], dy, ta=True, out_dtype=WIRE, name=f"out_dw{i}", hosts=hosts).reshape(N_DEV, -1, D)
        _queue_reduce_scatter(hosts, {out_names[i]: dw_out}, core, f"rs_out{i}", RS)
        P = Ps[i]
        if i == 3:
            dyc, d_z, small_g['cf_ln_g'], small_g['cf_ln_b'] = _cf_act_bwd(
                keep['yc'], P, da, CH['cf_ln_g'], CH['cf_ln_b'], D, "cf_act_bwd")
            d_a, d_g, small_g['cf_dw'], small_g['cf_dw_b'] = _cf_conv_bwd(P, dyc, CH['cf_dw'], D, "cf_conv_bwd")
            dP = jnp.concatenate([d_a, d_g, d_z], axis=1)
        elif i == 2:
            dP, small_g['gm_ln_g'], small_g['gm_ln_b'], small_g['gm_w_s'], dbs = _gm_bwd(
                P, da, CH['gm_ln_g'], CH['gm_ln_b'], gm_ws, gm_bst, D, "gm_mix_bwd")
            small_g['gm_b_s'] = dbs[:, :, 0]
        elif i == 1:
            dP, small_g['sc_conv'] = _sc_bwd(P, da, CH['sc_conv'], D, "sc_mix_bwd")
        else:
            do, d_z, delta = _mla_gate_bwd(da, keep['o'], P, pa, H, "gate_bwd")
            dqcat, dkcat, dv = _flash_bwd(keep['qcat'], keep['kcat'], keep['v'], do, keep['lse'], delta, H, "attn_bwd",
                                          hosts=hosts)
            dqpad, dkr = _rope_bwd(dqcat, dkcat, pos, invf, H, "rope_bwd")
            dcqn = _mm(dqpad, w_uq_pad, tb=True, out_dtype=F32, name="mla_q_bwd", hosts=hosts)
            dckvn_k = _mm(dkcat, w_k_pad, tb=True, out_dtype=F32, name="mla_k_bwd", hosts=hosts)
            dckvn_v = _mm(dv, w_v, tb=True, out_dtype=F32, name="mla_v_bwd", hosts=hosts)
            dw_uq_pad = _mm(keep['cqn'], dqpad, ta=True, out_dtype=F32, name="mla_q_dw", hosts=hosts)
            dw_k_pad = _mm(keep['ckvn'], dkcat, ta=True, out_dtype=F32, name="mla_k_dw", hosts=hosts)
            dw_v = _mm(keep['ckvn'], dv, ta=True, out_dtype=F32, name="mla_v_dw", hosts=hosts)
            dw_uq = _cols_blocks(dw_uq_pad.reshape(qr, H, QPAD)[:, :, :NOPE + ROPE].reshape(qr, H * (NOPE + ROPE)))
            dw_ukv = _cols_blocks(jnp.concatenate(
                [dw_k_pad.reshape(kvr, H, QPAD)[:, :, :NOPE], dw_v.reshape(kvr, H, VHEAD)], axis=2).reshape(kvr, H * (NOPE + VHEAD)))
            _queue_reduce_scatter(hosts, {'w_uq': dw_uq, 'w_ukv': dw_ukv}, core, "rs_lat", RS)
            d_pa, small_g['mla_q_norm'], small_g['mla_kv_norm'] = _mla_norms_bwd(
                P, dcqn, dckvn_k, dckvn_v, dkr, mla_q_norm, mla_kv_norm, qr, kvr, pa, "mla_norms_bwd")
            dP = jnp.concatenate([d_pa, d_z], axis=1)
        dh = _mm(dP, WIN[i], tb=True, out_dtype=MXU, name=f"in_bwd{i}", hosts=hosts)
        dx, dnorm_pre[i] = _rms_bwd(xin[i], dh, norm_pre[i:i + 1], f"pre_bwd{i}", resid=dx)
        if i == 0:
            dw_cat = _mm(hs[i], dP, ta=True, out_dtype=F32, name="in_dw0", hosts=hosts)
            dw_in = _cols_blocks(jnp.concatenate([dw_cat[:, :c3], dw_cat[:, pa:]], axis=1))
        else:
            dw_in = _mm(hs[i], dP, ta=True, out_dtype=WIRE, name=f"in_dw{i}", blocks=N_DEV, hosts=hosts)
        _queue_reduce_scatter(hosts, {in_names[i]: dw_in}, core, f"rs_in{i}", RS)
    grad_x = dx[None]

    small_g['norm_pre'] = jnp.concatenate(dnorm_pre, axis=0)
    small_g['norm_post'] = jnp.concatenate(dnorm_post, axis=0)
    small_names = repl + chan
    small_shapes = [(small_g[nm].shape) for nm in small_names]
    tail, big_out = {}, {}
    hosts.add(_GatherJob([_pack([small_g[nm] for nm in small_names])]), lambda outs: tail.update(grads=outs[0]))

    def adam_big(nm, carrier=None):
        if nm not in RS:
            hosts.flush("rs_late")
        big_out[nm] = [o[None] for o in _adam(W[nm][0], Mo[nm][0], Vo[nm][0], RS[nm], "adam_" + nm, hosts=carrier)]

    for nm in ('w_in_sc', 'w_in_gm', 'w_in_cf', 'w_out_sc', 'w_out_gm', 'w_out_cf'):
        adam_big(nm, hosts)
    hosts.flush("rs_late")
    for nm in ('w_in_mla', 'w_uq', 'w_ukv', 'w_out_mla'):
        adam_big(nm)
    totals = dict(zip(small_names, _unpack(_sum_blocks(tail['grads'], "sum_grads"), small_shapes)))
    local_g = []
    for nm in small_names:
        g = totals[nm]
        if nm in chan:
            g = lax.dynamic_slice_in_dim(g.reshape(-1, D), me * cs, cs, axis=1)
        local_g.append(g.reshape(W[nm].shape))
    local_shapes = [W[nm].shape for nm in small_names]
    sm = _adam(_pack([W[nm] for nm in small_names]), _pack([Mo[nm] for nm in small_names]),
               _pack([Vo[nm] for nm in small_names]), _pack(local_g)[None], "adam_small")
    small_out = [dict(zip(small_names, _unpack(buf, local_shapes))) for buf in sm]

    outs = [loss, grad_x]
    for k in range(4):
        outs += [big_out[nm][k] if nm in big_out else small_out[k][nm] for nm in names]
    return tuple(outs)
```

```python
import functools
import math

import numpy as np
import jax
import jax.numpy as jnp
from jax import lax
from jax.experimental import pallas as pl
from jax.experimental.pallas import tpu as pltpu

F32 = jnp.float32
MXU = jnp.bfloat16
WIRE = jnp.bfloat16
MESH = pl.DeviceIdType.MESH
N_DEV = 8

NORM_EPS = 1e-6
LN_EPS = 1e-5
ROPE_THETA = 10000.0
NOPE, ROPE, VHEAD = 128, 64, 128
QPAD = 256
ATT_SCALE = float((NOPE + ROPE) ** -0.5)
LOG2E = 1.4426950408889634
INV_SQRT2 = 0.7071067811865476
INV_SQRT_2PI = 0.3989422804014327
ADAM_LR, ADAM_B1, ADAM_B2, ADAM_EPS, ADAM_WD, ADAM_STEP = 0.001, 0.9, 0.999, 1e-08, 0.01, 10

LANE = 128
SUBLANE = 8
VMEM_LIMIT = 56 * 1024 * 1024
MM_TILE = 1024
MM_TK = 4096
ATT_TILE = 512
ROW_TILE = 128
CONV_CHUNK = 256
CONV_CB = 128
MXU_FLOP_PER_US = 9.2e8
HBM_BYTES_PER_US = 3.0e6
ATT_FWD_US, ATT_BWD_US = 1150.0, 1600.0
HOST_SLACK = 1.35

NT = (((1,), (1,)), ((), ()))
TN = (((0,), (0,)), ((), ()))
NN = (((1,), (0,)), ((), ()))


def _tile(dim, pref, align):
    t = min(pref, dim)
    t -= t % align
    while t >= align:
        if dim % t == 0:
            return t
        t -= align
    return dim


def _cp(*sem):
    return pltpu.CompilerParams(dimension_semantics=sem, vmem_limit_bytes=VMEM_LIMIT)


def _sig(v):
    return jax.nn.sigmoid(v)


def _silu(v):
    return v * _sig(v)


def _silu_grad(v):
    s = _sig(v)
    return s * (1.0 + v * (1.0 - s))


def _gelu(v):
    return 0.5 * v * (1.0 + lax.erf(v * INV_SQRT2))


def _gelu_grad(v):
    return 0.5 * (1.0 + lax.erf(v * INV_SQRT2)) + v * jnp.exp(-0.5 * v * v) * INV_SQRT_2PI


def _mesh_pos():
    return lax.axis_index("x"), lax.axis_index("y"), lax.axis_index("c")


def _other_chips(x, y):
    return [(1 - x, y), (x, 1 - y), (1 - x, 1 - y)]


class _GatherJob:
    US_PER_MB = 38.0

    def __init__(self, arrs, cols=False):
        n = len(arrs)
        self.ins = list(arrs)
        self.cols = cols
        if cols:
            assert all(a.ndim == 2 and a.shape[1] % LANE == 0 for a in arrs)
            self.out_shape = [jax.ShapeDtypeStruct((a.shape[0], N_DEV * a.shape[1]), a.dtype) for a in arrs]
        else:
            self.out_shape = [jax.ShapeDtypeStruct((N_DEV,) + a.shape, a.dtype) for a in arrs]
        self.sems = [pltpu.SemaphoreType.DMA((n, 7)), pltpu.SemaphoreType.DMA((n, 7)), pltpu.SemaphoreType.DMA((n,))]
        self.cost = self.US_PER_MB * sum(a.size * a.dtype.itemsize for a in arrs) / 1e6

    def _parts(self, ins, outs, sems):
        send_sems, recv_sems, loc_sems = sems
        x, y, c = _mesh_pos()
        n = len(ins)

        def block_of(a, dev):
            if not self.cols:
                return outs[a].at[dev]
            width = ins[a].shape[1]
            return outs[a].at[:, pl.ds(pl.multiple_of(dev * width, LANE), width)]

        def copy(a, k, block, to, src=None):
            dst = block_of(a, 4 * block[0] + 2 * block[1] + block[2])
            return pltpu.make_async_remote_copy(
                src_ref=dst if src is None else src, dst_ref=dst,
                send_sem=send_sems.at[a, k], recv_sem=recv_sems.at[a, k],
                device_id=to, device_id_type=MESH)

        me, sib = (x, y, c), (x, y, 1 - c)
        xn, yn, dg = (1 - x, y, c), (x, 1 - y, c), (1 - x, 1 - y, c)
        on_src = (x + (1 - c) * (1 - 2 * x), y + c * (1 - 2 * y), c)
        on_dst = (x + c * (1 - 2 * x), y + (1 - c) * (1 - 2 * y), c)
        locs = [pltpu.make_async_copy(ins[a], block_of(a, 4 * x + 2 * y + c), loc_sems.at[a]) for a in range(n)]
        first, landed, onward, to_sib, diag, diag_sib, from_sib = [], [], [], [], [], [], []
        for a in range(n):
            first += [copy(a, 0, me, sib, src=ins[a]), copy(a, 1, me, xn, src=ins[a]), copy(a, 2, me, yn, src=ins[a])]
            landed += [copy(a, 1, xn, me), copy(a, 2, yn, me)]
            onward.append(copy(a, 3, on_src, on_dst))
            to_sib += [copy(a, 4, xn, sib), copy(a, 5, yn, sib)]
            diag.append(copy(a, 3, dg, me))
            diag_sib.append(copy(a, 6, dg, sib))
            from_sib += [copy(a, 0, sib, me)] + [copy(a, 4 + j, (*blk[:2], 1 - c), me) for j, blk in enumerate((xn, yn, dg))]
        return dict(locs=locs, first=first, landed=landed, onward=onward, to_sib=to_sib, diag=diag,
                    diag_sib=diag_sib, from_sib=from_sib)

    def first(self, ins, outs, sems):
        p = self._parts(ins, outs, sems)
        for cp in p["locs"] + p["first"]:
            cp.start()

    def mid(self, ins, outs, sems):
        p = self._parts(ins, outs, sems)
        for cp in p["landed"]:
            cp.wait_recv()
        for cp in p["onward"] + p["to_sib"]:
            cp.start()

    def last(self, ins, outs, sems):
        p = self._parts(ins, outs, sems)
        for got, fwd in zip(p["diag"], p["diag_sib"]):
            got.wait_recv()
            fwd.start()
        for cp in p["from_sib"]:
            cp.wait_recv()
        for cp in p["first"] + p["onward"] + p["to_sib"] + p["diag_sib"]:
            cp.wait_send()
        for cp in p["locs"]:
            cp.wait()


class _SwapJob:
    US_PER_MB = 0.8

    def __init__(self, parts):
        n = len(parts)
        self.ins = list(parts)
        self.out_shape = [jax.ShapeDtypeStruct((4, 1) + p.shape[2:], p.dtype) for p in parts]
        self.sems = [pltpu.SemaphoreType.DMA((n,)), pltpu.SemaphoreType.DMA((n,))]
        self.cost = 5.0 + self.US_PER_MB * sum(p.size * p.dtype.itemsize for p in parts) / 1e6

    def _copies(self, ins, outs, sems):
        send_sems, recv_sems = sems
        x, y, c = _mesh_pos()
        return [pltpu.make_async_remote_copy(
            src_ref=ins[a].at[:, pl.ds(1 - c, 1)], dst_ref=outs[a],
            send_sem=send_sems.at[a], recv_sem=recv_sems.at[a],
            device_id=(x, y, 1 - c), device_id_type=MESH) for a in range(len(ins))]

    def first(self, ins, outs, sems):
        for cp in self._copies(ins, outs, sems):
            cp.start()

    def mid(self, ins, outs, sems):
        pass

    def last(self, ins, outs, sems):
        for cp in self._copies(ins, outs, sems):
            cp.wait()


class _ChipsJob:
    US_PER_MB = 11.0

    def __init__(self, qs):
        n = len(qs)
        self.ins = list(qs)
        self.out_shape = [jax.ShapeDtypeStruct(q.shape, q.dtype) for q in qs]
        self.sems = [pltpu.SemaphoreType.DMA((n, 3)), pltpu.SemaphoreType.DMA((n, 3)), pltpu.SemaphoreType.DMA((n,))]
        self.cost = 5.0 + self.US_PER_MB * sum(q.size * q.dtype.itemsize for q in qs) / 1e6

    def _copies(self, ins, outs, sems):
        send_sems, recv_sems, loc_sems = sems
        x, y, c = _mesh_pos()
        cps = []
        for a in range(len(ins)):
            cps.append(pltpu.make_async_copy(ins[a].at[2 * x + y], outs[a].at[3], loc_sems.at[a]))
            for k, chip in enumerate(_other_chips(x, y)):
                cps.append(pltpu.make_async_remote_copy(
                    src_ref=ins[a].at[2 * chip[0] + chip[1]], dst_ref=outs[a].at[k],
                    send_sem=send_sems.at[a, k], recv_sem=recv_sems.at[a, k],
                    device_id=(*chip, c), device_id_type=MESH))
        return cps

    def first(self, ins, outs, sems):
        for cp in self._copies(ins, outs, sems):
            cp.start()

    def mid(self, ins, outs, sems):
        pass

    def last(self, ins, outs, sems):
        for cp in self._copies(ins, outs, sems):
            cp.wait()


def _call(body, *, name, out_shape, in_specs, out_specs, args, grid=(), scratch_shapes=(), semantics=(), job=None):
    if job is None:
        outs = pl.pallas_call(
            body, name=name, out_shape=out_shape, grid=grid, in_specs=in_specs, out_specs=out_specs,
            scratch_shapes=list(scratch_shapes), compiler_params=_cp(*semantics))(*args)
        return outs, None
    single = not isinstance(out_shape, (list, tuple))
    host_out = [out_shape] if single else list(out_shape)
    host_ospecs = [out_specs] if single else list(out_specs)
    n_in, n_out, n_scr = len(in_specs), len(host_out), len(scratch_shapes)
    j_in, j_out = len(job.ins), len(job.out_shape)
    total = int(np.prod(grid)) if grid else 1
    mid_step = max(total // 2, 1)

    def full(*refs):
        h_in, jin = refs[:n_in], refs[n_in:n_in + j_in]
        o = n_in + j_in
        h_out, jout = refs[o:o + n_out], refs[o + n_out:o + n_out + j_out]
        o += n_out + j_out
        h_scr, jsem = refs[o:o + n_scr], refs[o + n_scr:]
        if total == 1:
            job.first(jin, jout, jsem)
            body(*h_in, *h_out, *h_scr)
            job.mid(jin, jout, jsem)
            job.last(jin, jout, jsem)
            return
        step = pl.program_id(0)
        for d in range(1, len(grid)):
            step = step * grid[d] + pl.program_id(d)
        pl.when(step == 0)(lambda: job.first(jin, jout, jsem))
        body(*h_in, *h_out, *h_scr)
        pl.when(step == mid_step)(lambda: job.mid(jin, jout, jsem))
        pl.when(step == total - 1)(lambda: job.last(jin, jout, jsem))

    hbm = pl.BlockSpec(memory_space=pltpu.HBM)
    outs = pl.pallas_call(
        full, name=name, out_shape=host_out + list(job.out_shape), grid=grid,
        in_specs=list(in_specs) + [hbm] * j_in, out_specs=host_ospecs + [hbm] * j_out,
        scratch_shapes=list(scratch_shapes) + list(job.sems),
        compiler_params=_cp(*(("arbitrary",) * len(grid))))(*args, *job.ins)
    host = outs[0] if single else list(outs[:n_out])
    return host, list(outs[n_out:])


def _run_job(job, name):
    def body():
        pass

    return _call(body, name=name, out_shape=[], in_specs=[], out_specs=[], args=[], job=job)[1]


class _Hosts:
    def __init__(self):
        self.pending = []
        self.flushed = 0

    def add(self, job, done):
        self.pending.append((job, done))

    def take(self, duration, force=False):
        taken, room = [], HOST_SLACK * duration
        for e in list(self.pending):
            if e[0].cost <= room:
                taken.append(e)
                room -= e[0].cost
        if not taken and force and self.pending:
            taken = [min(self.pending, key=lambda e: e[0].cost)]
        for e in taken:
            self.pending.remove(e)
        return taken

    def flush(self, name):
        while self.pending:
            job, done = self.pending.pop(0)
            done(_run_job(job, f"{name}{self.flushed}"))
            self.flushed += 1


def _pair_add(part4, recv, core, name):
    _, _, R, C = part4.shape
    tr = _tile(R, max(SUBLANE, (1 << 19) // C), 16)

    def body(core_ref, p_ref, r_ref, o_ref):
        o_ref[...] = (p_ref[...].astype(F32) + r_ref[...].astype(F32)).astype(o_ref.dtype)

    return pl.pallas_call(
        body, name=name,
        out_shape=jax.ShapeDtypeStruct((4, R, C), WIRE),
        grid_spec=pltpu.PrefetchScalarGridSpec(
            num_scalar_prefetch=1, grid=(4, R // tr),
            in_specs=[pl.BlockSpec((None, None, tr, C), lambda s, i, cr: (s, cr[0], i, 0)),
                      pl.BlockSpec((None, None, tr, C), lambda s, i, cr: (s, 0, i, 0))],
            out_specs=pl.BlockSpec((None, tr, C), lambda s, i, cr: (s, i, 0))),
        compiler_params=_cp("parallel", "parallel"),
    )(core, part4, recv)


def _queue_reduce_scatter(hosts, parts, core, name, sink):
    keys = list(parts)
    p4 = [parts[k].reshape((4, 2) + parts[k].shape[1:]) for k in keys]

    def swapped(recv):
        qs = [_pair_add(p, r, core, f"{name}_add{i}") for i, (p, r) in enumerate(zip(p4, recv))]
        hosts.add(_ChipsJob(qs), lambda outs: sink.update(zip(keys, outs)))

    hosts.add(_SwapJob(p4), swapped)


class _JobGroup:
    def __init__(self, jobs):
        self.jobs = jobs
        self.ins = [a for j in jobs for a in j.ins]
        self.out_shape = [s for j in jobs for s in j.out_shape]
        self.sems = [s for j in jobs for s in j.sems]

    def _each(self, phase, ins, outs, sems):
        i = o = s = 0
        for j in self.jobs:
            ni, no, ns = len(j.ins), len(j.out_shape), len(j.sems)
            getattr(j, phase)(ins[i:i + ni], outs[o:o + no], sems[s:s + ns])
            i, o, s = i + ni, o + no, s + ns

    def first(self, ins, outs, sems):
        self._each("first", ins, outs, sems)

    def mid(self, ins, outs, sems):
        self._each("mid", ins, outs, sems)

    def last(self, ins, outs, sems):
        self._each("last", ins, outs, sems)


def _hosted(hosts, duration, body, force=False, **kw):
    entries = hosts.take(duration, force) if hosts is not None else []
    out, jouts = _call(body, job=_JobGroup([e[0] for e in entries]) if entries else None, **kw)
    o = 0
    for job, done in entries:
        done(jouts[o:o + len(job.out_shape)])
        o += len(job.out_shape)
    return out


def _mm(a, b, *, ta=False, tb=False, out_dtype, name, blocks=None, hosts=None):
    a_parts = a.shape[0] if a.ndim == 3 else 0
    b_parts = b.shape[0] if b.ndim == 3 else 0
    assert not (a_parts and ta) and not (b_parts and tb)
    if a_parts:
        M, K = a.shape[1], a_parts * a.shape[2]
    else:
        M, K = (a.shape[1], a.shape[0]) if ta else a.shape
    N = b_parts * b.shape[2] if b_parts else (b.shape[0] if tb else b.shape[1])
    assert K == (b.shape[1] if (tb or b_parts) else b.shape[0]), (a.shape, b.shape, ta, tb)
    ns = N // blocks if blocks else N
    tm = _tile(M, MM_TILE, LANE)
    tk = _tile(a.shape[2] if a_parts else K, MM_TK, LANE)
    tn = _tile(math.gcd(ns, b.shape[2]) if b_parts else ns, MM_TILE, LANE)
    nk = K // tk
    per = ns // tn

    dims = (((0,) if ta else (1,), (1,) if tb else (0,)), ((), ()))

    def body(a_ref, b_ref, o_ref, *acc):
        part = lax.dot_general(a_ref[...], b_ref[...], dims, preferred_element_type=F32)
        if nk == 1:
            o_ref[...] = part.astype(o_ref.dtype)
            return
        acc_ref, k = acc[0], pl.program_id(2)

        @pl.when(k == 0)
        def _():
            acc_ref[...] = part

        @pl.when(k > 0)
        def _():
            acc_ref[...] += part

        @pl.when(k == nk - 1)
        def _():
            o_ref[...] = acc_ref[...].astype(o_ref.dtype)

    if a_parts:
        a_per = a.shape[2] // tk
        a_spec = pl.BlockSpec((None, tm, tk), lambda i, j, k: (k // a_per, i, k % a_per))
    else:
        a_spec = pl.BlockSpec((tk, tm), lambda i, j, k: (k, i)) if ta else pl.BlockSpec((tm, tk), lambda i, j, k: (i, k))
    if b_parts:
        b_per = b.shape[2] // tn
        b_spec = pl.BlockSpec((None, tk, tn), lambda i, j, k: (j // b_per, k, j % b_per))
    else:
        b_spec = pl.BlockSpec((tn, tk), lambda i, j, k: (j, k)) if tb else pl.BlockSpec((tk, tn), lambda i, j, k: (k, j))
    if blocks:
        out_shape = jax.ShapeDtypeStruct((blocks, M, ns), out_dtype)
        o_spec = pl.BlockSpec((None, tm, tn), lambda i, j, k: (j // per, i, j % per))
    else:
        out_shape = jax.ShapeDtypeStruct((M, N), out_dtype)
        o_spec = pl.BlockSpec((tm, tn), lambda i, j, k: (i, j))
    return _hosted(
        hosts, 2.0 * M * N * K / MXU_FLOP_PER_US, body, name=name, out_shape=out_shape, grid=(M // tm, N // tn, nk),
        in_specs=[a_spec, b_spec], out_specs=o_spec, args=[a, b],
        scratch_shapes=[pltpu.VMEM((tm, tn), F32)] if nk > 1 else [], semantics=("parallel", "parallel", "arbitrary"))


def _rms_fwd(xin, g, name, hosts=None):
    T, D = xin.shape
    tr = _tile(T, ROW_TILE, SUBLANE)

    def body(x_ref, g_ref, o_ref):
        xv = x_ref[...]
        r = lax.rsqrt(jnp.mean(xv * xv, axis=-1, keepdims=True) + NORM_EPS)
        o_ref[...] = (xv * r * g_ref[...]).astype(o_ref.dtype)

    row = pl.BlockSpec((tr, D), lambda i: (i, 0))
    return _hosted(
        hosts, T * D * 6.0 / HBM_BYTES_PER_US, body, force=True, name=name,
        out_shape=jax.ShapeDtypeStruct((T, D), MXU), grid=(T // tr,),
        in_specs=[row, pl.BlockSpec((1, D), lambda i: (0, 0))], out_specs=row, args=[xin, g], semantics=("parallel",))


def _rms_bwd(xin, dout, g, name, resid=None, out_dtype=F32):
    T, D = xin.shape
    tr = _tile(T, ROW_TILE, SUBLANE)

    def body(*refs):
        x_ref, d_ref, g_ref = refs[:3]
        dx_ref, dg_ref = refs[-2:]
        xv = x_ref[...].astype(F32)
        dv = d_ref[...].astype(F32)
        r = lax.rsqrt(jnp.mean(xv * xv, axis=-1, keepdims=True) + NORM_EPS)
        xh = xv * r
        dh = dv * g_ref[...]
        dxv = r * (dh - xh * jnp.mean(dh * xh, axis=-1, keepdims=True))
        if resid is not None:
            dxv = refs[3][...] + dxv
        dx_ref[...] = dxv.astype(dx_ref.dtype)

        @pl.when(pl.program_id(0) == 0)
        def _():
            dg_ref[...] = jnp.zeros_like(dg_ref)

        dg_ref[...] += jnp.sum(dv * xh, axis=0, keepdims=True)

    row = pl.BlockSpec((tr, D), lambda i: (i, 0))
    vec = pl.BlockSpec((1, D), lambda i: (0, 0))
    ins = [xin, dout, g] + ([resid] if resid is not None else [])
    return pl.pallas_call(
        body, name=name,
        out_shape=[jax.ShapeDtypeStruct((T, D), out_dtype), jax.ShapeDtypeStruct((1, D), F32)],
        grid=(T // tr,), in_specs=[row, row, vec] + ([row] if resid is not None else []),
        out_specs=[row, vec], compiler_params=_cp("arbitrary"),
    )(*ins)


def _post_fwd(y, g, resid, name, next_gain=None, target=None):
    T, D = y.shape
    tr = _tile(T, ROW_TILE, SUBLANE)

    def body(y_ref, g_ref, r_ref, e_ref, o1_ref, o2_ref):
        yv = y_ref[...]
        r = lax.rsqrt(jnp.mean(yv * yv, axis=-1, keepdims=True) + NORM_EPS)
        xv = r_ref[...] + yv * r * g_ref[...]
        if target is None:
            o1_ref[...] = xv
            r2 = lax.rsqrt(jnp.mean(xv * xv, axis=-1, keepdims=True) + NORM_EPS)
            o2_ref[...] = (xv * r2 * e_ref[...]).astype(o2_ref.dtype)
        else:
            e = xv - e_ref[...]
            o1_ref[...] = e * (1.0 / D)

            @pl.when(pl.program_id(0) == 0)
            def _():
                o2_ref[...] = jnp.zeros_like(o2_ref)

            rows = jnp.sum(e * e, axis=-1, keepdims=True) * (1.0 / D)
            o2_ref[...] += jnp.broadcast_to(0.5 * jnp.sum(rows, axis=0, keepdims=True), o2_ref.shape)

    row = pl.BlockSpec((tr, D), lambda i: (i, 0))
    vec = pl.BlockSpec((1, D), lambda i: (0, 0))
    if target is None:
        extra, e_spec = next_gain, vec
        out2, o2_spec = jax.ShapeDtypeStruct((T, D), MXU), row
    else:
        extra, e_spec = target, row
        out2, o2_spec = jax.ShapeDtypeStruct((1, LANE), F32), pl.BlockSpec((1, LANE), lambda i: (0, 0))
    return pl.pallas_call(
        body, name=name, out_shape=[jax.ShapeDtypeStruct((T, D), F32), out2],
        grid=(T // tr,), in_specs=[row, vec, row, e_spec], out_specs=[row, o2_spec],
        compiler_params=_cp("arbitrary" if target is not None else "parallel"),
    )(y, g, resid, extra)


def _mla_norms_fwd(P, gq, gkv, qr, kvr, name):
    T = P.shape[0]
    tr = _tile(T, ROW_TILE, SUBLANE)

    def body(cq_ref, ckv_ref, gq_ref, gkv_ref, oq_ref, okv_ref):
        for x_ref, g_ref, o_ref in ((cq_ref, gq_ref, oq_ref), (ckv_ref, gkv_ref, okv_ref)):
            xv = x_ref[...].astype(F32)
            r = lax.rsqrt(jnp.mean(xv * xv, axis=-1, keepdims=True) + NORM_EPS)
            o_ref[...] = (xv * r * g_ref[...]).astype(o_ref.dtype)

    return pl.pallas_call(
        body, name=name,
        out_shape=[jax.ShapeDtypeStruct((T, qr), MXU), jax.ShapeDtypeStruct((T, kvr), MXU)],
        grid=(T // tr,),
        in_specs=[pl.BlockSpec((tr, qr), lambda i: (i, 0)), pl.BlockSpec((tr, kvr), lambda i: (i, qr // kvr)),
                  pl.BlockSpec((1, qr), lambda i: (0, 0)), pl.BlockSpec((1, kvr), lambda i: (0, 0))],
        out_specs=[pl.BlockSpec((tr, qr), lambda i: (i, 0)), pl.BlockSpec((tr, kvr), lambda i: (i, 0))],
        compiler_params=_cp("parallel"),
    )(P, P, gq, gkv)


def _mla_norms_bwd(P, dcqn, dckvn_k, dckvn_v, dkr, gq, gkv, qr, kvr, pa, name):
    T = P.shape[0]
    tr = _tile(T, ROW_TILE, SUBLANE)
    c2 = qr + kvr

    def body(cq_ref, ckv_ref, dq_ref, dk_ref, dv_ref, dkr_ref, gq_ref, gkv_ref, dp_ref, dgq_ref, dgkv_ref):
        @pl.when(pl.program_id(0) == 0)
        def _():
            dgq_ref[...] = jnp.zeros_like(dgq_ref)
            dgkv_ref[...] = jnp.zeros_like(dgkv_ref)

        def one(x_ref, dv, g_ref, dg_ref):
            xv = x_ref[...].astype(F32)
            r = lax.rsqrt(jnp.mean(xv * xv, axis=-1, keepdims=True) + NORM_EPS)
            xh = xv * r
            dh = dv * g_ref[...]
            dg_ref[...] += jnp.sum(dv * xh, axis=0, keepdims=True)
            return r * (dh - xh * jnp.mean(dh * xh, axis=-1, keepdims=True))

        dp_ref[:, 0:qr] = one(cq_ref, dq_ref[...], gq_ref, dgq_ref).astype(dp_ref.dtype)
        dp_ref[:, qr:c2] = one(ckv_ref, dk_ref[...] + dv_ref[...], gkv_ref, dgkv_ref).astype(dp_ref.dtype)
        dp_ref[:, c2:c2 + LANE] = dkr_ref[...].astype(dp_ref.dtype)
        if pa > c2 + LANE:
            dp_ref[:, c2 + LANE:pa] = jnp.zeros((tr, pa - c2 - LANE), dp_ref.dtype)

    return pl.pallas_call(
        body, name=name,
        out_shape=[jax.ShapeDtypeStruct((T, pa), MXU), jax.ShapeDtypeStruct((1, qr), F32),
                   jax.ShapeDtypeStruct((1, kvr), F32)],
        grid=(T // tr,),
        in_specs=[pl.BlockSpec((tr, qr), lambda i: (i, 0)), pl.BlockSpec((tr, kvr), lambda i: (i, qr // kvr)),
                  pl.BlockSpec((tr, qr), lambda i: (i, 0)), pl.BlockSpec((tr, kvr), lambda i: (i, 0)),
                  pl.BlockSpec((tr, kvr), lambda i: (i, 0)), pl.BlockSpec((tr, LANE), lambda i: (i, 0)),
                  pl.BlockSpec((1, qr), lambda i: (0, 0)), pl.BlockSpec((1, kvr), lambda i: (0, 0))],
        out_specs=[pl.BlockSpec((tr, pa), lambda i: (i, 0)), pl.BlockSpec((1, qr), lambda i: (0, 0)),
                   pl.BlockSpec((1, kvr), lambda i: (0, 0))],
        compiler_params=_cp("arbitrary"),
    )(P, P, dcqn, dckvn_k, dckvn_v, dkr, gq, gkv)


def _rope_tables(pos_ref, invf_ref):
    ang = pos_ref[...].astype(F32) * invf_ref[...]
    lane = lax.broadcasted_iota(jnp.int32, ang.shape, 1)
    cos, sin = jnp.cos(ang), jnp.sin(ang)
    half = ROPE // 2
    c = jnp.where(lane < ROPE, cos, 0.0)
    s1 = jnp.where(lane < half, -sin, 0.0)
    s2 = jnp.where((lane >= half) & (lane < ROPE), sin, 0.0)
    return c, s1, s2


def _rope_fwd(qpad, kpad, v, P, pos, invf, kr_blk, H, name):
    T = qpad.shape[0]
    tr = _tile(T, ROW_TILE, SUBLANE)

    def body(q_ref, k_ref, v_ref, kr_ref, pos_ref, invf_ref, qo_ref, ko_ref, vo_ref):
        c, s1, s2 = _rope_tables(pos_ref, invf_ref)

        def rot(t):
            return t * c + pltpu.roll(t, LANE - ROPE // 2, 1) * s1 + pltpu.roll(t, ROPE // 2, 1) * s2

        kr = rot(kr_ref[...].astype(F32)).astype(ko_ref.dtype)
        ones = jnp.ones((tr, QPAD - VHEAD), vo_ref.dtype)
        for h in range(H):
            lo = h * QPAD
            qo_ref[:, lo:lo + NOPE] = q_ref[:, lo:lo + NOPE]
            qo_ref[:, lo + NOPE:lo + QPAD] = rot(q_ref[:, lo + NOPE:lo + QPAD].astype(F32)).astype(qo_ref.dtype)
            ko_ref[:, lo:lo + NOPE] = k_ref[:, lo:lo + NOPE]
            ko_ref[:, lo + NOPE:lo + QPAD] = kr
            vo_ref[:, lo:lo + VHEAD] = v_ref[:, h * VHEAD:(h + 1) * VHEAD]
            vo_ref[:, lo + VHEAD:lo + QPAD] = ones

    wide = pl.BlockSpec((tr, H * QPAD), lambda i: (i, 0))
    return pl.pallas_call(
        body, name=name,
        out_shape=[jax.ShapeDtypeStruct(qpad.shape, MXU)] * 3,
        grid=(T // tr,),
        in_specs=[wide, wide, pl.BlockSpec((tr, H * VHEAD), lambda i: (i, 0)),
                  pl.BlockSpec((tr, LANE), lambda i: (i, kr_blk)),
                  pl.BlockSpec((tr, 1), lambda i: (i, 0)), pl.BlockSpec((1, LANE), lambda i: (0, 0))],
        out_specs=[wide, wide, wide], compiler_params=_cp("parallel"),
    )(qpad, kpad, v, P, pos, invf)


def _rope_bwd(dqcat, dkcat, pos, invf, H, name):
    T = dqcat.shape[0]
    tr = _tile(T, ROW_TILE, SUBLANE)

    def body(dq_ref, dk_ref, pos_ref, invf_ref, dqo_ref, dkr_ref):
        c, s1, s2 = _rope_tables(pos_ref, invf_ref)

        def rot_t(v):
            return v * c + pltpu.roll(v * s1, ROPE // 2, 1) + pltpu.roll(v * s2, LANE - ROPE // 2, 1)

        acc = jnp.zeros((tr, LANE), F32)
        for h in range(H):
            lo = h * QPAD
            dqo_ref[:, lo:lo + NOPE] = dq_ref[:, lo:lo + NOPE]
            dqo_ref[:, lo + NOPE:lo + QPAD] = rot_t(dq_ref[:, lo + NOPE:lo + QPAD].astype(F32)).astype(dqo_ref.dtype)
            acc = acc + dk_ref[:, lo + NOPE:lo + QPAD].astype(F32)
        dkr_ref[...] = rot_t(acc)

    wide = pl.BlockSpec((tr, H * QPAD), lambda i: (i, 0))
    return pl.pallas_call(
        body, name=name,
        out_shape=[jax.ShapeDtypeStruct(dqcat.shape, MXU), jax.ShapeDtypeStruct((T, LANE), F32)],
        grid=(T // tr,),
        in_specs=[wide, wide, pl.BlockSpec((tr, 1), lambda i: (i, 0)), pl.BlockSpec((1, LANE), lambda i: (0, 0))],
        out_specs=[wide, pl.BlockSpec((tr, LANE), lambda i: (i, 0))], compiler_params=_cp("parallel"),
    )(dqcat, dkcat, pos, invf)


def _causal_mask(s):
    row = lax.broadcasted_iota(jnp.int32, s.shape, 0)
    col = lax.broadcasted_iota(jnp.int32, s.shape, 1)
    return jnp.where(col <= row, s, -1e30)


def _flash_fwd(qcat, kcat, v, P, z_blk, H, name, hosts=None):
    T = qcat.shape[0]
    tq = _tile(T, ATT_TILE, LANE)
    hp = 2 if H % 2 == 0 and z_blk % 2 == 0 else 1
    c2 = ATT_SCALE * LOG2E

    def body(q_ref, k_ref, v_ref, z_ref, o_ref, a_ref, lse_ref, m_s, acc_s):
        i = pl.program_id(1)
        m_s[...] = jnp.full_like(m_s, -1e30)
        acc_s[...] = jnp.zeros_like(acc_s)

        def step(j, masked):
            r0 = pl.multiple_of(j * tq, tq)
            for h in range(hp):
                q = q_ref[:, h * QPAD:(h + 1) * QPAD]
                kb = k_ref[pl.ds(r0, tq), h * QPAD:(h + 1) * QPAD]
                vb = v_ref[pl.ds(r0, tq), h * QPAD:(h + 1) * QPAD]
                s = lax.dot_general(q, kb, NT, preferred_element_type=F32)
                if masked:
                    s = _causal_mask(s)
                m_prev = m_s[h]
                m_new = jnp.maximum(m_prev, jnp.max(s, axis=1, keepdims=True))
                p = jnp.exp2((s - m_new) * c2).astype(MXU)
                alpha = jnp.exp2((m_prev - m_new) * c2)
                acc_s[h] = alpha * acc_s[h] + lax.dot_general(p, vb, NN, preferred_element_type=F32)
                m_s[h] = m_new

        def loop(j, carry):
            step(j, False)
            return carry

        lax.fori_loop(0, i, loop, 0)
        step(i, True)
        for h in range(hp):
            l = acc_s[h, :, VHEAD:QPAD]
            o = acc_s[h, :, 0:VHEAD] / l
            cols = slice(h * VHEAD, (h + 1) * VHEAD)
            o_ref[:, cols] = o.astype(o_ref.dtype)
            a_ref[:, cols] = (o * _silu(z_ref[:, cols].astype(F32))).astype(a_ref.dtype)
            lse_ref[h] = m_s[h] * ATT_SCALE + jnp.log(l)

    return _hosted(
        hosts, ATT_FWD_US * (T / 4096.0) ** 2 * (H / 32.0), body, name=name,
        out_shape=[jax.ShapeDtypeStruct((T, H * VHEAD), MXU), jax.ShapeDtypeStruct((T, H * VHEAD), MXU),
                   jax.ShapeDtypeStruct((H, T, LANE), F32)],
        grid=(H // hp, T // tq),
        in_specs=[pl.BlockSpec((tq, hp * QPAD), lambda h, i: (i, h)), pl.BlockSpec((T, hp * QPAD), lambda h, i: (0, h)),
                  pl.BlockSpec((T, hp * QPAD), lambda h, i: (0, h)),
                  pl.BlockSpec((tq, hp * VHEAD), lambda h, i: (i, z_blk // hp + h))],
        out_specs=[pl.BlockSpec((tq, hp * VHEAD), lambda h, i: (i, h)), pl.BlockSpec((tq, hp * VHEAD), lambda h, i: (i, h)),
                   pl.BlockSpec((hp, tq, LANE), lambda h, i: (h, i, 0))],
        scratch_shapes=[pltpu.VMEM((hp, tq, 1), F32), pltpu.VMEM((hp, tq, QPAD), F32)],
        args=[qcat, kcat, v, P], semantics=("parallel", "arbitrary"))


def _flash_bwd(qcat, kcat, v, do, lse, delta, H, name, hosts=None):
    T = qcat.shape[0]
    tq = _tile(T, ATT_TILE, LANE)
    nq = T // tq
    c2 = ATT_SCALE * LOG2E

    def body(q_ref, do_ref, lse_ref, dl_ref, k_ref, v_ref, dq_ref, dk_ref, dv_ref, dq_s, dk_s, dv_s):
        j = pl.program_id(1)
        kb, vb = k_ref[...], v_ref[...]
        dk_s[...] = jnp.zeros_like(dk_s)
        dv_s[...] = jnp.zeros_like(dv_s)

        @pl.when(j == 0)
        def _():
            dq_s[...] = jnp.zeros_like(dq_s)

        def step(i, masked):
            rows = pl.ds(pl.multiple_of(i * tq, tq), tq)
            qb = q_ref[rows, :]
            dob = do_ref[rows, :]
            s = lax.dot_general(qb, kb, NT, preferred_element_type=F32)
            if masked:
                s = _causal_mask(s)
            p = jnp.exp2(s * c2 - lse_ref[rows, 0:1] * LOG2E)
            dv_s[...] += lax.dot_general(p.astype(dob.dtype), dob, TN, preferred_element_type=F32)
            dp = lax.dot_general(dob, vb, NT, preferred_element_type=F32)
            ds = (p * (dp - dl_ref[rows, 0:1]) * ATT_SCALE).astype(qb.dtype)
            dk_s[...] += lax.dot_general(ds, qb, TN, preferred_element_type=F32)
            dq_s[rows, :] += lax.dot_general(ds, kb, NN, preferred_element_type=F32)

        def loop(i, carry):
            step(i, False)
            return carry

        step(j, True)
        lax.fori_loop(j + 1, nq, loop, 0)
        dk_ref[...] = dk_s[...].astype(dk_ref.dtype)
        dv_ref[...] = dv_s[...].astype(dv_ref.dtype)

        @pl.when(j == nq - 1)
        def _():
            dq_ref[...] = dq_s[...].astype(dq_ref.dtype)

    return _hosted(
        hosts, ATT_BWD_US * (T / 4096.0) ** 2 * (H / 32.0), body, name=name,
        out_shape=[jax.ShapeDtypeStruct(qcat.shape, MXU), jax.ShapeDtypeStruct(kcat.shape, MXU),
                   jax.ShapeDtypeStruct(v.shape, MXU)],
        grid=(H, nq),
        in_specs=[pl.BlockSpec((T, QPAD), lambda h, j: (0, h)), pl.BlockSpec((T, VHEAD), lambda h, j: (0, h)),
                  pl.BlockSpec((None, T, LANE), lambda h, j: (h, 0, 0)), pl.BlockSpec((None, T, LANE), lambda h, j: (h, 0, 0)),
                  pl.BlockSpec((tq, QPAD), lambda h, j: (j, h)), pl.BlockSpec((tq, VHEAD), lambda h, j: (j, h))],
        out_specs=[pl.BlockSpec((T, QPAD), lambda h, j: (0, h)), pl.BlockSpec((tq, QPAD), lambda h, j: (j, h)),
                   pl.BlockSpec((tq, VHEAD), lambda h, j: (j, h))],
        scratch_shapes=[pltpu.VMEM((T, QPAD), F32), pltpu.VMEM((tq, QPAD), F32), pltpu.VMEM((tq, VHEAD), F32)],
        args=[qcat, do, lse, delta, kcat, v], semantics=("parallel", "arbitrary"))


def _mla_gate_bwd(da, o, P, pa, H, name):
    T, HV = da.shape
    tr = _tile(T, ROW_TILE, SUBLANE)
    cw = _tile(math.gcd(pa, HV), 512, LANE)
    hb = cw // VHEAD

    def body(da_ref, o_ref, z_ref, do_ref, dz_ref, dl_ref):
        dav, ov, zv = da_ref[...].astype(F32), o_ref[...].astype(F32), z_ref[...].astype(F32)
        dov = dav * _silu(zv)
        do_ref[...] = dov.astype(do_ref.dtype)
        dz_ref[...] = (dav * ov * _silu_grad(zv)).astype(dz_ref.dtype)
        prod = dov * ov
        for h in range(hb):
            dl_ref[h] = jnp.broadcast_to(jnp.sum(prod[:, h * VHEAD:(h + 1) * VHEAD], axis=1, keepdims=True), (tr, LANE))

    blk = pl.BlockSpec((tr, cw), lambda i, j: (i, j))
    shifted = pl.BlockSpec((tr, cw), lambda i, j: (i, pa // cw + j))
    return pl.pallas_call(
        body, name=name,
        out_shape=[jax.ShapeDtypeStruct((T, HV), MXU), jax.ShapeDtypeStruct((T, HV), MXU),
                   jax.ShapeDtypeStruct((H, T, LANE), F32)],
        grid=(T // tr, HV // cw),
        in_specs=[blk, blk, shifted],
        out_specs=[blk, blk, pl.BlockSpec((hb, tr, LANE), lambda i, j: (j, i, 0))],
        compiler_params=_cp("parallel", "parallel"),
    )(da, o, P)


def _bank_rows(width):
    return min(SUBLANE, width), -(-(width - 1) // SUBLANE) * SUBLANE


def _bank_fill(bank_ref, val, width, T, causal):
    nr, hp = _bank_rows(width)
    rows = lax.broadcasted_iota(jnp.int32, val.shape, 0)
    zero = jnp.zeros((hp,) + val.shape[1:], F32)
    for r in range(nr):
        if causal:
            bank_ref[r, 0:hp, :] = zero
            bank_ref[r, hp:hp + T, :] = val if r == 0 else jnp.where(rows >= r, pltpu.roll(val, r, 0), 0.0)
        else:
            bank_ref[r, T:T + hp, :] = zero
            bank_ref[r, 0:T, :] = val if r == 0 else jnp.where(rows < T - r, pltpu.roll(val, T - r, 0), 0.0)


def _bank_tap(bank_ref, s, t0, tc, width, causal):
    _, hp = _bank_rows(width)
    q, r = divmod(s, SUBLANE)
    off = hp - SUBLANE * q if causal else SUBLANE * q
    return bank_ref[r, pl.ds(pl.multiple_of(t0 + off, SUBLANE), tc), :]


def _conv_chunk(bank_ref, w_ref, t0, tc, width):
    acc = None
    for s in range(width):
        k = width - 1 - s
        term = w_ref[k:k + 1, :] * _bank_tap(bank_ref, s, t0, tc, width, True)
        acc = term if acc is None else acc + term
    return acc


def _conv_bwd_chunk(bank_ref, w_ref, xin, dw_ref, t0, tc, width):
    acc = None
    for s in range(width):
        k = width - 1 - s
        tap = _bank_tap(bank_ref, s, t0, tc, width, False)
        term = w_ref[k:k + 1, :] * tap
        acc = term if acc is None else acc + term
        dw_ref[k] += jnp.sum((tap * xin).reshape(tc // SUBLANE, SUBLANE, xin.shape[1]), axis=0)
    return acc


def _full_t(T, cb, col):
    return pl.BlockSpec((T, cb), lambda j, col=col: (0, col + j))


def _sc_fwd(P, w, D, name):
    T = P.shape[0]
    cb = _tile(D, CONV_CB, LANE)
    nb = D // cb
    width = w.shape[0]
    tc = _tile(T, CONV_CHUNK, SUBLANE)
    nr, hp = _bank_rows(width)

    def body(b_ref, c_ref, u_ref, z_ref, w_ref, a_ref, bank):
        _bank_fill(bank, c_ref[...].astype(F32) * u_ref[...].astype(F32), width, T, True)

        def chunk(ci, carry):
            t0 = pl.multiple_of(ci * tc, tc)
            rows = pl.ds(t0, tc)
            v = _conv_chunk(bank, w_ref, t0, tc, width)
            a_ref[rows, :] = (b_ref[rows, :].astype(F32) * v * _silu(z_ref[rows, :].astype(F32))).astype(a_ref.dtype)
            return carry

        lax.fori_loop(0, T // tc, chunk, 0)

    return pl.pallas_call(
        body, name=name, out_shape=jax.ShapeDtypeStruct((T, D), MXU), grid=(nb,),
        in_specs=[_full_t(T, cb, 0), _full_t(T, cb, nb), _full_t(T, cb, 2 * nb), _full_t(T, cb, 3 * nb),
                  pl.BlockSpec((width, cb), lambda j: (0, j))],
        out_specs=_full_t(T, cb, 0),
        scratch_shapes=[pltpu.VMEM((nr, T + hp, cb), F32)],
        compiler_params=_cp("parallel"),
    )(P, P, P, P, w)


def _sc_bwd(P, da, w, D, name):
    T = P.shape[0]
    cb = _tile(D, CONV_CB, LANE)
    nb = D // cb
    width = w.shape[0]
    tc = _tile(T, CONV_CHUNK, SUBLANE)
    nr, hp = _bank_rows(width)

    def body(b_ref, c_ref, u_ref, z_ref, da_ref, w_ref, dp_ref, dw_ref, bank, dv_s, dw_s):
        _bank_fill(bank, c_ref[...].astype(F32) * u_ref[...].astype(F32), width, T, True)

        def first(ci, carry):
            t0 = pl.multiple_of(ci * tc, tc)
            rows = pl.ds(t0, tc)
            v = _conv_chunk(bank, w_ref, t0, tc, width)
            bv, zv, dav = b_ref[rows, :].astype(F32), z_ref[rows, :].astype(F32), da_ref[rows, :].astype(F32)
            dyb = dav * _silu(zv)
            dp_ref[3, rows, :] = (dav * bv * v * _silu_grad(zv)).astype(dp_ref.dtype)
            dp_ref[0, rows, :] = (dyb * v).astype(dp_ref.dtype)
            dv_s[rows, :] = dyb * bv
            return carry

        lax.fori_loop(0, T // tc, first, 0)
        _bank_fill(bank, dv_s[...], width, T, False)
        dw_s[...] = jnp.zeros_like(dw_s)

        def second(ci, carry):
            t0 = pl.multiple_of(ci * tc, tc)
            rows = pl.ds(t0, tc)
            cv, uv = c_ref[rows, :].astype(F32), u_ref[rows, :].astype(F32)
            dcu = _conv_bwd_chunk(bank, w_ref, cv * uv, dw_s, t0, tc, width)
            dp_ref[1, rows, :] = (dcu * uv).astype(dp_ref.dtype)
            dp_ref[2, rows, :] = (dcu * cv).astype(dp_ref.dtype)
            return carry

        lax.fori_loop(0, T // tc, second, 0)
        dw_ref[...] = jnp.sum(dw_s[...], axis=1)

    return pl.pallas_call(
        body, name=name,
        out_shape=[jax.ShapeDtypeStruct((4, T, D), MXU), jax.ShapeDtypeStruct((width, D), F32)],
        grid=(nb,),
        in_specs=[_full_t(T, cb, 0), _full_t(T, cb, nb), _full_t(T, cb, 2 * nb), _full_t(T, cb, 3 * nb),
                  _full_t(T, cb, 0), pl.BlockSpec((width, cb), lambda j: (0, j))],
        out_specs=[pl.BlockSpec((4, T, cb), lambda j: (0, 0, j)), pl.BlockSpec((width, cb), lambda j: (0, j))],
        scratch_shapes=[pltpu.VMEM((nr, T + hp, cb), F32), pltpu.VMEM((T, cb), F32),
                        pltpu.VMEM((width, SUBLANE, cb), F32)],
        compiler_params=_cp("parallel"),
    )(P, P, P, P, da, w)


def _cf_conv_fwd(P, w, bias, D, name):
    T = P.shape[0]
    cb = _tile(D, CONV_CB, LANE)
    nb = D // cb
    width = w.shape[0]
    tc = _tile(T, CONV_CHUNK, SUBLANE)
    nr, hp = _bank_rows(width)

    def body(a_ref, g_ref, w_ref, b_ref, y_ref, bank):
        _bank_fill(bank, a_ref[...].astype(F32) * _sig(g_ref[...].astype(F32)), width, T, True)

        def chunk(ci, carry):
            t0 = pl.multiple_of(ci * tc, tc)
            y_ref[pl.ds(t0, tc), :] = (_conv_chunk(bank, w_ref, t0, tc, width) + b_ref[...]).astype(y_ref.dtype)
            return carry

        lax.fori_loop(0, T // tc, chunk, 0)

    return pl.pallas_call(
        body, name=name, out_shape=jax.ShapeDtypeStruct((T, D), F32), grid=(nb,),
        in_specs=[_full_t(T, cb, 0), _full_t(T, cb, nb), pl.BlockSpec((width, cb), lambda j: (0, j)),
                  pl.BlockSpec((1, cb), lambda j: (0, j))],
        out_specs=_full_t(T, cb, 0),
        scratch_shapes=[pltpu.VMEM((nr, T + hp, cb), F32)],
        compiler_params=_cp("parallel"),
    )(P, P, w, bias)


def _cf_conv_bwd(P, dyc, w, D, name):
    T = P.shape[0]
    cb = _tile(D, CONV_CB, LANE)
    nb = D // cb
    width = w.shape[0]
    tc = _tile(T, CONV_CHUNK, SUBLANE)
    nr, hp = _bank_rows(width)

    def body(a_ref, g_ref, dy_ref, w_ref, da_ref, dg_ref, dw_ref, db_ref, bank, dw_s):
        dyv = dy_ref[...]
        db_ref[...] = jnp.sum(dyv, axis=0, keepdims=True)
        _bank_fill(bank, dyv, width, T, False)
        dw_s[...] = jnp.zeros_like(dw_s)

        def chunk(ci, carry):
            t0 = pl.multiple_of(ci * tc, tc)
            rows = pl.ds(t0, tc)
            av, sg = a_ref[rows, :].astype(F32), _sig(g_ref[rows, :].astype(F32))
            dyg = _conv_bwd_chunk(bank, w_ref, av * sg, dw_s, t0, tc, width)
            da_ref[rows, :] = (dyg * sg).astype(da_ref.dtype)
            dg_ref[rows, :] = (dyg * av * sg * (1.0 - sg)).astype(dg_ref.dtype)
            return carry

        lax.fori_loop(0, T // tc, chunk, 0)
        dw_ref[...] = jnp.sum(dw_s[...], axis=1)

    return pl.pallas_call(
        body, name=name,
        out_shape=[jax.ShapeDtypeStruct((T, D), MXU), jax.ShapeDtypeStruct((T, D), MXU),
                   jax.ShapeDtypeStruct((width, D), F32), jax.ShapeDtypeStruct((1, D), F32)],
        grid=(nb,),
        in_specs=[_full_t(T, cb, 0), _full_t(T, cb, nb), _full_t(T, cb, 0),
                  pl.BlockSpec((width, cb), lambda j: (0, j))],
        out_specs=[_full_t(T, cb, 0), _full_t(T, cb, 0), pl.BlockSpec((width, cb), lambda j: (0, j)),
                   pl.BlockSpec((1, cb), lambda j: (0, j))],
        scratch_shapes=[pltpu.VMEM((nr, T + hp, cb), F32), pltpu.VMEM((width, SUBLANE, cb), F32)],
        compiler_params=_cp("parallel"),
    )(P, P, dyc, w)


def _layer_norm_stats(v):
    mu = jnp.mean(v, axis=-1, keepdims=True)
    cen = v - mu
    rstd = lax.rsqrt(jnp.mean(cen * cen, axis=-1, keepdims=True) + LN_EPS)
    return cen * rstd, rstd


def _layer_norm_bwd(dxh, xh, rstd):
    return rstd * (dxh - jnp.mean(dxh, axis=-1, keepdims=True) - xh * jnp.mean(dxh * xh, axis=-1, keepdims=True))


def _cf_act_fwd(yc, P, lg, lb, D, name):
    T = yc.shape[0]
    tr = _tile(T, ROW_TILE, SUBLANE)

    def body(y_ref, z_ref, g_ref, b_ref, o_ref):
        xh, _ = _layer_norm_stats(y_ref[...])
        yl = xh * g_ref[...] + b_ref[...]
        o_ref[...] = (_silu(yl) * _silu(z_ref[...].astype(F32))).astype(o_ref.dtype)

    row = pl.BlockSpec((tr, D), lambda i: (i, 0))
    vec = pl.BlockSpec((1, D), lambda i: (0, 0))
    return pl.pallas_call(
        body, name=name, out_shape=jax.ShapeDtypeStruct((T, D), MXU), grid=(T // tr,),
        in_specs=[row, pl.BlockSpec((tr, D), lambda i: (i, 2)), vec, vec], out_specs=row,
        compiler_params=_cp("parallel"),
    )(yc, P, lg, lb)


def _cf_act_bwd(yc, P, da, lg, lb, D, name):
    T = yc.shape[0]
    tr = _tile(T, ROW_TILE, SUBLANE)

    def body(y_ref, z_ref, da_ref, g_ref, b_ref, dy_ref, dz_ref, dg_ref, db_ref):
        @pl.when(pl.program_id(0) == 0)
        def _():
            dg_ref[...] = jnp.zeros_like(dg_ref)
            db_ref[...] = jnp.zeros_like(db_ref)

        xh, rstd = _layer_norm_stats(y_ref[...])
        yl = xh * g_ref[...] + b_ref[...]
        zv, dav = z_ref[...].astype(F32), da_ref[...].astype(F32)
        dz_ref[...] = (dav * _silu(yl) * _silu_grad(zv)).astype(dz_ref.dtype)
        dyl = dav * _silu(zv) * _silu_grad(yl)
        dg_ref[...] += jnp.sum(dyl * xh, axis=0, keepdims=True)
        db_ref[...] += jnp.sum(dyl, axis=0, keepdims=True)
        dy_ref[...] = _layer_norm_bwd(dyl * g_ref[...], xh, rstd)

    row = pl.BlockSpec((tr, D), lambda i: (i, 0))
    zcol = pl.BlockSpec((tr, D), lambda i: (i, 2))
    vec = pl.BlockSpec((1, D), lambda i: (0, 0))
    return pl.pallas_call(
        body, name=name,
        out_shape=[jax.ShapeDtypeStruct((T, D), F32), jax.ShapeDtypeStruct((T, D), MXU),
                   jax.ShapeDtypeStruct((1, D), F32), jax.ShapeDtypeStruct((1, D), F32)],
        grid=(T // tr,), in_specs=[row, zcol, row, vec, vec], out_specs=[row, row, vec, vec],
        compiler_params=_cp("arbitrary"),
    )(yc, P, da, lg, lb)


def _tril(w):
    row = lax.broadcasted_iota(jnp.int32, w.shape, 0)
    col = lax.broadcasted_iota(jnp.int32, w.shape, 1)
    return jnp.where(col <= row, w, 0.0)


def _gm_fwd(P, lg, lb, ws, bst, D, name):
    T = P.shape[0]
    G, ch, _ = ws.shape
    gw = D // G

    def body(p_ref, g_ref, b_ref, ws_ref, bs_ref, a_ref):
        uv, vv, zv = (p_ref[:, k * D:(k + 1) * D].astype(F32) for k in range(3))
        xh, _ = _layer_norm_stats(_gelu(vv))
        vn = (xh * g_ref[...] + b_ref[...]).astype(MXU)
        gate = _gelu(uv) * _silu(zv)
        for g in range(G):
            cols = slice(g * gw, (g + 1) * gw)
            s = lax.dot_general(_tril(ws_ref[g]).astype(MXU), vn[:, cols], NN, preferred_element_type=F32)
            a_ref[:, cols] = (gate[:, cols] * (s + bs_ref[:, g:g + 1])).astype(a_ref.dtype)

    vec = pl.BlockSpec((1, D), lambda i: (0, 0))
    return pl.pallas_call(
        body, name=name, out_shape=jax.ShapeDtypeStruct((T, D), MXU), grid=(T // ch,),
        in_specs=[pl.BlockSpec((ch, 3 * D), lambda i: (i, 0)), vec, vec,
                  pl.BlockSpec((G, ch, ch), lambda i: (0, 0, 0)), pl.BlockSpec((ch, G), lambda i: (0, 0))],
        out_specs=pl.BlockSpec((ch, D), lambda i: (i, 0)), compiler_params=_cp("parallel"),
    )(P, lg, lb, ws, bst)


def _gm_bwd(P, da, lg, lb, ws, bst, D, name):
    T = P.shape[0]
    G, ch, _ = ws.shape
    gw = D // G

    def body(p_ref, da_ref, g_ref, b_ref, ws_ref, bs_ref, dp_ref, dg_ref, db_ref, dws_ref, dbs_ref, dvn_s):
        @pl.when(pl.program_id(0) == 0)
        def _():
            dg_ref[...] = jnp.zeros_like(dg_ref)
            db_ref[...] = jnp.zeros_like(db_ref)
            dws_ref[...] = jnp.zeros_like(dws_ref)
            dbs_ref[...] = jnp.zeros_like(dbs_ref)

        uv, vv, zv = (p_ref[:, k * D:(k + 1) * D].astype(F32) for k in range(3))
        dav = da_ref[...].astype(F32)
        xh, rstd = _layer_norm_stats(_gelu(vv))
        vn = (xh * g_ref[...] + b_ref[...]).astype(MXU)
        ug, sz = _gelu(uv), _silu(zv)
        ds_all = dav * sz * ug
        for g in range(G):
            cols = slice(g * gw, (g + 1) * gw)
            wm = _tril(ws_ref[g]).astype(MXU)
            s = lax.dot_general(wm, vn[:, cols], NN, preferred_element_type=F32) + bs_ref[:, g:g + 1]
            dp_ref[:, g * gw:(g + 1) * gw] = (dav[:, cols] * sz[:, cols] * s * _gelu_grad(uv[:, cols])).astype(dp_ref.dtype)
            dp_ref[:, 2 * D + g * gw:2 * D + (g + 1) * gw] = (
                dav[:, cols] * ug[:, cols] * s * _silu_grad(zv[:, cols])).astype(dp_ref.dtype)
            ds = ds_all[:, cols]
            dsb = ds.astype(MXU)
            dvn_s[:, cols] = lax.dot_general(wm, dsb, TN, preferred_element_type=F32)
            dws_ref[g] += _tril(lax.dot_general(dsb, vn[:, cols], NT, preferred_element_type=F32))
            dbs_ref[g] += jnp.broadcast_to(jnp.sum(ds, axis=1, keepdims=True), (ch, LANE))
        dvn = dvn_s[...]
        dg_ref[...] += jnp.sum(dvn * xh, axis=0, keepdims=True)
        db_ref[...] += jnp.sum(dvn, axis=0, keepdims=True)
        dp_ref[:, D:2 * D] = (_layer_norm_bwd(dvn * g_ref[...], xh, rstd) * _gelu_grad(vv)).astype(dp_ref.dtype)

    vec = pl.BlockSpec((1, D), lambda i: (0, 0))
    return pl.pallas_call(
        body, name=name,
        out_shape=[jax.ShapeDtypeStruct((T, 3 * D), MXU), jax.ShapeDtypeStruct((1, D), F32), jax.ShapeDtypeStruct((1, D), F32),
                   jax.ShapeDtypeStruct((G, ch, ch), F32), jax.ShapeDtypeStruct((G, ch, LANE), F32)],
        grid=(T // ch,),
        in_specs=[pl.BlockSpec((ch, 3 * D), lambda i: (i, 0)), pl.BlockSpec((ch, D), lambda i: (i, 0)), vec, vec,
                  pl.BlockSpec((G, ch, ch), lambda i: (0, 0, 0)), pl.BlockSpec((ch, G), lambda i: (0, 0))],
        out_specs=[pl.BlockSpec((ch, 3 * D), lambda i: (i, 0)), vec, vec,
                   pl.BlockSpec((G, ch, ch), lambda i: (0, 0, 0)), pl.BlockSpec((G, ch, LANE), lambda i: (0, 0, 0))],
        scratch_shapes=[pltpu.VMEM((ch, D), F32)],
        compiler_params=_cp("arbitrary"),
    )(P, da, lg, lb, ws, bst)


def _adam_math(w, g, m, v):
    m = ADAM_B1 * m + (1.0 - ADAM_B1) * g
    v = ADAM_B2 * v + (1.0 - ADAM_B2) * (g * g)
    m_hat = m / (1.0 - ADAM_B1 ** ADAM_STEP)
    v_hat = v / (1.0 - ADAM_B2 ** ADAM_STEP)
    return -ADAM_LR * (m_hat / (jnp.sqrt(v_hat) + ADAM_EPS) + ADAM_WD * w), m, v


def _adam(w, m, v, gparts, name, hosts=None):
    R, C = w.shape
    n = gparts.shape[0]
    tr = _tile(R, max(SUBLANE, (1 << 18) // C), 16 if gparts.dtype != F32 else SUBLANE)

    def body(w_ref, m_ref, v_ref, gp_ref, g_ref, d_ref, mo_ref, vo_ref):
        g = gp_ref[0].astype(F32)
        for k in range(1, n):
            g = g + gp_ref[k].astype(F32)
        g_ref[...] = g
        d_ref[...], mo_ref[...], vo_ref[...] = _adam_math(w_ref[...], g, m_ref[...], v_ref[...])

    blk = pl.BlockSpec((tr, C), lambda i: (i, 0))
    return _hosted(
        hosts, R * C * 36.0 / HBM_BYTES_PER_US, body, force=True, name=name,
        out_shape=[jax.ShapeDtypeStruct((R, C), F32)] * 4, grid=(R // tr,),
        in_specs=[blk, blk, blk, pl.BlockSpec((n, tr, C), lambda i: (0, i, 0))], out_specs=[blk] * 4,
        args=[w, m, v, gparts], semantics=("parallel",))


def _sum_blocks(parts, name):
    n, R, C = parts.shape
    tr = _tile(R, 512, SUBLANE)

    def body(p_ref, o_ref):
        acc = p_ref[0]
        for k in range(1, n):
            acc = acc + p_ref[k]
        o_ref[...] = acc

    return pl.pallas_call(
        body, name=name, out_shape=jax.ShapeDtypeStruct((R, C), F32), grid=(R // tr,),
        in_specs=[pl.BlockSpec((n, tr, C), lambda i: (0, i, 0))], out_specs=pl.BlockSpec((tr, C), lambda i: (i, 0)),
        compiler_params=_cp("parallel"),
    )(parts)


def _pack_rows(shape):
    return -(-int(np.prod(shape)) // (SUBLANE * LANE)) * SUBLANE


def _pack(arrs):
    parts = []
    for a in arrs:
        flat = a.reshape(-1).astype(F32)
        rows = _pack_rows(a.shape)
        parts.append(jnp.pad(flat, (0, rows * LANE - flat.shape[0])).reshape(rows, LANE))
    return jnp.concatenate(parts, axis=0)


def _unpack(buf, shapes):
    out, off = [], 0
    for s in shapes:
        rows, n = _pack_rows(s), int(np.prod(s))
        out.append(buf[off:off + rows].reshape(-1)[:n].reshape(s))
        off += rows
    return out


def _cols_full(g):
    return jnp.transpose(g, (1, 0, 2)).reshape(g.shape[1], N_DEV * g.shape[2])


def _cols_blocks(w):
    R, N = w.shape
    return jnp.transpose(w.reshape(R, N_DEV, N // N_DEV), (1, 0, 2)).astype(WIRE)


def _rows_blocks(w):
    return w.reshape(N_DEV, w.shape[0] // N_DEV, w.shape[1]).astype(WIRE)


def kernel(x, positions, norm_pre, norm_post, w_in_mla, mla_q_norm, w_uq, mla_kv_norm, w_ukv, w_out_mla, w_in_sc, sc_conv, w_out_sc, w_in_gm, gm_ln_g, gm_ln_b, gm_w_s, gm_b_s, w_out_gm, w_in_cf, cf_dw, cf_dw_b, cf_ln_g, cf_ln_b, w_out_cf, loss_target, m_norm_pre, m_norm_post, m_w_in_mla, m_mla_q_norm, m_w_uq, m_mla_kv_norm, m_w_ukv, m_w_out_mla, m_w_in_sc, m_sc_conv, m_w_out_sc, m_w_in_gm, m_gm_ln_g, m_gm_ln_b, m_gm_w_s, m_gm_b_s, m_w_out_gm, m_w_in_cf, m_cf_dw, m_cf_dw_b, m_cf_ln_g, m_cf_ln_b, m_w_out_cf, v_norm_pre, v_norm_post, v_w_in_mla, v_mla_q_norm, v_w_uq, v_mla_kv_norm, v_w_ukv, v_w_out_mla, v_w_in_sc, v_sc_conv, v_w_out_sc, v_w_in_gm, v_gm_ln_g, v_gm_ln_b, v_gm_w_s, v_gm_b_s, v_w_out_gm, v_w_in_cf, v_cf_dw, v_cf_dw_b, v_cf_ln_g, v_cf_ln_b, v_w_out_cf):
    names = ['norm_pre', 'norm_post', 'w_in_mla', 'mla_q_norm', 'w_uq', 'mla_kv_norm', 'w_ukv', 'w_out_mla', 'w_in_sc',
             'sc_conv', 'w_out_sc', 'w_in_gm', 'gm_ln_g', 'gm_ln_b', 'gm_w_s', 'gm_b_s', 'w_out_gm', 'w_in_cf', 'cf_dw',
             'cf_dw_b', 'cf_ln_g', 'cf_ln_b', 'w_out_cf']
    W = dict(zip(names, (norm_pre, norm_post, w_in_mla, mla_q_norm, w_uq, mla_kv_norm, w_ukv, w_out_mla, w_in_sc, sc_conv,
                         w_out_sc, w_in_gm, gm_ln_g, gm_ln_b, gm_w_s, gm_b_s, w_out_gm, w_in_cf, cf_dw, cf_dw_b, cf_ln_g,
                         cf_ln_b, w_out_cf)))
    Mo = dict(zip(names, (m_norm_pre, m_norm_post, m_w_in_mla, m_mla_q_norm, m_w_uq, m_mla_kv_norm, m_w_ukv, m_w_out_mla,
                          m_w_in_sc, m_sc_conv, m_w_out_sc, m_w_in_gm, m_gm_ln_g, m_gm_ln_b, m_gm_w_s, m_gm_b_s, m_w_out_gm,
                          m_w_in_cf, m_cf_dw, m_cf_dw_b, m_cf_ln_g, m_cf_ln_b, m_w_out_cf)))
    Vo = dict(zip(names, (v_norm_pre, v_norm_post, v_w_in_mla, v_mla_q_norm, v_w_uq, v_mla_kv_norm, v_w_ukv, v_w_out_mla,
                          v_w_in_sc, v_sc_conv, v_w_out_sc, v_w_in_gm, v_gm_ln_g, v_gm_ln_b, v_gm_w_s, v_gm_b_s, v_w_out_gm,
                          v_w_in_cf, v_cf_dw, v_cf_dw_b, v_cf_ln_g, v_cf_ln_b, v_w_out_cf)))
    big = ['w_in_mla', 'w_uq', 'w_ukv', 'w_out_mla', 'w_in_sc', 'w_out_sc', 'w_in_gm', 'w_out_gm', 'w_in_cf', 'w_out_cf']
    row_sharded = {'w_out_mla', 'w_out_sc', 'w_out_gm', 'w_out_cf'}
    chan = ['sc_conv', 'gm_ln_g', 'gm_ln_b', 'cf_dw', 'cf_dw_b', 'cf_ln_g', 'cf_ln_b']
    repl = ['norm_pre', 'norm_post', 'mla_q_norm', 'mla_kv_norm', 'gm_w_s', 'gm_b_s']

    T, D = x.shape[1], x.shape[2]
    xs, tgt = x[0], loss_target[0]
    pos = positions.reshape(T, 1)
    qr, kvr = mla_q_norm.shape[-1], mla_kv_norm.shape[-1]
    H = w_uq.shape[-1] * N_DEV // (NOPE + ROPE)
    HV = H * VHEAD
    c3 = qr + kvr + ROPE
    pa = -(-c3 // 512) * 512
    assert qr % kvr == 0 and qr % LANE == 0 and kvr % LANE == 0 and pa >= qr + kvr + LANE
    mx, my, mc = _mesh_pos()
    me = 4 * mx + 2 * my + mc
    core = mc.astype(jnp.int32).reshape(1)
    cs = D // N_DEV

    def wire(nm):
        return W[nm][0].astype(WIRE)

    hosts = _Hosts()
    G = {}

    def gather(nm, name=None):
        shard = wire(nm)
        in_place = nm not in row_sharded and shard.shape[1] % LANE == 0
        job = _GatherJob([shard], cols=in_place)
        if name is None:
            hosts.add(job, lambda outs: G.update({nm: outs[0]}))
        else:
            G[nm] = _run_job(job, name)[0]

    def gathered_rows(nm):
        if nm not in G:
            hosts.flush("ag_late")
        return G[nm].reshape(G[nm].shape[0] * G[nm].shape[1], G[nm].shape[2])

    def gathered_cols(nm):
        if nm not in G:
            hosts.flush("ag_late")
        return G[nm] if G[nm].ndim == 2 else _cols_full(G[nm])

    gather('w_in_mla')
    h = _rms_fwd(xs, norm_pre[0:1], "pre0", hosts=hosts)
    chan_rows = [W[nm][0].reshape(-1, cs) for nm in chan]
    chan_cnt = [r.shape[0] for r in chan_rows]
    chan_local = jnp.concatenate(chan_rows, axis=0)
    chan_pad = -chan_local.shape[0] % SUBLANE
    chan_full = _cols_full(_run_job(_GatherJob([jnp.pad(chan_local, ((0, chan_pad), (0, 0)))]), "ag_small")[0])
    offs = np.cumsum([0] + chan_cnt)
    CH = {nm: chan_full[offs[i]:offs[i + 1]] for i, nm in enumerate(chan)}

    w_in_full = gathered_cols('w_in_mla')
    w_cat = jnp.concatenate([w_in_full[:, :c3], jnp.zeros((D, pa - c3), WIRE), w_in_full[:, c3:]], axis=1)
    in_names = ['w_in_mla', 'w_in_sc', 'w_in_gm', 'w_in_cf']
    out_names = ['w_out_mla', 'w_out_sc', 'w_out_gm', 'w_out_cf']
    WIN, WOUT = [w_cat], []

    half = ROPE // 2
    invf_np = np.zeros((1, LANE), np.float32)
    invf_np[0, :ROPE] = np.tile(np.float32(ROPE_THETA) ** (-np.arange(half, dtype=np.float32) / np.float32(half)), 2)
    invf = jnp.asarray(invf_np)
    gm_ws = gm_w_s[0]
    gm_bst = jnp.transpose(gm_b_s[0])

    xin, hs, Ps, acts, ys, keep = [xs], [], [], [], [], {}
    for i in range(4):
        if i == 0:
            gather('w_uq')
            gather('w_ukv')
        else:
            WIN.append(gathered_cols(in_names[i]))
            gather(in_names[i + 1] if i < 3 else out_names[3])
        P = _mm(h, WIN[i], out_dtype=MXU, name=f"in{i}", hosts=hosts)
        if i == 0:
            w_uq_pad = jnp.pad(gathered_cols('w_uq').reshape(qr, H, NOPE + ROPE),
                               ((0, 0), (0, 0), (0, QPAD - NOPE - ROPE))).reshape(qr, H * QPAD)
            w_ukv3 = gathered_cols('w_ukv').reshape(kvr, H, NOPE + VHEAD)
            w_k_pad = jnp.pad(w_ukv3[:, :, :NOPE], ((0, 0), (0, 0), (0, QPAD - NOPE))).reshape(kvr, H * QPAD)
            w_v = w_ukv3[:, :, NOPE:].reshape(kvr, HV)
            cqn, ckvn = _mla_norms_fwd(P, mla_q_norm, mla_kv_norm, qr, kvr, "mla_norms")
            qpad = _mm(cqn, w_uq_pad, out_dtype=MXU, name="mla_q")
            kpad = _mm(ckvn, w_k_pad, out_dtype=MXU, name="mla_k")
            vv = _mm(ckvn, w_v, out_dtype=MXU, name="mla_v")
            qcat, kcat, vcat = _rope_fwd(qpad, kpad, vv, P, pos, invf, (qr + kvr) // LANE, H, "rope")
            for nm in (out_names[0], in_names[1]):
                gather(nm)
            o, act, lse = _flash_fwd(qcat, kcat, vcat, P, pa // VHEAD, H, "attn", hosts=hosts)
            gather(out_names[1])
            keep.update(cqn=cqn, ckvn=ckvn, qcat=qcat, kcat=kcat, v=vv, o=o, lse=lse)
        elif i == 1:
            act = _sc_fwd(P, CH['sc_conv'], D, "sc_mix")
        elif i == 2:
            act = _gm_fwd(P, CH['gm_ln_g'], CH['gm_ln_b'], gm_ws, gm_bst, D, "gm_mix")
        else:
            yc = _cf_conv_fwd(P, CH['cf_dw'], CH['cf_dw_b'], D, "cf_conv")
            act = _cf_act_fwd(yc, P, CH['cf_ln_g'], CH['cf_ln_b'], D, "cf_act")
            keep.update(yc=yc)
        WOUT.append(gathered_rows(out_names[i]))
        if i == 1:
            gather(out_names[2])
        y = _mm(act, WOUT[i], out_dtype=F32, name=f"out{i}", hosts=hosts)
        hs.append(h), Ps.append(P), acts.append(act), ys.append(y)
        if i < 3:
            x_next, h = _post_fwd(y, norm_post[i:i + 1], xin[i], f"post{i}", next_gain=norm_pre[i + 1:i + 2])
            xin.append(x_next)
        else:
            dx, loss_part = _post_fwd(y, norm_post[i:i + 1], xin[i], f"post{i}", target=tgt)
    loss = lax.psum(loss_part[0, 0], ("x", "y", "c"))

    hosts.flush("ag_late")
    dnorm_pre, dnorm_post, small_g, RS = [None] * 4, [None] * 4, {}, {}
    for i in (3, 2, 1, 0):
        dy, dnorm_post[i] = _rms_bwd(ys[i], dx, norm_post[i:i + 1], f"post_bwd{i}", out_dtype=MXU)
        da = _mm(dy, WOUT[i], tb=True, out_dtype=MXU, name=f"out_bwd{i}", hosts=hosts)
        dw_out = _mm(acts[i], dy, ta=True, out_dtype=WIRE, name=f"out_dw{i}", hosts=hosts).reshape(N_DEV, -1, D)
        _queue_reduce_scatter(hosts, {out_names[i]: dw_out}, core, f"rs_out{i}", RS)
        P = Ps[i]
        if i == 3:
            dyc, d_z, small_g['cf_ln_g'], small_g['cf_ln_b'] = _cf_act_bwd(
                keep['yc'], P, da, CH['cf_ln_g'], CH['cf_ln_b'], D, "cf_act_bwd")
            d_a, d_g, small_g['cf_dw'], small_g['cf_dw_b'] = _cf_conv_bwd(P, dyc, CH['cf_dw'], D, "cf_conv_bwd")
            dP = jnp.concatenate([d_a, d_g, d_z], axis=1)
        elif i == 2:
            dP, small_g['gm_ln_g'], small_g['gm_ln_b'], small_g['gm_w_s'], dbs = _gm_bwd(
                P, da, CH['gm_ln_g'], CH['gm_ln_b'], gm_ws, gm_bst, D, "gm_mix_bwd")
            small_g['gm_b_s'] = dbs[:, :, 0]
        elif i == 1:
            dP, small_g['sc_conv'] = _sc_bwd(P, da, CH['sc_conv'], D, "sc_mix_bwd")
        else:
            do, d_z, delta = _mla_gate_bwd(da, keep['o'], P, pa, H, "gate_bwd")
            dqcat, dkcat, dv = _flash_bwd(keep['qcat'], keep['kcat'], keep['v'], do, keep['lse'], delta, H, "attn_bwd",
                                          hosts=hosts)
            dqpad, dkr = _rope_bwd(dqcat, dkcat, pos, invf, H, "rope_bwd")
            dcqn = _mm(dqpad, w_uq_pad, tb=True, out_dtype=F32, name="mla_q_bwd", hosts=hosts)
            dckvn_k = _mm(dkcat, w_k_pad, tb=True, out_dtype=F32, name="mla_k_bwd", hosts=hosts)
            dckvn_v = _mm(dv, w_v, tb=True, out_dtype=F32, name="mla_v_bwd", hosts=hosts)
            dw_uq_pad = _mm(keep['cqn'], dqpad, ta=True, out_dtype=F32, name="mla_q_dw", hosts=hosts)
            dw_k_pad = _mm(keep['ckvn'], dkcat, ta=True, out_dtype=F32, name="mla_k_dw", hosts=hosts)
            dw_v = _mm(keep['ckvn'], dv, ta=True, out_dtype=F32, name="mla_v_dw", hosts=hosts)
            dw_uq = _cols_blocks(dw_uq_pad.reshape(qr, H, QPAD)[:, :, :NOPE + ROPE].reshape(qr, H * (NOPE + ROPE)))
            dw_ukv = _cols_blocks(jnp.concatenate(
                [dw_k_pad.reshape(kvr, H, QPAD)[:, :, :NOPE], dw_v.reshape(kvr, H, VHEAD)], axis=2).reshape(kvr, H * (NOPE + VHEAD)))
            _queue_reduce_scatter(hosts, {'w_uq': dw_uq, 'w_ukv': dw_ukv}, core, "rs_lat", RS)
            d_pa, small_g['mla_q_norm'], small_g['mla_kv_norm'] = _mla_norms_bwd(
                P, dcqn, dckvn_k, dckvn_v, dkr, mla_q_norm, mla_kv_norm, qr, kvr, pa, "mla_norms_bwd")
            dP = jnp.concatenate([d_pa, d_z], axis=1)
        dh = _mm(dP, WIN[i], tb=True, out_dtype=MXU, name=f"in_bwd{i}", hosts=hosts)
        dx, dnorm_pre[i] = _rms_bwd(xin[i], dh, norm_pre[i:i + 1], f"pre_bwd{i}", resid=dx)
        if i == 0:
            dw_cat = _mm(hs[i], dP, ta=True, out_dtype=F32, name="in_dw0", hosts=hosts)
            dw_in = _cols_blocks(jnp.concatenate([dw_cat[:, :c3], dw_cat[:, pa:]], axis=1))
        else:
            dw_in = _mm(hs[i], dP, ta=True, out_dtype=WIRE, name=f"in_dw{i}", blocks=N_DEV, hosts=hosts)
        _queue_reduce_scatter(hosts, {in_names[i]: dw_in}, core, f"rs_in{i}", RS)
    grad_x = dx[None]

    small_g['norm_pre'] = jnp.concatenate(dnorm_pre, axis=0)
    small_g['norm_post'] = jnp.concatenate(dnorm_post, axis=0)
    small_names = repl + chan
    small_shapes = [(small_g[nm].shape) for nm in small_names]
    tail, big_out = {}, {}
    hosts.add(_GatherJob([_pack([small_g[nm] for nm in small_names])]), lambda outs: tail.update(grads=outs[0]))

    def adam_big(nm, carrier=None):
        if nm not in RS:
            hosts.flush("rs_late")
        big_out[nm] = [o[None] for o in _adam(W[nm][0], Mo[nm][0], Vo[nm][0], RS[nm], "adam_" + nm, hosts=carrier)]

    for nm in ('w_in_sc', 'w_in_gm', 'w_in_cf', 'w_out_sc', 'w_out_gm', 'w_out_cf'):
        adam_big(nm, hosts)
    hosts.flush("rs_late")
    for nm in ('w_in_mla', 'w_uq', 'w_ukv', 'w_out_mla'):
        adam_big(nm)
    totals = dict(zip(small_names, _unpack(_sum_blocks(tail['grads'], "sum_grads"), small_shapes)))
    local_g = []
    for nm in small_names:
        g = totals[nm]
        if nm in chan:
            g = lax.dynamic_slice_in_dim(g.reshape(-1, D), me * cs, cs, axis=1)
        local_g.append(g.reshape(W[nm].shape))
    local_shapes = [W[nm].shape for nm in small_names]
    sm = _adam(_pack([W[nm] for nm in small_names]), _pack([Mo[nm] for nm in small_names]),
               _pack([Vo[nm] for nm in small_names]), _pack(local_g)[None], "adam_small")
    small_out = [dict(zip(small_names, _unpack(buf, local_shapes))) for buf in sm]

    outs = [loss, grad_x]
    for k in range(4):
        outs += [big_out[nm][k] if nm in big_out else small_out[k][nm] for nm in names]
    return tuple(outs)
```
